```python
import jax, jax.numpy as jnp
from jax import lax
import numpy as np

D_MODEL = 1024
BATCH = 4
SEQ = 4096
DEPTH = 2
DEC_BATCH = 16
DEC_SEQ = 16
PAST_LEN = 4096

CHUNK = 64
D_MIX = 2 * D_MODEL
ML_WIDTH = D_MIX // 2
ML_HEADS = 4
ML_DV = ML_WIDTH // ML_HEADS
ML_DK = ML_DV // 2
LRU_WIDTH = D_MIX - ML_WIDTH
LRU_BLOCKS = 8
LRU_BW = LRU_WIDTH // LRU_BLOCKS
CONV_W = 4
LRU_C = 8.0
MEM_LEN = 256
XA_HEADS = 4
XA_DH = D_MODEL // XA_HEADS
D_FF = 4 * D_MODEL
EPS = 1e-6

OFF_Q = 0
OFF_K = OFF_Q + ML_HEADS * ML_DK
OFF_V = OFF_K + ML_HEADS * ML_DK
OFF_O = OFF_V + ML_WIDTH
OFF_I = OFF_O + ML_WIDTH
OFF_F = OFF_I + ML_HEADS
OFF_X = OFF_F + ML_HEADS
OFF_Y = OFF_X + LRU_WIDTH
D_IN = OFF_Y + LRU_WIDTH

kernel_name = 'hymba_mlstm_rglru_macaron_stream_step'


def rmsnorm(x, g):
    xf = x.astype(jnp.float32)
    y = xf * lax.rsqrt(jnp.mean(xf * xf, axis=-1, keepdims=True) + EPS) * g.astype(jnp.float32)
    return y.astype(x.dtype)


def swiglu(x, wg, wu, wd):
    return (jax.nn.silu(x @ wg) * (x @ wu)) @ wd


def mlstm_blocked(q, k, v, ig, lf, C0, n0, m0, chunk_len):
    B, T = q.shape[0], q.shape[1]
    nc = T // chunk_len

    def to_c(a):
        a = a.reshape((B, nc, chunk_len) + a.shape[2:])
        return jnp.swapaxes(jnp.moveaxis(a, 1, 0), 2, 3)

    tril = jnp.tril(jnp.ones((chunk_len, chunk_len), dtype=bool))

    def step(carry, inp):
        C, n, m = carry
        qc, kc, vc, igc, lfc = inp
        b = jnp.cumsum(lfc, axis=-1)
        inter = b + m[..., None]
        D = b[..., :, None] - b[..., None, :] + igc[..., None, :]
        D = jnp.where(tril, D, -jnp.inf)
        m_t = jnp.maximum(inter, jnp.max(D, axis=-1))
        S = jnp.einsum('bhtd,bhsd->bhts', qc, kc) * jnp.exp(D - m_t[..., None])
        w_inter = jnp.exp(inter - m_t)
        num = w_inter[..., None] * jnp.einsum('bhvd,bhtd->bhtv', C, qc) + jnp.einsum('bhts,bhsv->bhtv', S, vc)
        den = w_inter * jnp.einsum('bhd,bhtd->bht', n, qc) + jnp.sum(S, axis=-1)
        h = num / jnp.maximum(jnp.abs(den), jnp.exp(-m_t))[..., None]
        m_new = m_t[..., -1]
        wg = jnp.exp(b[..., -1:] - b + igc - m_new[..., None])
        decay = jnp.exp(b[..., -1] + m - m_new)
        C_new = decay[..., None, None] * C + jnp.einsum('bhs,bhsv,bhsd->bhvd', wg, vc, kc)
        n_new = decay[..., None] * n + jnp.einsum('bhs,bhsd->bhd', wg, kc)
        return (C_new, n_new, m_new), h

    (C, n, m), hs = lax.scan(step, (C0, n0, m0), (to_c(q), to_c(k), to_c(v), to_c(ig), to_c(lf)))
    hs = jnp.moveaxis(jnp.swapaxes(hs, 2, 3), 0, 1).reshape(B, T, ML_HEADS, ML_DV)
    return hs, C, n, m


def rglru(u, buf0, h0, conv_w, conv_b, w_a, b_a, w_i, b_i, lam):
    B, T = u.shape[0], u.shape[1]
    up = jnp.concatenate([buf0.astype(u.dtype), u], axis=1)
    c = conv_b + sum(up[:, j:j + T] * conv_w[j] for j in range(CONV_W))
    new_buf = up[:, up.shape[1] - (CONV_W - 1):]
    cf = c.astype(jnp.float32)
    cb = cf.reshape(B, T, LRU_BLOCKS, LRU_BW)
    r = jax.nn.sigmoid(jnp.einsum('btnc,ncd->btnd', cb, w_a.astype(jnp.float32)) + b_a).reshape(B, T, LRU_WIDTH)
    i = jax.nn.sigmoid(jnp.einsum('btnc,ncd->btnd', cb, w_i.astype(jnp.float32)) + b_i).reshape(B, T, LRU_WIDTH)
    log_a = -LRU_C * r * jax.nn.softplus(-lam.astype(jnp.float32))
    a = jnp.exp(log_a)
    xin = jnp.sqrt(-jnp.expm1(2.0 * log_a)) * (i * cf)

    def step(h, ax):
        a_t, x_t = ax
        h = a_t * h + x_t
        return h, h

    hT, hs = lax.scan(step, h0.astype(jnp.float32), (jnp.swapaxes(a, 0, 1), jnp.swapaxes(xin, 0, 1)))
    return jnp.swapaxes(hs, 0, 1), hT, new_buf


def parallel_mixer(xn, C0, n0, m0, h0, buf0, p, chunk_len):
    B, T = xn.shape[0], xn.shape[1]
    z = xn @ p['w_in']
    f32 = jnp.float32
    q = z[..., OFF_Q:OFF_K].astype(f32).reshape(B, T, ML_HEADS, ML_DK) * (ML_DK ** -0.5)
    k = z[..., OFF_K:OFF_V].astype(f32).reshape(B, T, ML_HEADS, ML_DK)
    v = z[..., OFF_V:OFF_O].astype(f32).reshape(B, T, ML_HEADS, ML_DV)
    o = jax.nn.sigmoid(z[..., OFF_O:OFF_I].astype(f32)).reshape(B, T, ML_HEADS, ML_DV)
    ig = z[..., OFF_I:OFF_F].astype(f32) + p['ml_b_i']
    lf = jax.nn.log_sigmoid(z[..., OFF_F:OFF_X].astype(f32) + p['ml_b_f'])
    hm, C, n, m = mlstm_blocked(q, k, v, ig, lf, C0.astype(f32), n0.astype(f32), m0.astype(f32), chunk_len)
    hm = hm * lax.rsqrt(jnp.mean(hm * hm, axis=-1, keepdims=True) + EPS) * p['ml_out_norm']
    hm = (o * hm).reshape(B, T, ML_WIDTH)
    u = z[..., OFF_X:OFF_Y]
    ygate = jax.nn.gelu(z[..., OFF_Y:D_IN].astype(f32))
    hl, hT, new_buf = rglru(u, buf0, h0, p['lru_conv_w'], p['lru_conv_b'], p['lru_w_a'], p['lru_b_a'],
                            p['lru_w_i'], p['lru_b_i'], p['lru_lambda'])
    hl = ygate * hl
    cat = jnp.concatenate([hm, hl], axis=-1).astype(xn.dtype)
    out = cat @ p['w_out']
    dt = xn.dtype
    return out, (C.astype(dt), n.astype(dt), m.astype(dt), hT.astype(dt), new_buf.astype(dt))


def mem_kv(mem, g, wk, wv):
    B = mem.shape[0]
    mn = rmsnorm(mem, g)
    k = (mn @ wk).reshape(B, MEM_LEN, XA_HEADS, XA_DH)
    v = (mn @ wv).reshape(B, MEM_LEN, XA_HEADS, XA_DH)
    return k, v


def memory_xattn(xn, mk, mv, wq, wo):
    B, T = xn.shape[0], xn.shape[1]
    q = (xn @ wq).reshape(B, T, XA_HEADS, XA_DH)
    s = jnp.einsum('bthd,bmhd->bhtm', q.astype(jnp.float32), mk.astype(jnp.float32)) * (XA_DH ** -0.5)
    pr = jax.nn.softmax(s, axis=-1)
    o = jnp.einsum('bhtm,bmhd->bthd', pr, mv.astype(jnp.float32)).reshape(B, T, D_MODEL).astype(xn.dtype)
    return o @ wo


def layer(x, mk, mv, C0, n0, m0, h0, buf0, p, chunk_len):
    x = x + 0.5 * swiglu(rmsnorm(x, p['ffn1_norm']), p['ffn1_w_gate'], p['ffn1_w_up'], p['ffn1_w_down'])
    mix, st = parallel_mixer(rmsnorm(x, p['mix_norm']), C0, n0, m0, h0, buf0, p, chunk_len)
    x = x + mix
    x = x + memory_xattn(rmsnorm(x, p['xattn_norm']), mk, mv, p['xattn_w_q'], p['xattn_w_o'])
    x = x + 0.5 * swiglu(rmsnorm(x, p['ffn2_norm']), p['ffn2_w_gate'], p['ffn2_w_up'], p['ffn2_w_down'])
    return x, st


def setup_inputs(seed: int = 0) -> dict:
    key = jax.random.key(seed)
    ks = iter(jax.random.split(key, 64))
    nrm = lambda shape, s=1.0: jax.random.normal(next(ks), shape, jnp.float32) * s
    gain = lambda shape: 1.0 + nrm(shape, 0.05)
    u = jax.random.uniform(next(ks), (DEPTH, LRU_WIDTH), jnp.float32, 0.9, 0.999)
    sl = u ** (1.0 / LRU_C)
    lam = jnp.log(sl / (1.0 - sl))
    b_f = jnp.linspace(3.0, 6.0, ML_HEADS, dtype=jnp.float32)[None, :] + nrm((DEPTH, ML_HEADS), 0.1)
    return {
        'x_prompt': nrm((BATCH, SEQ, D_MODEL)),
        'x_sample': nrm((DEC_BATCH, DEC_SEQ, D_MODEL)),
        'mem_prompt': nrm((BATCH, MEM_LEN, D_MODEL)),
        'state_mlstm_C': nrm((DEPTH, DEC_BATCH, ML_HEADS, ML_DV, ML_DK), 0.5),
        'state_mlstm_n': nrm((DEPTH, DEC_BATCH, ML_HEADS, ML_DK), 0.5),
        'state_mlstm_m': nrm((DEPTH, DEC_BATCH, ML_HEADS), 0.5),
        'state_lru_h': nrm((DEPTH, DEC_BATCH, LRU_WIDTH), 0.5),
        'state_lru_conv': nrm((DEPTH, DEC_BATCH, CONV_W - 1, LRU_WIDTH)),
        'cache_mem_k': nrm((DEPTH, DEC_BATCH, MEM_LEN, XA_HEADS, XA_DH)),
        'cache_mem_v': nrm((DEPTH, DEC_BATCH, MEM_LEN, XA_HEADS, XA_DH)),
        'ffn1_norm': gain((DEPTH, D_MODEL)),
        'ffn1_w_gate': nrm((DEPTH, D_MODEL, D_FF), D_MODEL ** -0.5),
        'ffn1_w_up': nrm((DEPTH, D_MODEL, D_FF), D_MODEL ** -0.5),
        'ffn1_w_down': nrm((DEPTH, D_FF, D_MODEL), D_FF ** -0.5),
        'mix_norm': gain((DEPTH, D_MODEL)),
        'w_in': nrm((DEPTH, D_MODEL, D_IN), D_MODEL ** -0.5),
        'ml_b_i': nrm((DEPTH, ML_HEADS), 0.1),
        'ml_b_f': b_f,
        'ml_out_norm': gain((DEPTH, ML_HEADS, ML_DV)),
        'lru_conv_w': nrm((DEPTH, CONV_W, LRU_WIDTH), 0.5),
        'lru_conv_b': nrm((DEPTH, LRU_WIDTH), 0.02),
        'lru_w_a': nrm((DEPTH, LRU_BLOCKS, LRU_BW, LRU_BW), LRU_BW ** -0.5),
        'lru_b_a': nrm((DEPTH, LRU_BLOCKS, LRU_BW), 0.02),
        'lru_w_i': nrm((DEPTH, LRU_BLOCKS, LRU_BW, LRU_BW), LRU_BW ** -0.5),
        'lru_b_i': nrm((DEPTH, LRU_BLOCKS, LRU_BW), 0.02),
        'lru_lambda': lam,
        'w_out': nrm((DEPTH, D_MIX, D_MODEL), D_MIX ** -0.5),
        'xattn_norm': gain((DEPTH, D_MODEL)),
        'mem_norm': gain((DEPTH, D_MODEL)),
        'xattn_w_q': nrm((DEPTH, D_MODEL, D_MODEL), D_MODEL ** -0.5),
        'xattn_w_k': nrm((DEPTH, D_MODEL, D_MODEL), D_MODEL ** -0.5),
        'xattn_w_v': nrm((DEPTH, D_MODEL, D_MODEL), D_MODEL ** -0.5),
        'xattn_w_o': nrm((DEPTH, D_MODEL, D_MODEL), D_MODEL ** -0.5),
        'ffn2_norm': gain((DEPTH, D_MODEL)),
        'ffn2_w_gate': nrm((DEPTH, D_MODEL, D_FF), D_MODEL ** -0.5),
        'ffn2_w_up': nrm((DEPTH, D_MODEL, D_FF), D_MODEL ** -0.5),
        'ffn2_w_down': nrm((DEPTH, D_FF, D_MODEL), D_FF ** -0.5),
        'final_norm': gain((D_MODEL,)),
    }


def reference(x_prompt, x_sample, mem_prompt, state_mlstm_C, state_mlstm_n, state_mlstm_m, state_lru_h,
              state_lru_conv, cache_mem_k, cache_mem_v, ffn1_norm, ffn1_w_gate, ffn1_w_up, ffn1_w_down,
              mix_norm, w_in, ml_b_i, ml_b_f, ml_out_norm, lru_conv_w, lru_conv_b, lru_w_a, lru_b_a,
              lru_w_i, lru_b_i, lru_lambda, w_out, xattn_norm, mem_norm, xattn_w_q, xattn_w_k, xattn_w_v,
              xattn_w_o, ffn2_norm, ffn2_w_gate, ffn2_w_up, ffn2_w_down, final_norm):
    f32 = jnp.float32
    xp, xs = x_prompt, x_sample
    Bp, Bs = x_prompt.shape[0], x_sample.shape[0]
    pC, pn, pm, ph, pconv, pk, pv = [], [], [], [], [], [], []
    sC, sn, sm, sh, sconv = [], [], [], [], []
    for l in range(DEPTH):
        p = {
            'ffn1_norm': ffn1_norm[l], 'ffn1_w_gate': ffn1_w_gate[l], 'ffn1_w_up': ffn1_w_up[l],
            'ffn1_w_down': ffn1_w_down[l], 'mix_norm': mix_norm[l], 'w_in': w_in[l],
            'ml_b_i': ml_b_i[l].astype(f32), 'ml_b_f': ml_b_f[l].astype(f32),
            'ml_out_norm': ml_out_norm[l].astype(f32), 'lru_conv_w': lru_conv_w[l],
            'lru_conv_b': lru_conv_b[l], 'lru_w_a': lru_w_a[l], 'lru_b_a': lru_b_a[l].astype(f32),
            'lru_w_i': lru_w_i[l], 'lru_b_i': lru_b_i[l].astype(f32), 'lru_lambda': lru_lambda[l],
            'w_out': w_out[l], 'xattn_norm': xattn_norm[l], 'xattn_w_q': xattn_w_q[l],
            'xattn_w_o': xattn_w_o[l], 'ffn2_norm': ffn2_norm[l], 'ffn2_w_gate': ffn2_w_gate[l],
            'ffn2_w_up': ffn2_w_up[l], 'ffn2_w_down': ffn2_w_down[l],
        }
        mk, mv = mem_kv(mem_prompt, mem_norm[l], xattn_w_k[l], xattn_w_v[l])
        xp, (C, n, m, h, buf) = layer(
            xp, mk, mv,
            jnp.zeros((Bp, ML_HEADS, ML_DV, ML_DK), f32), jnp.zeros((Bp, ML_HEADS, ML_DK), f32),
            jnp.zeros((Bp, ML_HEADS), f32), jnp.zeros((Bp, LRU_WIDTH), f32),
            jnp.zeros((Bp, CONV_W - 1, LRU_WIDTH), xp.dtype), p, CHUNK)
        pC.append(C); pn.append(n); pm.append(m); ph.append(h); pconv.append(buf)
        pk.append(mk); pv.append(mv)
        xs, (C, n, m, h, buf) = layer(
            xs, cache_mem_k[l], cache_mem_v[l], state_mlstm_C[l], state_mlstm_n[l], state_mlstm_m[l],
            state_lru_h[l], state_lru_conv[l], p, xs.shape[1])
        sC.append(C); sn.append(n); sm.append(m); sh.append(h); sconv.append(buf)
    y_prompt = rmsnorm(xp, final_norm)
    y_sample = rmsnorm(xs, final_norm)
    return (y_prompt, y_sample,
            jnp.stack(pC), jnp.stack(pn), jnp.stack(pm), jnp.stack(ph), jnp.stack(pconv),
            jnp.stack(pk), jnp.stack(pv),
            jnp.stack(sC), jnp.stack(sn), jnp.stack(sm), jnp.stack(sh), jnp.stack(sconv))
```

```python
import functools

import jax
import jax.numpy as jnp
from jax import lax
from jax.experimental import pallas as pl
from jax.experimental.pallas import tpu as pltpu

F32 = jnp.float32
BF16 = jnp.bfloat16

D_MODEL = 1024
DEPTH = 2
CHUNK = 64
ML_HEADS = 4
ML_DV = 256
ML_DK = 128
ML_WIDTH = ML_HEADS * ML_DV
LRU_WIDTH = 1024
LRU_BLOCKS = 8
LRU_BW = LRU_WIDTH // LRU_BLOCKS
CONV_W = 4
LRU_C = 8.0
MEM_LEN = 256
XA_HEADS = 4
XA_DH = D_MODEL // XA_HEADS
D_FF = 4 * D_MODEL
EPS = 1e-6

Z_Q = 0
Z_K = Z_Q + ML_HEADS * ML_DK
Z_V = Z_K + ML_HEADS * ML_DK
Z_O = Z_V + ML_WIDTH
Z_X = Z_O + ML_WIDTH
Z_Y = Z_X + LRU_WIDTH
Z_MAIN = Z_Y + LRU_WIDTH
LANES = 128
SUBLANES = 8
VMEM_LIMIT = 56 * 1024 * 1024


def _rms(x, g):
    return x * lax.rsqrt(jnp.mean(x * x, axis=-1, keepdims=True) + EPS) * g


def _nt_dot(a, b):
    return lax.dot_general(a, b, (((1,), (1,)), ((), ())), preferred_element_type=F32)


def _tn_dot(a, b):
    return lax.dot_general(a, b, (((0,), (0,)), ((), ())), preferred_element_type=F32)


def _ffn_kernel(x_ref, g_ref, wg_ref, wu_ref, wd_ref, fg_ref, o_ref, xn_s, acc_s, *, final_norm):
    j = pl.program_id(1)

    @pl.when(j == 0)
    def _():
        xn_s[...] = _rms(x_ref[...], g_ref[...]).astype(BF16)
        acc_s[...] = jnp.zeros_like(acc_s)

    xn = xn_s[...]
    g = jnp.dot(xn, wg_ref[...], preferred_element_type=F32)
    u = jnp.dot(xn, wu_ref[...], preferred_element_type=F32)
    h = (g * jax.nn.sigmoid(g) * u).astype(BF16)
    acc_s[...] += jnp.dot(h, wd_ref[...], preferred_element_type=F32)

    @pl.when(j == pl.num_programs(1) - 1)
    def _():
        y = x_ref[...] + 0.5 * acc_s[...]
        if final_norm:
            y = _rms(y, fg_ref[...])
        o_ref[...] = y


def _ffn(x, g, wg, wu, wd, fg, *, tm, tf, final_norm):
    m = x.shape[0]
    return pl.pallas_call(
        functools.partial(_ffn_kernel, final_norm=final_norm),
        out_shape=jax.ShapeDtypeStruct((m, D_MODEL), F32),
        grid=(m // tm, D_FF // tf),
        in_specs=[
            pl.BlockSpec((tm, D_MODEL), lambda i, j: (i, 0)),
            pl.BlockSpec((1, D_MODEL), lambda i, j: (0, 0)),
            pl.BlockSpec((D_MODEL, tf), lambda i, j: (0, j)),
            pl.BlockSpec((D_MODEL, tf), lambda i, j: (0, j)),
            pl.BlockSpec((tf, D_MODEL), lambda i, j: (j, 0)),
            pl.BlockSpec((1, D_MODEL), lambda i, j: (0, 0)),
        ],
        out_specs=pl.BlockSpec((tm, D_MODEL), lambda i, j: (i, 0)),
        scratch_shapes=[pltpu.VMEM((tm, D_MODEL), BF16), pltpu.VMEM((tm, D_MODEL), F32)],
        compiler_params=pltpu.CompilerParams(
            dimension_semantics=("parallel", "arbitrary"), vmem_limit_bytes=VMEM_LIMIT),
        name="ffn",
    )(x, g, wg, wu, wd, fg)


def _zproj_kernel(x_ref, g_ref, w_ref, wgate_ref, z_ref, gate_ref, xn_s):
    j = pl.program_id(1)

    @pl.when(j == 0)
    def _():
        xn = _rms(x_ref[...], g_ref[...]).astype(BF16)
        xn_s[...] = xn
        gate_ref[...] = jnp.dot(xn, wgate_ref[...], preferred_element_type=F32)

    z_ref[...] = jnp.dot(xn_s[...], w_ref[...], preferred_element_type=F32)


def _zproj(x, g, w_main, w_gate, *, tm, tn):
    m = x.shape[0]
    return pl.pallas_call(
        _zproj_kernel,
        out_shape=(jax.ShapeDtypeStruct((m, Z_MAIN), F32), jax.ShapeDtypeStruct((m, LANES), F32)),
        grid=(m // tm, Z_MAIN // tn),
        in_specs=[
            pl.BlockSpec((tm, D_MODEL), lambda i, j: (i, 0)),
            pl.BlockSpec((1, D_MODEL), lambda i, j: (0, 0)),
            pl.BlockSpec((D_MODEL, tn), lambda i, j: (0, j)),
            pl.BlockSpec((D_MODEL, LANES), lambda i, j: (0, 0)),
        ],
        out_specs=(pl.BlockSpec((tm, tn), lambda i, j: (i, j)),
                   pl.BlockSpec((tm, LANES), lambda i, j: (i, 0))),
        scratch_shapes=[pltpu.VMEM((tm, D_MODEL), BF16)],
        compiler_params=pltpu.CompilerParams(
            dimension_semantics=("parallel", "arbitrary"), vmem_limit_bytes=VMEM_LIMIT),
        name="zproj",
    )(x, g, w_main, w_gate)


def _log_sigmoid(x):
    return jnp.minimum(x, 0.0) - jnp.log1p(jnp.exp(-jnp.abs(x)))


def _softplus(x):
    return jnp.maximum(x, 0.0) + jnp.log1p(jnp.exp(-jnp.abs(x)))


def _lane_pick(x, lane_ids, idx):
    return jnp.sum(jnp.where(lane_ids == idx, x, 0.0), axis=1, keepdims=True)


def _mlstm_chunk(z_ref, gt_ref, gbias_ref, onorm_ref, c_ref, n_ref, m_ref, hm_s, r0, cl):
    lane = lax.broadcasted_iota(jnp.int32, (cl, LANES), 1)
    row = lax.broadcasted_iota(jnp.int32, (cl, LANES), 0)
    tril = (lax.broadcasted_iota(jnp.int32, (cl, cl), 1)
            <= lax.broadcasted_iota(jnp.int32, (cl, cl), 0))
    lane1 = lax.broadcasted_iota(jnp.int32, (1, LANES), 1)

    gates = gt_ref[r0:r0 + cl, :] + gbias_ref[...]
    bsum = _log_sigmoid(gates)
    d = 1
    while d < cl:
        bsum = bsum + jnp.where(row >= d, pltpu.roll(bsum, d, axis=0), 0.0)
        d *= 2
    mixed = jnp.where(lane < ML_HEADS, gates, bsum)
    mixed = jnp.concatenate([mixed, jnp.zeros((LANES - cl, LANES), F32)], axis=0)
    mixed_t = mixed.T

    m_vec = m_ref[0]
    m_out = m_vec
    for h in range(ML_HEADS):
        b_col = _lane_pick(bsum, lane, ML_HEADS + h)
        ig_col = _lane_pick(gates, lane, h)
        b_row = mixed_t[ML_HEADS + h:ML_HEADS + h + 1, 0:cl]
        ig_row = mixed_t[h:h + 1, 0:cl]
        m_prev = _lane_pick(m_vec, lane1, h)

        dmat = jnp.where(tril, b_col - b_row + ig_row, -jnp.inf)
        inter = b_col + m_prev
        m_t = jnp.maximum(inter, jnp.max(dmat, axis=1, keepdims=True))

        qf = z_ref[r0:r0 + cl, Z_Q + h * ML_DK:Z_Q + (h + 1) * ML_DK] * (ML_DK ** -0.5)
        kf = z_ref[r0:r0 + cl, Z_K + h * ML_DK:Z_K + (h + 1) * ML_DK]
        vf = z_ref[r0:r0 + cl, Z_V + h * ML_DV:Z_V + (h + 1) * ML_DV]
        q = qf.astype(BF16)
        k = kf.astype(BF16)

        s = _nt_dot(q, k) * jnp.exp(dmat - m_t)
        w_inter = jnp.exp(inter - m_t)
        c_old = c_ref[0, h]
        n_old = n_ref[0, h:h + 1, :]
        num = w_inter * _nt_dot(q, c_old.astype(BF16)) + jnp.dot(
            s.astype(BF16), vf.astype(BF16), preferred_element_type=F32)
        den = (w_inter * jnp.sum(qf * n_old, axis=1, keepdims=True)
               + jnp.sum(s, axis=1, keepdims=True))
        hh = num / jnp.maximum(jnp.abs(den), jnp.exp(-m_t))

        hh = hh * lax.rsqrt(jnp.mean(hh * hh, axis=1, keepdims=True) + EPS) * onorm_ref[h:h + 1, :]
        o_gate = jax.nn.sigmoid(z_ref[r0:r0 + cl, Z_O + h * ML_DV:Z_O + (h + 1) * ML_DV])
        hm_s[r0:r0 + cl, h * ML_DV:(h + 1) * ML_DV] = (o_gate * hh).astype(BF16)

        m_new = m_t[cl - 1:cl, :]
        b_last = b_col[cl - 1:cl, :]
        wgt = jnp.exp(b_last - b_col + ig_col - m_new)
        decay = jnp.exp(b_last + m_prev - m_new)
        c_ref[0, h] = decay * c_old + _tn_dot((wgt * vf).astype(BF16), k)
        n_ref[0, h:h + 1, :] = decay * n_old + jnp.sum(wgt * kf, axis=0, keepdims=True)
        m_out = jnp.where(lane1 == h, m_new, m_out)
    m_ref[0] = m_out


def _mixer_kernel(z_ref, gt_ref, x_ref, c0_ref, n0_ref, m0_ref, h0_ref, cv0_ref,
                  gbias_ref, onorm_ref, convw_ref, convb_ref, wai_ref, ba_ref, bi_ref, lam_ref,
                  wout_ref,
                  y_ref, c_ref, n_ref, m_ref, h_ref, cv_ref,
                  ubuf_s, hm_s, *, cl, tt):
    @pl.when(pl.program_id(1) == 0)
    def _():
        c_ref[...] = c0_ref[...]
        n_ref[...] = n0_ref[...]
        m_ref[...] = m0_ref[...]
        h_ref[...] = h0_ref[...]
        cv_ref[...] = cv0_ref[...]

    for c in range(tt // cl):
        _mlstm_chunk(z_ref, gt_ref, gbias_ref, onorm_ref, c_ref, n_ref, m_ref, hm_s, c * cl, cl)

    u = z_ref[:, Z_X:Z_X + LRU_WIDTH]
    ubuf_s[0:SUBLANES, :] = cv_ref[0]
    ubuf_s[SUBLANES:SUBLANES + tt, :] = u
    cv_ref[0] = ubuf_s[tt:tt + SUBLANES, :]
    conv = convb_ref[...] + u * convw_ref[CONV_W - 1:CONV_W, :]
    for j in range(1, CONV_W):
        conv = conv + (ubuf_s[pl.ds(SUBLANES - j, tt), :]
                       * convw_ref[CONV_W - 1 - j:CONV_W - j, :])
    conv_b = conv.astype(BF16)
    pre = [jnp.dot(conv_b[:, n * LRU_BW:(n + 1) * LRU_BW], wai_ref[n], preferred_element_type=F32)
           for n in range(LRU_BLOCKS)]
    r_gate = jax.nn.sigmoid(jnp.concatenate([p[:, :LRU_BW] for p in pre], axis=1) + ba_ref[...])
    i_gate = jax.nn.sigmoid(jnp.concatenate([p[:, LRU_BW:] for p in pre], axis=1) + bi_ref[...])
    log_a = -LRU_C * r_gate * _softplus(-lam_ref[...])
    a = jnp.exp(log_a)
    xin = jnp.sqrt(-jnp.tanh(log_a) * (a * a + 1.0)) * (i_gate * conv)

    rmod = lax.broadcasted_iota(jnp.int32, (tt, LRU_WIDTH), 0) & (SUBLANES - 1)
    d = 1
    while d < SUBLANES:
        keep = rmod >= d
        xin = xin + a * jnp.where(keep, pltpu.roll(xin, d, axis=0), 0.0)
        a = a * jnp.where(keep, pltpu.roll(a, d, axis=0), 1.0)
        d *= 2
    carry = h_ref[0]
    groups = []
    for gidx in range(tt // SUBLANES):
        lo = gidx * SUBLANES
        hb = xin[lo:lo + SUBLANES, :] + a[lo:lo + SUBLANES, :] * carry
        carry = hb[SUBLANES - 1:SUBLANES, :]
        groups.append(hb)
    h_ref[0] = carry
    h_lru = jnp.concatenate(groups, axis=0)

    hl = (jax.nn.gelu(z_ref[:, Z_Y:Z_Y + LRU_WIDTH]) * h_lru).astype(BF16)
    out = (jnp.dot(hm_s[...], wout_ref[0:ML_WIDTH, :], preferred_element_type=F32)
           + jnp.dot(hl, wout_ref[ML_WIDTH:ML_WIDTH + LRU_WIDTH, :], preferred_element_type=F32))
    y_ref[...] = x_ref[...] + out


def _mixer(z, gt, x, c0, n0, m0, h0, cv0, gbias, onorm, convw, convb, wai, ba, bi, lam, wout,
           *, nb, t, tt, cl):
    nt = t // tt
    tok = lambda b, i: (b * nt + i, 0)
    st4 = lambda b, i: (b, 0, 0, 0)
    st3 = lambda b, i: (b, 0, 0)
    c2 = lambda b, i: (0, 0)
    c3 = lambda b, i: (0, 0, 0)
    return pl.pallas_call(
        functools.partial(_mixer_kernel, cl=cl, tt=tt),
        out_shape=(
            jax.ShapeDtypeStruct((nb * t, D_MODEL), F32),
            jax.ShapeDtypeStruct((nb, ML_HEADS, ML_DV, ML_DK), F32),
            jax.ShapeDtypeStruct((nb, ML_HEADS, ML_DK), F32),
            jax.ShapeDtypeStruct((nb, 1, LANES), F32),
            jax.ShapeDtypeStruct((nb, 1, LRU_WIDTH), F32),
            jax.ShapeDtypeStruct((nb, SUBLANES, LRU_WIDTH), F32),
        ),
        grid=(nb, nt),
        in_specs=[
            pl.BlockSpec((tt, Z_MAIN), tok),
            pl.BlockSpec((tt, LANES), tok),
            pl.BlockSpec((tt, D_MODEL), tok),
            pl.BlockSpec((1, ML_HEADS, ML_DV, ML_DK), st4),
            pl.BlockSpec((1, ML_HEADS, ML_DK), st3),
            pl.BlockSpec((1, 1, LANES), st3),
            pl.BlockSpec((1, 1, LRU_WIDTH), st3),
            pl.BlockSpec((1, SUBLANES, LRU_WIDTH), st3),
            pl.BlockSpec((1, LANES), c2),
            pl.BlockSpec((ML_HEADS, ML_DV), c2),
            pl.BlockSpec((CONV_W, LRU_WIDTH), c2),
            pl.BlockSpec((1, LRU_WIDTH), c2),
            pl.BlockSpec((LRU_BLOCKS, LRU_BW, 2 * LRU_BW), c3),
            pl.BlockSpec((1, LRU_WIDTH), c2),
            pl.BlockSpec((1, LRU_WIDTH), c2),
            pl.BlockSpec((1, LRU_WIDTH), c2),
            pl.BlockSpec((ML_WIDTH + LRU_WIDTH, D_MODEL), c2),
        ],
        out_specs=(
            pl.BlockSpec((tt, D_MODEL), tok),
            pl.BlockSpec((1, ML_HEADS, ML_DV, ML_DK), st4),
            pl.BlockSpec((1, ML_HEADS, ML_DK), st3),
            pl.BlockSpec((1, 1, LANES), st3),
            pl.BlockSpec((1, 1, LRU_WIDTH), st3),
            pl.BlockSpec((1, SUBLANES, LRU_WIDTH), st3),
        ),
        scratch_shapes=[pltpu.VMEM((tt + SUBLANES, LRU_WIDTH), F32),
                        pltpu.VMEM((tt, ML_WIDTH), BF16)],
        compiler_params=pltpu.CompilerParams(
            dimension_semantics=("parallel", "arbitrary"), vmem_limit_bytes=VMEM_LIMIT),
        name="mixer",
    )(z, gt, x, c0, n0, m0, h0, cv0, gbias, onorm, convw, convb, wai, ba, bi, lam, wout)


def _xattn_kernel(x_ref, g_ref, wq_ref, mk_ref, mv_ref, wo_ref, y_ref, o_s):
    x = x_ref[...]
    xn = _rms(x, g_ref[...]).astype(BF16)
    q = jnp.dot(xn, wq_ref[...], preferred_element_type=F32)
    for h in range(XA_HEADS):
        cols = slice(h * XA_DH, (h + 1) * XA_DH)
        s = _nt_dot(q[:, cols].astype(BF16), mk_ref[0, :, cols].astype(BF16)) * (XA_DH ** -0.5)
        p = jnp.exp(s - jnp.max(s, axis=1, keepdims=True))
        p = p / jnp.sum(p, axis=1, keepdims=True)
        o_s[:, cols] = jnp.dot(p.astype(BF16), mv_ref[0, :, cols].astype(BF16),
                               preferred_element_type=F32).astype(BF16)
    y_ref[...] = x + jnp.dot(o_s[...], wo_ref[...], preferred_element_type=F32)


def _xattn(x, g, wq, mk, mv, wo, *, nb, t, tt):
    nt = t // tt
    tok = lambda b, i: (b * nt + i, 0)
    c2 = lambda b, i: (0, 0)
    mem = lambda b, i: (b, 0, 0)
    return pl.pallas_call(
        _xattn_kernel,
        out_shape=jax.ShapeDtypeStruct((nb * t, D_MODEL), F32),
        grid=(nb, nt),
        in_specs=[
            pl.BlockSpec((tt, D_MODEL), tok),
            pl.BlockSpec((1, D_MODEL), c2),
            pl.BlockSpec((D_MODEL, D_MODEL), c2),
            pl.BlockSpec((1, MEM_LEN, D_MODEL), mem),
            pl.BlockSpec((1, MEM_LEN, D_MODEL), mem),
            pl.BlockSpec((D_MODEL, D_MODEL), c2),
        ],
        out_specs=pl.BlockSpec((tt, D_MODEL), tok),
        scratch_shapes=[pltpu.VMEM((tt, D_MODEL), BF16)],
        compiler_params=pltpu.CompilerParams(
            dimension_semantics=("parallel", "arbitrary"), vmem_limit_bytes=VMEM_LIMIT),
        name="xattn",
    )(x, g, wq, mk, mv, wo)


def _memkv_kernel(mem_ref, g_ref, wk_ref, wv_ref, k_ref, v_ref):
    mn = _rms(mem_ref[0], g_ref[...]).astype(BF16)
    k_ref[0] = jnp.dot(mn, wk_ref[...], preferred_element_type=F32)
    v_ref[0] = jnp.dot(mn, wv_ref[...], preferred_element_type=F32)


def _memkv(mem, g, wk, wv):
    nb = mem.shape[0]
    blk = pl.BlockSpec((1, MEM_LEN, D_MODEL), lambda b: (b, 0, 0))
    wspec = pl.BlockSpec((D_MODEL, D_MODEL), lambda b: (0, 0))
    out = jax.ShapeDtypeStruct((nb, MEM_LEN, D_MODEL), F32)
    return pl.pallas_call(
        _memkv_kernel,
        out_shape=(out, out),
        grid=(nb,),
        in_specs=[blk, pl.BlockSpec((1, D_MODEL), lambda b: (0, 0)), wspec, wspec],
        out_specs=(blk, blk),
        compiler_params=pltpu.CompilerParams(
            dimension_semantics=("parallel",), vmem_limit_bytes=VMEM_LIMIT),
        name="memkv",
    )(mem, g, wk, wv)


def _layer(x, mk, mv, st, p, *, nb, t, cl, tiles, final_g):
    tm, tf, tn, tt_mix, tt_xa = tiles
    row = lambda v: v.reshape(1, -1)
    x = _ffn(x, row(p["ffn1_norm"]), p["ffn1_w_gate"], p["ffn1_w_up"], p["ffn1_w_down"],
             row(p["ffn1_norm"]), tm=tm, tf=tf, final_norm=False)
    z, gt = _zproj(x, row(p["mix_norm"]), p["w_main"], p["w_gate"], tm=tm, tn=tn)
    x, c_new, n_new, m_new, h_new, cv_new = _mixer(
        z, gt, x, *st, p["gbias"], p["ml_out_norm"], p["lru_conv_w"], row(p["lru_conv_b"]),
        p["w_ai"], row(p["lru_b_a"]), row(p["lru_b_i"]), row(p["lru_lambda"]), p["w_out"],
        nb=nb, t=t, tt=tt_mix, cl=cl)
    x = _xattn(x, row(p["xattn_norm"]), p["xattn_w_q"], mk, mv, p["xattn_w_o"],
               nb=nb, t=t, tt=tt_xa)
    x = _ffn(x, row(p["ffn2_norm"]), p["ffn2_w_gate"], p["ffn2_w_up"], p["ffn2_w_down"],
             row(final_g), tm=tm, tf=tf, final_norm=final_g is not None and p["is_last"])
    return x, (c_new, n_new, m_new, h_new, cv_new)


def _pad_m(m):
    return jnp.pad(m.astype(F32), ((0, 0), (0, LANES - ML_HEADS)))[:, None, :]


def _pad_conv(cv):
    return jnp.pad(cv.astype(F32), ((0, 0), (SUBLANES - (CONV_W - 1), 0), (0, 0)))


def kernel(x_prompt, x_sample, mem_prompt, state_mlstm_C, state_mlstm_n, state_mlstm_m, state_lru_h,
           state_lru_conv, cache_mem_k, cache_mem_v, ffn1_norm, ffn1_w_gate, ffn1_w_up, ffn1_w_down,
           mix_norm, w_in, ml_b_i, ml_b_f, ml_out_norm, lru_conv_w, lru_conv_b, lru_w_a, lru_b_a,
           lru_w_i, lru_b_i, lru_lambda, w_out, xattn_norm, mem_norm, xattn_w_q, xattn_w_k, xattn_w_v,
           xattn_w_o, ffn2_norm, ffn2_w_gate, ffn2_w_up, ffn2_w_down, final_norm):
    bp, tp, _ = x_prompt.shape
    bs, ts, _ = x_sample.shape
    xp = x_prompt.reshape(bp * tp, D_MODEL)
    xs = x_sample.reshape(bs * ts, D_MODEL)

    p_out = [[] for _ in range(7)]
    s_out = [[] for _ in range(5)]
    for l in range(DEPTH):
        bf = lambda w: w[l].astype(BF16)
        wl = w_in[l]
        w_main = jnp.concatenate([wl[:, 0:3072], wl[:, 3080:5128]], axis=1).astype(BF16)
        w_gate = jnp.pad(wl[:, 3072:3080], ((0, 0), (0, LANES - 2 * ML_HEADS))).astype(BF16)
        gbias = jnp.pad(jnp.concatenate([ml_b_i[l], ml_b_f[l]]).astype(F32),
                        (0, LANES - 2 * ML_HEADS)).reshape(1, LANES)
        p = {
            "ffn1_norm": ffn1_norm[l], "ffn1_w_gate": bf(ffn1_w_gate), "ffn1_w_up": bf(ffn1_w_up),
            "ffn1_w_down": bf(ffn1_w_down), "mix_norm": mix_norm[l], "w_main": w_main,
            "w_gate": w_gate, "gbias": gbias, "ml_out_norm": ml_out_norm[l].astype(F32),
            "lru_conv_w": lru_conv_w[l], "lru_conv_b": lru_conv_b[l],
            "w_ai": jnp.concatenate([lru_w_a[l], lru_w_i[l]], axis=-1).astype(BF16),
            "lru_b_a": lru_b_a[l].astype(F32).reshape(-1), "lru_b_i": lru_b_i[l].astype(F32).reshape(-1),
            "lru_lambda": lru_lambda[l].astype(F32), "w_out": bf(w_out),
            "xattn_norm": xattn_norm[l], "xattn_w_q": bf(xattn_w_q), "xattn_w_o": bf(xattn_w_o),
            "ffn2_norm": ffn2_norm[l], "ffn2_w_gate": bf(ffn2_w_gate), "ffn2_w_up": bf(ffn2_w_up),
            "ffn2_w_down": bf(ffn2_w_down), "is_last": l == DEPTH - 1,
        }
        mk, mv = _memkv(mem_prompt, mem_norm[l].reshape(1, -1), bf(xattn_w_k), bf(xattn_w_v))
        st0 = (jnp.zeros((bp, ML_HEADS, ML_DV, ML_DK), F32), jnp.zeros((bp, ML_HEADS, ML_DK), F32),
               jnp.zeros((bp, 1, LANES), F32), jnp.zeros((bp, 1, LRU_WIDTH), F32),
               jnp.zeros((bp, SUBLANES, LRU_WIDTH), F32))
        xp, st = _layer(xp, mk, mv, st0, p, nb=bp, t=tp, cl=CHUNK,
                        tiles=(1024, 512, 1024, 256, 512), final_g=final_norm)
        for acc, v in zip(p_out, (*st, mk, mv)):
            acc.append(v)
        st0 = (state_mlstm_C[l].astype(F32), state_mlstm_n[l].astype(F32), _pad_m(state_mlstm_m[l]),
               state_lru_h[l].astype(F32)[:, None, :], _pad_conv(state_lru_conv[l]))
        xs, st = _layer(xs, cache_mem_k[l].reshape(bs, MEM_LEN, D_MODEL),
                        cache_mem_v[l].reshape(bs, MEM_LEN, D_MODEL), st0, p, nb=bs, t=ts, cl=ts,
                        tiles=(bs * ts, 1024, 1024, ts, ts), final_g=final_norm)
        for acc, v in zip(s_out, st):
            acc.append(v)

    def states(acc):
        c, n, m, h, cv = (jnp.stack(a) for a in acc[:5])
        return c, n, m[:, :, 0, :ML_HEADS], h[:, :, 0, :], cv[:, :, SUBLANES - (CONV_W - 1):, :]

    pk = jnp.stack(p_out[5]).reshape(DEPTH, bp, MEM_LEN, XA_HEADS, XA_DH)
    pv = jnp.stack(p_out[6]).reshape(DEPTH, bp, MEM_LEN, XA_HEADS, XA_DH)
    return (xp.reshape(bp, tp, D_MODEL), xs.reshape(bs, ts, D_MODEL),
            *states(p_out), pk, pv, *states(s_out))
```

```python
import functools

import jax
import jax.numpy as jnp
from jax import lax
from jax.experimental import pallas as pl
from jax.experimental.pallas import tpu as pltpu

F32 = jnp.float32
BF16 = jnp.bfloat16

D_MODEL = 1024
DEPTH = 2
CHUNK = 64
ML_HEADS = 4
ML_DV = 256
ML_DK = 128
ML_WIDTH = ML_HEADS * ML_DV
LRU_WIDTH = 1024
LRU_BLOCKS = 8
LRU_BW = LRU_WIDTH // LRU_BLOCKS
CONV_W = 4
LRU_C = 8.0
MEM_LEN = 256
XA_HEADS = 4
XA_DH = D_MODEL // XA_HEADS
D_FF = 4 * D_MODEL
EPS = 1e-6

LANES = 128
SUBLANES = 8
VMEM_LIMIT = 56 * 1024 * 1024

Z_Q = 0
Z_K = Z_Q + ML_HEADS * ML_DK
Z_V = Z_K + ML_HEADS * ML_DK
Z_O = Z_V + ML_WIDTH
Z_G = Z_O + ML_WIDTH
Z_NAT = Z_G + LANES
Z_XY = 2 * LRU_WIDTH
W_I = Z_O + ML_WIDTH
W_X = W_I + 2 * ML_HEADS


def _rms(x, g):
    return x * lax.rsqrt(jnp.mean(x * x, axis=-1, keepdims=True) + EPS) * g


def _nt_dot(a, b):
    return lax.dot_general(a, b, (((1,), (1,)), ((), ())), preferred_element_type=F32)


def _tn_dot(a, b):
    return lax.dot_general(a, b, (((0,), (0,)), ((), ())), preferred_element_type=F32)


def _dot(a, b):
    return jnp.dot(a, b, preferred_element_type=F32)


def _params(*sem):
    return pltpu.CompilerParams(dimension_semantics=sem, vmem_limit_bytes=VMEM_LIMIT)


def _resident(shape, index_map):
    return pl.BlockSpec(shape, index_map, pipeline_mode=pl.Buffered(1))


def _ffn_kernel(x_ref, g_ref, wg_ref, wu_ref, wd_ref, fg_ref, o_ref, xn_s, acc_s, *, final_norm):
    j = pl.program_id(1)

    @pl.when(j == 0)
    def _():
        xn_s[...] = _rms(x_ref[...], g_ref[...]).astype(BF16)
        acc_s[...] = jnp.zeros_like(acc_s)

    xn = xn_s[...]
    g = _dot(xn, wg_ref[...])
    u = _dot(xn, wu_ref[...])
    h = (g * jax.nn.sigmoid(g) * u).astype(BF16)
    acc_s[...] += _dot(h, wd_ref[...])

    @pl.when(j == pl.num_programs(1) - 1)
    def _():
        y = x_ref[...] + 0.5 * acc_s[...]
        if final_norm:
            y = _rms(y, fg_ref[...])
        o_ref[...] = y


def _ffn(x, g, wg, wu, wd, fg, l, *, tm, tf, final_norm):
    m = x.shape[0]
    return pl.pallas_call(
        functools.partial(_ffn_kernel, final_norm=final_norm),
        out_shape=jax.ShapeDtypeStruct((m, D_MODEL), F32),
        grid=(m // tm, D_FF // tf),
        in_specs=[
            pl.BlockSpec((tm, D_MODEL), lambda i, j: (i, 0)),
            pl.BlockSpec((None, 1, D_MODEL), lambda i, j: (l, 0, 0)),
            pl.BlockSpec((None, D_MODEL, tf), lambda i, j: (l, 0, j)),
            pl.BlockSpec((None, D_MODEL, tf), lambda i, j: (l, 0, j)),
            pl.BlockSpec((None, tf, D_MODEL), lambda i, j: (l, j, 0)),
            pl.BlockSpec((1, D_MODEL), lambda i, j: (0, 0)),
        ],
        out_specs=pl.BlockSpec((tm, D_MODEL), lambda i, j: (i, 0)),
        scratch_shapes=[pltpu.VMEM((tm, D_MODEL), BF16), pltpu.VMEM((tm, D_MODEL), F32)],
        compiler_params=_params("parallel", "arbitrary"),
        name="ffn",
    )(x, g, wg, wu, wd, fg)


def _log_sigmoid(x):
    return jnp.minimum(x, 0.0) - jnp.log1p(jnp.exp(-jnp.abs(x)))


def _softplus(x):
    return jnp.maximum(x, 0.0) + jnp.log1p(jnp.exp(-jnp.abs(x)))


def _lane_pick(x, lane_ids, idx):
    return jnp.sum(jnp.where(lane_ids == idx, x, 0.0), axis=1, keepdims=True)


def _mlstm_chunk(z_ref, gbias_ref, onorm_ref, c_ref, n_ref, m_ref, hm_s, r0, cl):
    lane = lax.broadcasted_iota(jnp.int32, (cl, LANES), 1)
    row = lax.broadcasted_iota(jnp.int32, (cl, LANES), 0)
    tril = (lax.broadcasted_iota(jnp.int32, (cl, cl), 1)
            <= lax.broadcasted_iota(jnp.int32, (cl, cl), 0))
    lane1 = lax.broadcasted_iota(jnp.int32, (1, LANES), 1)

    gates = z_ref[r0:r0 + cl, Z_G:Z_G + LANES] + gbias_ref[...]
    bsum = _log_sigmoid(gates)
    d = 1
    while d < cl:
        bsum = bsum + jnp.where(row >= d, pltpu.roll(bsum, d, axis=0), 0.0)
        d *= 2
    mixed = jnp.where(lane < ML_HEADS, gates, bsum)
    mixed = jnp.concatenate([mixed, jnp.zeros((LANES - cl, LANES), F32)], axis=0)
    mixed_t = mixed.T

    m_vec = m_ref[0]
    m_out = m_vec
    for h in range(ML_HEADS):
        b_col = _lane_pick(bsum, lane, ML_HEADS + h)
        ig_col = _lane_pick(gates, lane, h)
        b_row = mixed_t[ML_HEADS + h:ML_HEADS + h + 1, 0:cl]
        ig_row = mixed_t[h:h + 1, 0:cl]
        m_prev = _lane_pick(m_vec, lane1, h)

        dmat = jnp.where(tril, b_col - b_row + ig_row, -jnp.inf)
        inter = b_col + m_prev
        m_t = jnp.maximum(inter, jnp.max(dmat, axis=1, keepdims=True))

        qf = z_ref[r0:r0 + cl, Z_Q + h * ML_DK:Z_Q + (h + 1) * ML_DK] * (ML_DK ** -0.5)
        kf = z_ref[r0:r0 + cl, Z_K + h * ML_DK:Z_K + (h + 1) * ML_DK]
        vf = z_ref[r0:r0 + cl, Z_V + h * ML_DV:Z_V + (h + 1) * ML_DV]
        q = qf.astype(BF16)
        k = kf.astype(BF16)

        s = _nt_dot(q, k) * jnp.exp(dmat - m_t)
        w_inter = jnp.exp(inter - m_t)
        c_old = c_ref[0, h]
        n_old = n_ref[0, h:h + 1, :]
        num = w_inter * _nt_dot(q, c_old.astype(BF16)) + _dot(s.astype(BF16), vf.astype(BF16))
        den = (w_inter * jnp.sum(qf * n_old, axis=1, keepdims=True)
               + jnp.sum(s, axis=1, keepdims=True))
        hh = num / jnp.maximum(jnp.abs(den), jnp.exp(-m_t))

        hh = hh * lax.rsqrt(jnp.mean(hh * hh, axis=1, keepdims=True) + EPS) * onorm_ref[h:h + 1, :]
        o_gate = jax.nn.sigmoid(z_ref[r0:r0 + cl, Z_O + h * ML_DV:Z_O + (h + 1) * ML_DV])
        hm_s[r0:r0 + cl, h * ML_DV:(h + 1) * ML_DV] = (o_gate * hh).astype(BF16)

        m_new = m_t[cl - 1:cl, :]
        b_last = b_col[cl - 1:cl, :]
        wgt = jnp.exp(b_last - b_col + ig_col - m_new)
        decay = jnp.exp(b_last + m_prev - m_new)
        c_ref[0, h] = decay * c_old + _tn_dot((wgt * vf).astype(BF16), k)
        n_ref[0, h:h + 1, :] = decay * n_old + jnp.sum(wgt * kf, axis=0, keepdims=True)
        m_out = jnp.where(lane1 == h, m_new, m_out)
    m_ref[0] = m_out


def _lru_gates(conv, wai_ref, ba_ref, bi_ref, lam_ref):
    conv_b = conv.astype(BF16)
    pre = [_dot(conv_b[:, n * LRU_BW:(n + 1) * LRU_BW], wai_ref[n])
           for n in range(LRU_BLOCKS)]
    r_gate = jax.nn.sigmoid(jnp.concatenate([p[:, :LRU_BW] for p in pre], axis=1) + ba_ref[...])
    i_gate = jax.nn.sigmoid(jnp.concatenate([p[:, LRU_BW:] for p in pre], axis=1) + bi_ref[...])
    log_a = -LRU_C * r_gate * _softplus(-lam_ref[...])
    a = jnp.exp(log_a)
    xin = jnp.sqrt(-jnp.tanh(log_a) * (a * a + 1.0)) * (i_gate * conv)
    return a, xin


def _pmixer_kernel(x_ref, gmix_ref, wnat_ref, wxy_ref, gbias_ref, onorm_ref, convw_ref, convb_ref,
                   wai_ref, ba_ref, bi_ref, lam_ref, wout_ref,
                   y_ref, c_ref, n_ref, m_ref, h_ref, cv_ref,
                   z_s, pbuf, obuf, hm_s, *, cl, tt):
    seg = tt // SUBLANES
    pitch = seg + SUBLANES

    @pl.when(pl.program_id(1) == 0)
    def _():
        c_ref[...] = jnp.zeros_like(c_ref)
        n_ref[...] = jnp.zeros_like(n_ref)
        m_ref[...] = jnp.zeros_like(m_ref)
        h_ref[...] = jnp.zeros_like(h_ref)
        cv_ref[...] = jnp.zeros_like(cv_ref)

    x = x_ref[...]
    xn = _rms(x, gmix_ref[...])
    z_s[...] = _dot(xn.astype(BF16), wnat_ref[...])

    for k in range(LRU_WIDTH // LANES):
        for s in range(SUBLANES):
            pbuf[k, pl.ds(s * pitch, seg), :] = xn[s * seg:(s + 1) * seg, k * LANES:(k + 1) * LANES]
    xnp = jnp.stack(
        [jnp.concatenate([pbuf[k, pl.ds(j, SUBLANES, stride=pitch), :]
                          for k in range(LRU_WIDTH // LANES)], axis=1) for j in range(seg)], axis=0)
    zxy = _dot(xnp.reshape(tt, D_MODEL).astype(BF16), wxy_ref[...])

    u3 = zxy[:, 0:LRU_WIDTH].reshape(seg, SUBLANES, LRU_WIDTH)
    tail = cv_ref[0]
    sub = lax.broadcasted_iota(jnp.int32, (SUBLANES, LRU_WIDTH), 0)
    wrap = []
    for i in range(CONV_W - 1):
        prev = pltpu.roll(u3[seg - (CONV_W - 1) + i], 1, axis=0)
        fill = tail[SUBLANES - (CONV_W - 1) + i:SUBLANES - (CONV_W - 1) + i + 1, :]
        wrap.append(jnp.where(sub == 0, fill, prev))
        cv_ref[0, SUBLANES - (CONV_W - 1) + i:SUBLANES - (CONV_W - 1) + i + 1, :] = (
            u3[seg - (CONV_W - 1) + i][SUBLANES - 1:SUBLANES, :])
    ext = jnp.concatenate([jnp.stack(wrap, axis=0), u3], axis=0)
    conv3 = convb_ref[...] + ext[CONV_W - 1:] * convw_ref[CONV_W - 1:CONV_W, :]
    for j in range(1, CONV_W):
        conv3 = conv3 + ext[CONV_W - 1 - j:CONV_W - 1 - j + seg] * convw_ref[CONV_W - 1 - j:CONV_W - j, :]

    a, xin = _lru_gates(conv3.reshape(tt, LRU_WIDTH), wai_ref, ba_ref, bi_ref, lam_ref)

    a3 = a.reshape(seg, SUBLANES, LRU_WIDTH)
    x3 = xin.reshape(seg, SUBLANES, LRU_WIDTH)
    hs = [x3[0]]
    ps = [a3[0]]
    for j in range(1, seg):
        hs.append(a3[j] * hs[-1] + x3[j])
        ps.append(a3[j] * ps[-1])
    carry = h_ref[0]
    cin = []
    for s in range(SUBLANES):
        cin.append(carry)
        carry = hs[-1][s:s + 1, :] + ps[-1][s:s + 1, :] * carry
    h_ref[0] = carry
    cin = jnp.concatenate(cin, axis=0)
    h3 = jnp.stack([hs[j] + ps[j] * cin for j in range(seg)], axis=0)

    hl = (jax.nn.gelu(zxy[:, LRU_WIDTH:]) * h3.reshape(tt, LRU_WIDTH)).astype(BF16)
    ol3 = _dot(hl, wout_ref[ML_WIDTH:ML_WIDTH + LRU_WIDTH, :]).reshape(seg, SUBLANES, D_MODEL)
    for j in range(seg):
        for k in range(D_MODEL // LANES):
            obuf[k, pl.ds(j, SUBLANES, stride=pitch), :] = ol3[j][:, k * LANES:(k + 1) * LANES]
    out_lru = jnp.concatenate(
        [jnp.concatenate([obuf[k, pl.ds(s * pitch, seg), :] for s in range(SUBLANES)], axis=0)
         for k in range(D_MODEL // LANES)], axis=1)

    for c in range(tt // cl):
        _mlstm_chunk(z_s, gbias_ref, onorm_ref, c_ref, n_ref, m_ref, hm_s, c * cl, cl)

    y_ref[...] = x + _dot(hm_s[...], wout_ref[0:ML_WIDTH, :]) + out_lru


def _pmixer(x, p, l, *, nb, t, tt, cl):
    nt = t // tt
    pitch = tt // SUBLANES + SUBLANES
    tok = lambda b, i: (b * nt + i, 0)
    st4 = lambda b, i: (b, 0, 0, 0)
    st3 = lambda b, i: (b, 0, 0)
    lay3 = lambda b, i: (l, 0, 0)
    lay4 = lambda b, i: (l, 0, 0, 0)
    return pl.pallas_call(
        functools.partial(_pmixer_kernel, cl=cl, tt=tt),
        out_shape=(
            jax.ShapeDtypeStruct((nb * t, D_MODEL), F32),
            jax.ShapeDtypeStruct((nb, ML_HEADS, ML_DV, ML_DK), F32),
            jax.ShapeDtypeStruct((nb, ML_HEADS, ML_DK), F32),
            jax.ShapeDtypeStruct((nb, 1, LANES), F32),
            jax.ShapeDtypeStruct((nb, 1, LRU_WIDTH), F32),
            jax.ShapeDtypeStruct((nb, SUBLANES, LRU_WIDTH), F32),
        ),
        grid=(nb, nt),
        in_specs=[
            pl.BlockSpec((tt, D_MODEL), tok),
            _resident((None, 1, D_MODEL), lay3),
            _resident((None, D_MODEL, Z_NAT), lay3),
            _resident((None, D_MODEL, Z_XY), lay3),
            _resident((None, 1, LANES), lay3),
            _resident((None, ML_HEADS, ML_DV), lay3),
            _resident((None, CONV_W, LRU_WIDTH), lay3),
            _resident((None, 1, LRU_WIDTH), lay3),
            _resident((None, LRU_BLOCKS, LRU_BW, 2 * LRU_BW), lay4),
            _resident((None, 1, LRU_WIDTH), lay3),
            _resident((None, 1, LRU_WIDTH), lay3),
            _resident((None, 1, LRU_WIDTH), lay3),
            _resident((None, ML_WIDTH + LRU_WIDTH, D_MODEL), lay3),
        ],
        out_specs=(
            pl.BlockSpec((tt, D_MODEL), tok),
            pl.BlockSpec((1, ML_HEADS, ML_DV, ML_DK), st4),
            pl.BlockSpec((1, ML_HEADS, ML_DK), st3),
            pl.BlockSpec((1, 1, LANES), st3),
            pl.BlockSpec((1, 1, LRU_WIDTH), st3),
            pl.BlockSpec((1, SUBLANES, LRU_WIDTH), st3),
        ),
        scratch_shapes=[pltpu.VMEM((tt, Z_NAT), F32),
                        pltpu.VMEM((LRU_WIDTH // LANES, SUBLANES * pitch, LANES), F32),
                        pltpu.VMEM((D_MODEL // LANES, SUBLANES * pitch, LANES), F32),
                        pltpu.VMEM((tt, ML_WIDTH), BF16)],
        compiler_params=_params("parallel", "arbitrary"),
        name="pmixer",
    )(x, p["mix_norm"], p["w_nat"], p["w_xy"], p["gbias"], p["ml_out_norm"], p["lru_conv_w"],
      p["lru_conv_b"], p["w_ai"], p["lru_b_a"], p["lru_b_i"], p["lru_lambda"], p["w_out"])


def _zproj_kernel(x_ref, g_ref, wnat_ref, wxy_ref, znat_ref, zxy_ref):
    xn = _rms(x_ref[...], g_ref[...]).astype(BF16)
    znat_ref[...] = _dot(xn, wnat_ref[...])
    zxy_ref[...] = _dot(xn, wxy_ref[...])


def _zproj(x, p, l):
    m = x.shape[0]
    lay3 = lambda i: (l, 0, 0)
    return pl.pallas_call(
        _zproj_kernel,
        out_shape=(jax.ShapeDtypeStruct((m, Z_NAT), F32), jax.ShapeDtypeStruct((m, Z_XY), F32)),
        grid=(1,),
        in_specs=[
            pl.BlockSpec((m, D_MODEL), lambda i: (0, 0)),
            pl.BlockSpec((None, 1, D_MODEL), lay3),
            pl.BlockSpec((None, D_MODEL, Z_NAT), lay3),
            pl.BlockSpec((None, D_MODEL, Z_XY), lay3),
        ],
        out_specs=(pl.BlockSpec((m, Z_NAT), lambda i: (0, 0)), pl.BlockSpec((m, Z_XY), lambda i: (0, 0))),
        compiler_params=_params("arbitrary"),
        name="zproj",
    )(x, p["mix_norm"], p["w_nat"], p["w_xy"])


def _smixer_kernel(znat_ref, zxy_ref, x_ref, c0_ref, n0_ref, m0_ref, h0_ref, cv0_ref,
                   gbias_ref, onorm_ref, convw_ref, convb_ref, wai_ref, ba_ref, bi_ref, lam_ref,
                   wout_ref,
                   y_ref, c_ref, n_ref, m_ref, h_ref, cv_ref,
                   ubuf_s, hm_s, *, cl, tt):
    c_ref[...] = c0_ref[...]
    n_ref[...] = n0_ref[...]
    m_ref[...] = m0_ref[...]

    for c in range(tt // cl):
        _mlstm_chunk(znat_ref, gbias_ref, onorm_ref, c_ref, n_ref, m_ref, hm_s, c * cl, cl)

    u = zxy_ref[:, 0:LRU_WIDTH]
    ubuf_s[0:SUBLANES, :] = cv0_ref[0]
    ubuf_s[SUBLANES:SUBLANES + tt, :] = u
    cv_ref[0] = ubuf_s[tt:tt + SUBLANES, :]
    conv = convb_ref[...] + u * convw_ref[CONV_W - 1:CONV_W, :]
    for j in range(1, CONV_W):
        conv = conv + (ubuf_s[pl.ds(SUBLANES - j, tt), :]
                       * convw_ref[CONV_W - 1 - j:CONV_W - j, :])
    a, xin = _lru_gates(conv, wai_ref, ba_ref, bi_ref, lam_ref)

    rmod = lax.broadcasted_iota(jnp.int32, (tt, LRU_WIDTH), 0) & (SUBLANES - 1)
    d = 1
    while d < SUBLANES:
        keep = rmod >= d
        xin = xin + a * jnp.where(keep, pltpu.roll(xin, d, axis=0), 0.0)
        a = a * jnp.where(keep, pltpu.roll(a, d, axis=0), 1.0)
        d *= 2
    carry = h0_ref[0]
    groups = []
    for gidx in range(tt // SUBLANES):
        lo = gidx * SUBLANES
        hb = xin[lo:lo + SUBLANES, :] + a[lo:lo + SUBLANES, :] * carry
        carry = hb[SUBLANES - 1:SUBLANES, :]
        groups.append(hb)
    h_ref[0] = carry
    h_lru = jnp.concatenate(groups, axis=0)

    hl = (jax.nn.gelu(zxy_ref[:, LRU_WIDTH:]) * h_lru).astype(BF16)
    out = (_dot(hm_s[...], wout_ref[0:ML_WIDTH, :])
           + _dot(hl, wout_ref[ML_WIDTH:ML_WIDTH + LRU_WIDTH, :]))
    y_ref[...] = x_ref[...] + out


def _smixer(znat, zxy, x, st, p, l, *, nb, t):
    tok = lambda b: (b, 0)
    lay3 = lambda b: (l, 0, 0)
    lay4 = lambda b: (l, 0, 0, 0)
    st3 = lambda b: (b, 0, 0)
    c0, n0, m0, h0, cv0 = st
    return pl.pallas_call(
        functools.partial(_smixer_kernel, cl=t, tt=t),
        out_shape=(
            jax.ShapeDtypeStruct((nb * t, D_MODEL), F32),
            jax.ShapeDtypeStruct((nb, ML_HEADS, ML_DV, ML_DK), F32),
            jax.ShapeDtypeStruct((nb, ML_HEADS, ML_DK), F32),
            jax.ShapeDtypeStruct((nb, 1, LANES), F32),
            jax.ShapeDtypeStruct((nb, 1, LRU_WIDTH), F32),
            jax.ShapeDtypeStruct((nb, SUBLANES, LRU_WIDTH), F32),
        ),
        grid=(nb,),
        in_specs=[
            pl.BlockSpec((t, Z_NAT), tok),
            pl.BlockSpec((t, Z_XY), tok),
            pl.BlockSpec((t, D_MODEL), tok),
            pl.BlockSpec((None, 1, ML_HEADS, ML_DV, ML_DK), lambda b: (l, b, 0, 0, 0)),
            pl.BlockSpec((None, 1, ML_HEADS, ML_DK), lambda b: (l, b, 0, 0)),
            pl.BlockSpec((None, 1, 1, LANES), lambda b: (l, b, 0, 0)),
            pl.BlockSpec((None, 1, 1, LRU_WIDTH), lambda b: (l, b, 0, 0)),
            pl.BlockSpec((None, 1, SUBLANES, LRU_WIDTH), lambda b: (l, b, 0, 0)),
            pl.BlockSpec((None, 1, LANES), lay3),
            pl.BlockSpec((None, ML_HEADS, ML_DV), lay3),
            pl.BlockSpec((None, CONV_W, LRU_WIDTH), lay3),
            pl.BlockSpec((None, 1, LRU_WIDTH), lay3),
            pl.BlockSpec((None, LRU_BLOCKS, LRU_BW, 2 * LRU_BW), lay4),
            pl.BlockSpec((None, 1, LRU_WIDTH), lay3),
            pl.BlockSpec((None, 1, LRU_WIDTH), lay3),
            pl.BlockSpec((None, 1, LRU_WIDTH), lay3),
            pl.BlockSpec((None, ML_WIDTH + LRU_WIDTH, D_MODEL), lay3),
        ],
        out_specs=(
            pl.BlockSpec((t, D_MODEL), tok),
            pl.BlockSpec((1, ML_HEADS, ML_DV, ML_DK), lambda b: (b, 0, 0, 0)),
            pl.BlockSpec((1, ML_HEADS, ML_DK), st3),
            pl.BlockSpec((1, 1, LANES), st3),
            pl.BlockSpec((1, 1, LRU_WIDTH), st3),
            pl.BlockSpec((1, SUBLANES, LRU_WIDTH), st3),
        ),
        scratch_shapes=[pltpu.VMEM((t + SUBLANES, LRU_WIDTH), F32),
                        pltpu.VMEM((t, ML_WIDTH), BF16)],
        compiler_params=_params("parallel"),
        name="smixer",
    )(znat, zxy, x, c0, n0, m0, h0, cv0, p["gbias"], p["ml_out_norm"], p["lru_conv_w"],
      p["lru_conv_b"], p["w_ai"], p["lru_b_a"], p["lru_b_i"], p["lru_lambda"], p["w_out"])


def _xattn_kernel(x_ref, g_ref, wq_ref, mk_ref, mv_ref, wo_ref, y_ref, o_s):
    x = x_ref[...]
    xn = _rms(x, g_ref[...]).astype(BF16)
    q = _dot(xn, wq_ref[...])
    for h in range(XA_HEADS):
        cols = slice(h * XA_DH, (h + 1) * XA_DH)
        s = _nt_dot(q[:, cols].astype(BF16), mk_ref[:, cols].astype(BF16)) * (XA_DH ** -0.5)
        p = jnp.exp(s - jnp.max(s, axis=1, keepdims=True))
        p = p / jnp.sum(p, axis=1, keepdims=True)
        o_s[:, cols] = _dot(p.astype(BF16), mv_ref[:, cols].astype(BF16)).astype(BF16)
    y_ref[...] = x + _dot(o_s[...], wo_ref[...])


def _xattn(x, g, wq, mk, mv, wo, l, *, nb, t, tt):
    nt = t // tt
    tok = lambda b, i: (b * nt + i, 0)
    lay3 = lambda b, i: (l, 0, 0)
    mem = lambda b, i: (l, b, 0, 0)
    return pl.pallas_call(
        _xattn_kernel,
        out_shape=jax.ShapeDtypeStruct((nb * t, D_MODEL), F32),
        grid=(nb, nt),
        in_specs=[
            pl.BlockSpec((tt, D_MODEL), tok),
            pl.BlockSpec((None, 1, D_MODEL), lay3),
            pl.BlockSpec((None, D_MODEL, D_MODEL), lay3),
            pl.BlockSpec((None, None, MEM_LEN, D_MODEL), mem),
            pl.BlockSpec((None, None, MEM_LEN, D_MODEL), mem),
            pl.BlockSpec((None, D_MODEL, D_MODEL), lay3),
        ],
        out_specs=pl.BlockSpec((tt, D_MODEL), tok),
        scratch_shapes=[pltpu.VMEM((tt, D_MODEL), BF16)],
        compiler_params=_params("parallel", "arbitrary"),
        name="xattn",
    )(x, g, wq, mk, mv, wo)


def _memkv_kernel(mem_ref, g_ref, wk_ref, wv_ref, k_ref, v_ref):
    mn = _rms(mem_ref[...], g_ref[...]).astype(BF16)
    k_ref[...] = _dot(mn, wk_ref[...])
    v_ref[...] = _dot(mn, wv_ref[...])


def _memkv(mem, g, wk, wv):
    nb = mem.shape[0]
    wspec = pl.BlockSpec((None, D_MODEL, D_MODEL), lambda l, b: (l, 0, 0))
    ospec = pl.BlockSpec((None, None, MEM_LEN, D_MODEL), lambda l, b: (l, b, 0, 0))
    out = jax.ShapeDtypeStruct((DEPTH, nb, MEM_LEN, D_MODEL), F32)
    return pl.pallas_call(
        _memkv_kernel,
        out_shape=(out, out),
        grid=(DEPTH, nb),
        in_specs=[pl.BlockSpec((None, MEM_LEN, D_MODEL), lambda l, b: (b, 0, 0)),
                  pl.BlockSpec((None, 1, D_MODEL), lambda l, b: (l, 0, 0)), wspec, wspec],
        out_specs=(ospec, ospec),
        compiler_params=_params("parallel", "parallel"),
        name="memkv",
    )(mem, g, wk, wv)


def kernel(x_prompt, x_sample, mem_prompt, state_mlstm_C, state_mlstm_n, state_mlstm_m, state_lru_h,
           state_lru_conv, cache_mem_k, cache_mem_v, ffn1_norm, ffn1_w_gate, ffn1_w_up, ffn1_w_down,
           mix_norm, w_in, ml_b_i, ml_b_f, ml_out_norm, lru_conv_w, lru_conv_b, lru_w_a, lru_b_a,
           lru_w_i, lru_b_i, lru_lambda, w_out, xattn_norm, mem_norm, xattn_w_q, xattn_w_k, xattn_w_v,
           xattn_w_o, ffn2_norm, ffn2_w_gate, ffn2_w_up, ffn2_w_down, final_norm):
    bp, tp, _ = x_prompt.shape
    bs, ts, _ = x_sample.shape
    xp = x_prompt.reshape(bp * tp, D_MODEL)
    xs = x_sample.reshape(bs * ts, D_MODEL)

    bf = lambda w: w.astype(BF16)
    row = lambda v: v.astype(F32).reshape(DEPTH, 1, -1)
    gate_pad = ((0, 0), (0, 0), (0, LANES - 2 * ML_HEADS))
    p = {
        "mix_norm": row(mix_norm),
        "w_nat": bf(jnp.concatenate([w_in[:, :, 0:W_I], jnp.pad(w_in[:, :, W_I:W_X], gate_pad)], axis=2)),
        "w_xy": bf(w_in[:, :, W_X:]),
        "gbias": jnp.pad(jnp.concatenate([ml_b_i, ml_b_f], axis=1).astype(F32)[:, None, :], gate_pad),
        "ml_out_norm": ml_out_norm.astype(F32),
        "lru_conv_w": lru_conv_w, "lru_conv_b": row(lru_conv_b),
        "w_ai": bf(jnp.concatenate([lru_w_a, lru_w_i], axis=-1)),
        "lru_b_a": row(lru_b_a), "lru_b_i": row(lru_b_i), "lru_lambda": row(lru_lambda),
        "w_out": bf(w_out),
    }
    ffn1 = (row(ffn1_norm), bf(ffn1_w_gate), bf(ffn1_w_up), bf(ffn1_w_down))
    ffn2 = (row(ffn2_norm), bf(ffn2_w_gate), bf(ffn2_w_up), bf(ffn2_w_down))
    xa_g, xa_q, xa_o = row(xattn_norm), bf(xattn_w_q), bf(xattn_w_o)
    fin = final_norm.astype(F32).reshape(1, D_MODEL)

    pk, pv = _memkv(mem_prompt, row(mem_norm), bf(xattn_w_k), bf(xattn_w_v))
    sk = cache_mem_k.reshape(DEPTH, bs, MEM_LEN, D_MODEL)
    sv = cache_mem_v.reshape(DEPTH, bs, MEM_LEN, D_MODEL)
    s_state = (
        state_mlstm_C.astype(F32), state_mlstm_n.astype(F32),
        jnp.pad(state_mlstm_m.astype(F32)[:, :, None, :], ((0, 0), (0, 0), (0, 0), (0, LANES - ML_HEADS))),
        state_lru_h.astype(F32)[:, :, None, :],
        jnp.pad(state_lru_conv.astype(F32), ((0, 0), (0, 0), (SUBLANES - (CONV_W - 1), 0), (0, 0))),
    )

    p_out = [[] for _ in range(5)]
    s_out = [[] for _ in range(5)]
    for l in range(DEPTH):
        last = l == DEPTH - 1
        xp = _ffn(xp, *ffn1, fin, l, tm=1024, tf=512, final_norm=False)
        xp, *st = _pmixer(xp, p, l, nb=bp, t=tp, tt=256, cl=CHUNK)
        xp = _xattn(xp, xa_g, xa_q, pk, pv, xa_o, l, nb=bp, t=tp, tt=512)
        xp = _ffn(xp, *ffn2, fin, l, tm=1024, tf=512, final_norm=last)
        for acc, v in zip(p_out, st):
            acc.append(v)
        xs = _ffn(xs, *ffn1, fin, l, tm=bs * ts, tf=1024, final_norm=False)
        znat, zxy = _zproj(xs, p, l)
        xs, *st = _smixer(znat, zxy, xs, s_state, p, l, nb=bs, t=ts)
        xs = _xattn(xs, xa_g, xa_q, sk, sv, xa_o, l, nb=bs, t=ts, tt=ts)
        xs = _ffn(xs, *ffn2, fin, l, tm=bs * ts, tf=1024, final_norm=last)
        for acc, v in zip(s_out, st):
            acc.append(v)

    def states(acc):
        c, n, m, h, cv = (jnp.stack(a) for a in acc)
        return c, n, m[:, :, 0, :ML_HEADS], h[:, :, 0, :], cv[:, :, SUBLANES - (CONV_W - 1):, :]

    return (xp.reshape(bp, tp, D_MODEL), xs.reshape(bs, ts, D_MODEL),
            *states(p_out),
            pk.reshape(DEPTH, bp, MEM_LEN, XA_HEADS, XA_DH), pv.reshape(DEPTH, bp, MEM_LEN, XA_HEADS, XA_DH),
            *states(s_out))
```

```python
import functools

import jax
import jax.numpy as jnp
from jax import lax
from jax.experimental import pallas as pl
from jax.experimental.pallas import tpu as pltpu

F32 = jnp.float32
BF16 = jnp.bfloat16

D_MODEL = 1024
DEPTH = 2
CHUNK = 64
ML_HEADS = 4
ML_DV = 256
ML_DK = 128
ML_WIDTH = ML_HEADS * ML_DV
LRU_WIDTH = 1024
LRU_BLOCKS = 8
LRU_BW = LRU_WIDTH // LRU_BLOCKS
CONV_W = 4
LRU_C = 8.0
MEM_LEN = 256
XA_HEADS = 4
XA_DH = D_MODEL // XA_HEADS
D_FF = 4 * D_MODEL
EPS = 1e-6

LANES = 128
SUBLANES = 8
VMEM_LIMIT = 56 * 1024 * 1024

Z_Q = 0
Z_K = Z_Q + ML_HEADS * ML_DK
Z_V = Z_K + ML_HEADS * ML_DK
Z_O = Z_V + ML_WIDTH
Z_G = Z_O + ML_WIDTH
Z_NAT = Z_G + LANES
Z_XY = 2 * LRU_WIDTH
W_I = Z_O + ML_WIDTH
W_X = W_I + 2 * ML_HEADS


def _rms(x, g):
    return x * lax.rsqrt(jnp.mean(x * x, axis=-1, keepdims=True) + EPS) * g


def _nt_dot(a, b):
    return lax.dot_general(a, b, (((1,), (1,)), ((), ())), preferred_element_type=F32)


def _tn_dot(a, b):
    return lax.dot_general(a, b, (((0,), (0,)), ((), ())), preferred_element_type=F32)


def _dot(a, b):
    return jnp.dot(a, b, preferred_element_type=F32)


def _params(*sem):
    return pltpu.CompilerParams(dimension_semantics=sem, vmem_limit_bytes=VMEM_LIMIT)


def _resident(shape, index_map):
    return pl.BlockSpec(shape, index_map, pipeline_mode=pl.Buffered(1))


def _ffn_kernel(x_ref, g_ref, wg_ref, wu_ref, wd_ref, fg_ref, o_ref, xn_s, acc_s, *, final_norm):
    j = pl.program_id(1)

    @pl.when(j == 0)
    def _():
        xn_s[...] = _rms(x_ref[...], g_ref[...]).astype(BF16)
        acc_s[...] = jnp.zeros_like(acc_s)

    xn = xn_s[...]
    g = _dot(xn, wg_ref[...])
    u = _dot(xn, wu_ref[...])
    h = (g * jax.nn.sigmoid(g) * u).astype(BF16)
    acc_s[...] += _dot(h, wd_ref[...])

    @pl.when(j == pl.num_programs(1) - 1)
    def _():
        y = x_ref[...] + 0.5 * acc_s[...]
        if final_norm:
            y = _rms(y, fg_ref[...])
        o_ref[...] = y


def _ffn(x, g, wg, wu, wd, fg, l, *, tm, tf, final_norm):
    m = x.shape[0]
    return pl.pallas_call(
        functools.partial(_ffn_kernel, final_norm=final_norm),
        out_shape=jax.ShapeDtypeStruct((m, D_MODEL), F32),
        grid=(m // tm, D_FF // tf),
        in_specs=[
            pl.BlockSpec((tm, D_MODEL), lambda i, j: (i, 0)),
            pl.BlockSpec((None, 1, D_MODEL), lambda i, j: (l, 0, 0)),
            pl.BlockSpec((None, D_MODEL, tf), lambda i, j: (l, 0, j)),
            pl.BlockSpec((None, D_MODEL, tf), lambda i, j: (l, 0, j)),
            pl.BlockSpec((None, tf, D_MODEL), lambda i, j: (l, j, 0)),
            pl.BlockSpec((1, D_MODEL), lambda i, j: (0, 0)),
        ],
        out_specs=pl.BlockSpec((tm, D_MODEL), lambda i, j: (i, 0)),
        scratch_shapes=[pltpu.VMEM((tm, D_MODEL), BF16), pltpu.VMEM((tm, D_MODEL), F32)],
        compiler_params=_params("parallel", "arbitrary"),
        name="ffn",
    )(x, g, wg, wu, wd, fg)


def _ffn_res_kernel(x_ref, g_ref, wg_ref, wu_ref, wd_ref, fg_ref, o_ref, h_s, *, tf, final_norm):
    x = x_ref[...]
    xn = _rms(x, g_ref[...]).astype(BF16)
    for j in range(D_FF // tf):
        cols = slice(j * tf, (j + 1) * tf)
        g = _dot(xn, wg_ref[:, cols])
        u = _dot(xn, wu_ref[:, cols])
        h_s[:, cols] = (g * jax.nn.sigmoid(g) * u).astype(BF16)
    y = x + 0.5 * _dot(h_s[...], wd_ref[...])
    if final_norm:
        y = _rms(y, fg_ref[...])
    o_ref[...] = y


def _ffn_res(x, g, wg, wu, wd, fg, l, *, tm, tf, final_norm):
    m = x.shape[0]
    return pl.pallas_call(
        functools.partial(_ffn_res_kernel, tf=tf, final_norm=final_norm),
        out_shape=jax.ShapeDtypeStruct((m, D_MODEL), F32),
        grid=(m // tm,),
        in_specs=[
            pl.BlockSpec((tm, D_MODEL), lambda i: (i, 0)),
            _resident((None, 1, D_MODEL), lambda i: (l, 0, 0)),
            _resident((None, D_MODEL, D_FF), lambda i: (l, 0, 0)),
            _resident((None, D_MODEL, D_FF), lambda i: (l, 0, 0)),
            _resident((None, D_FF, D_MODEL), lambda i: (l, 0, 0)),
            _resident((1, D_MODEL), lambda i: (0, 0)),
        ],
        out_specs=pl.BlockSpec((tm, D_MODEL), lambda i: (i, 0)),
        scratch_shapes=[pltpu.VMEM((tm, D_FF), BF16)],
        compiler_params=_params("parallel"),
        name="ffn_res",
    )(x, g, wg, wu, wd, fg)


def _log_sigmoid(x):
    return jnp.minimum(x, 0.0) - jnp.log1p(jnp.exp(-jnp.abs(x)))


def _softplus(x):
    return jnp.maximum(x, 0.0) + jnp.log1p(jnp.exp(-jnp.abs(x)))


def _lane_pick(x, lane_ids, idx):
    return jnp.sum(jnp.where(lane_ids == idx, x, 0.0), axis=1, keepdims=True)


def _mlstm_chunk(z_ref, gbias_ref, onorm_ref, c_ref, n_ref, m_ref, hm_s, r0, cl):
    lane = lax.broadcasted_iota(jnp.int32, (cl, LANES), 1)
    row = lax.broadcasted_iota(jnp.int32, (cl, LANES), 0)
    tril = (lax.broadcasted_iota(jnp.int32, (cl, cl), 1)
            <= lax.broadcasted_iota(jnp.int32, (cl, cl), 0))
    lane1 = lax.broadcasted_iota(jnp.int32, (1, LANES), 1)

    gates = z_ref[r0:r0 + cl, Z_G:Z_G + LANES] + gbias_ref[...]
    bsum = _log_sigmoid(gates)
    d = 1
    while d < cl:
        bsum = bsum + jnp.where(row >= d, pltpu.roll(bsum, d, axis=0), 0.0)
        d *= 2
    mixed = jnp.where(lane < ML_HEADS, gates, bsum)
    mixed = jnp.concatenate([mixed, jnp.zeros((LANES - cl, LANES), F32)], axis=0)
    mixed_t = mixed.T

    m_vec = m_ref[0]
    m_out = m_vec
    for h in range(ML_HEADS):
        b_col = _lane_pick(bsum, lane, ML_HEADS + h)
        ig_col = _lane_pick(gates, lane, h)
        b_row = mixed_t[ML_HEADS + h:ML_HEADS + h + 1, 0:cl]
        ig_row = mixed_t[h:h + 1, 0:cl]
        m_prev = _lane_pick(m_vec, lane1, h)

        dmat = jnp.where(tril, b_col - b_row + ig_row, -jnp.inf)
        inter = b_col + m_prev
        m_t = jnp.maximum(inter, jnp.max(dmat, axis=1, keepdims=True))

        qf = z_ref[r0:r0 + cl, Z_Q + h * ML_DK:Z_Q + (h + 1) * ML_DK] * (ML_DK ** -0.5)
        kf = z_ref[r0:r0 + cl, Z_K + h * ML_DK:Z_K + (h + 1) * ML_DK]
        vf = z_ref[r0:r0 + cl, Z_V + h * ML_DV:Z_V + (h + 1) * ML_DV]
        q = qf.astype(BF16)
        k = kf.astype(BF16)

        s = _nt_dot(q, k) * jnp.exp(dmat - m_t)
        w_inter = jnp.exp(inter - m_t)
        c_old = c_ref[0, h]
        n_old = n_ref[0, h:h + 1, :]
        num = w_inter * _nt_dot(q, c_old.astype(BF16)) + _dot(s.astype(BF16), vf.astype(BF16))
        den = (w_inter * jnp.sum(qf * n_old, axis=1, keepdims=True)
               + jnp.sum(s, axis=1, keepdims=True))
        hh = num / jnp.maximum(jnp.abs(den), jnp.exp(-m_t))

        hh = hh * lax.rsqrt(jnp.mean(hh * hh, axis=1, keepdims=True) + EPS) * onorm_ref[h:h + 1, :]
        o_gate = jax.nn.sigmoid(z_ref[r0:r0 + cl, Z_O + h * ML_DV:Z_O + (h + 1) * ML_DV])
        hm_s[r0:r0 + cl, h * ML_DV:(h + 1) * ML_DV] = (o_gate * hh).astype(BF16)

        m_new = m_t[cl - 1:cl, :]
        b_last = b_col[cl - 1:cl, :]
        wgt = jnp.exp(b_last - b_col + ig_col - m_new)
        decay = jnp.exp(b_last + m_prev - m_new)
        c_ref[0, h] = decay * c_old + _tn_dot((wgt * vf).astype(BF16), k)
        n_ref[0, h:h + 1, :] = decay * n_old + jnp.sum(wgt * kf, axis=0, keepdims=True)
        m_out = jnp.where(lane1 == h, m_new, m_out)
    m_ref[0] = m_out


def _lru_gates(conv, wai_ref, ba_ref, bi_ref, lam_ref):
    conv_b = conv.astype(BF16)
    pre = [_dot(conv_b[:, n * LRU_BW:(n + 1) * LRU_BW], wai_ref[n])
           for n in range(LRU_BLOCKS)]
    r_gate = jax.nn.sigmoid(jnp.concatenate([p[:, :LRU_BW] for p in pre], axis=1) + ba_ref[...])
    i_gate = jax.nn.sigmoid(jnp.concatenate([p[:, LRU_BW:] for p in pre], axis=1) + bi_ref[...])
    log_a = -LRU_C * r_gate * _softplus(-lam_ref[...])
    a = jnp.exp(log_a)
    xin = jnp.sqrt(-jnp.tanh(log_a) * (a * a + 1.0)) * (i_gate * conv)
    return a, xin


def _pmixer_kernel(x_ref, gmix_ref, wnat_ref, wxy_ref, gbias_ref, onorm_ref, convw_ref, convb_ref,
                   wai_ref, ba_ref, bi_ref, lam_ref, wout_ref,
                   y_ref, c_ref, n_ref, m_ref, h_ref, cv_ref,
                   z_s, pbuf, obuf, hm_s, *, cl, tt):
    seg = tt // SUBLANES
    pitch = seg + SUBLANES

    @pl.when(pl.program_id(1) == 0)
    def _():
        c_ref[...] = jnp.zeros_like(c_ref)
        n_ref[...] = jnp.zeros_like(n_ref)
        m_ref[...] = jnp.zeros_like(m_ref)
        h_ref[...] = jnp.zeros_like(h_ref)
        cv_ref[...] = jnp.zeros_like(cv_ref)

    x = x_ref[...]
    xn = _rms(x, gmix_ref[...])
    z_s[...] = _dot(xn.astype(BF16), wnat_ref[...])

    for k in range(LRU_WIDTH // LANES):
        for s in range(SUBLANES):
            pbuf[k, pl.ds(s * pitch, seg), :] = xn[s * seg:(s + 1) * seg, k * LANES:(k + 1) * LANES]
    xnp = jnp.stack(
        [jnp.concatenate([pbuf[k, pl.ds(j, SUBLANES, stride=pitch), :]
                          for k in range(LRU_WIDTH // LANES)], axis=1) for j in range(seg)], axis=0)
    zxy = _dot(xnp.reshape(tt, D_MODEL).astype(BF16), wxy_ref[...])

    u3 = zxy[:, 0:LRU_WIDTH].reshape(seg, SUBLANES, LRU_WIDTH)
    tail = cv_ref[0]
    sub = lax.broadcasted_iota(jnp.int32, (SUBLANES, LRU_WIDTH), 0)
    wrap = []
    for i in range(CONV_W - 1):
        prev = pltpu.roll(u3[seg - (CONV_W - 1) + i], 1, axis=0)
        fill = tail[SUBLANES - (CONV_W - 1) + i:SUBLANES - (CONV_W - 1) + i + 1, :]
        wrap.append(jnp.where(sub == 0, fill, prev))
        cv_ref[0, SUBLANES - (CONV_W - 1) + i:SUBLANES - (CONV_W - 1) + i + 1, :] = (
            u3[seg - (CONV_W - 1) + i][SUBLANES - 1:SUBLANES, :])
    ext = jnp.concatenate([jnp.stack(wrap, axis=0), u3], axis=0)
    conv3 = convb_ref[...] + ext[CONV_W - 1:] * convw_ref[CONV_W - 1:CONV_W, :]
    for j in range(1, CONV_W):
        conv3 = conv3 + ext[CONV_W - 1 - j:CONV_W - 1 - j + seg] * convw_ref[CONV_W - 1 - j:CONV_W - j, :]

    a, xin = _lru_gates(conv3.reshape(tt, LRU_WIDTH), wai_ref, ba_ref, bi_ref, lam_ref)

    a3 = a.reshape(seg, SUBLANES, LRU_WIDTH)
    x3 = xin.reshape(seg, SUBLANES, LRU_WIDTH)
    hs = [x3[0]]
    ps = [a3[0]]
    for j in range(1, seg):
        hs.append(a3[j] * hs[-1] + x3[j])
        ps.append(a3[j] * ps[-1])
    carry = h_ref[0]
    cin = []
    for s in range(SUBLANES):
        cin.append(carry)
        carry = hs[-1][s:s + 1, :] + ps[-1][s:s + 1, :] * carry
    h_ref[0] = carry
    cin = jnp.concatenate(cin, axis=0)
    h3 = jnp.stack([hs[j] + ps[j] * cin for j in range(seg)], axis=0)

    hl = (jax.nn.gelu(zxy[:, LRU_WIDTH:]) * h3.reshape(tt, LRU_WIDTH)).astype(BF16)
    ol3 = _dot(hl, wout_ref[ML_WIDTH:ML_WIDTH + LRU_WIDTH, :]).reshape(seg, SUBLANES, D_MODEL)
    for j in range(seg):
        for k in range(D_MODEL // LANES):
            obuf[k, pl.ds(j, SUBLANES, stride=pitch), :] = ol3[j][:, k * LANES:(k + 1) * LANES]
    out_lru = jnp.concatenate(
        [jnp.concatenate([obuf[k, pl.ds(s * pitch, seg), :] for s in range(SUBLANES)], axis=0)
         for k in range(D_MODEL // LANES)], axis=1)

    for c in range(tt // cl):
        _mlstm_chunk(z_s, gbias_ref, onorm_ref, c_ref, n_ref, m_ref, hm_s, c * cl, cl)

    y_ref[...] = x + _dot(hm_s[...], wout_ref[0:ML_WIDTH, :]) + out_lru


def _pmixer(x, p, l, *, nb, t, tt, cl):
    nt = t // tt
    pitch = tt // SUBLANES + SUBLANES
    tok = lambda b, i: (b * nt + i, 0)
    st4 = lambda b, i: (b, 0, 0, 0)
    st3 = lambda b, i: (b, 0, 0)
    lay3 = lambda b, i: (l, 0, 0)
    lay4 = lambda b, i: (l, 0, 0, 0)
    return pl.pallas_call(
        functools.partial(_pmixer_kernel, cl=cl, tt=tt),
        out_shape=(
            jax.ShapeDtypeStruct((nb * t, D_MODEL), F32),
            jax.ShapeDtypeStruct((nb, ML_HEADS, ML_DV, ML_DK), F32),
            jax.ShapeDtypeStruct((nb, ML_HEADS, ML_DK), F32),
            jax.ShapeDtypeStruct((nb, 1, LANES), F32),
            jax.ShapeDtypeStruct((nb, 1, LRU_WIDTH), F32),
            jax.ShapeDtypeStruct((nb, SUBLANES, LRU_WIDTH), F32),
        ),
        grid=(nb, nt),
        in_specs=[
            pl.BlockSpec((tt, D_MODEL), tok),
            _resident((None, 1, D_MODEL), lay3),
            _resident((None, D_MODEL, Z_NAT), lay3),
            _resident((None, D_MODEL, Z_XY), lay3),
            _resident((None, 1, LANES), lay3),
            _resident((None, ML_HEADS, ML_DV), lay3),
            _resident((None, CONV_W, LRU_WIDTH), lay3),
            _resident((None, 1, LRU_WIDTH), lay3),
            _resident((None, LRU_BLOCKS, LRU_BW, 2 * LRU_BW), lay4),
            _resident((None, 1, LRU_WIDTH), lay3),
            _resident((None, 1, LRU_WIDTH), lay3),
            _resident((None, 1, LRU_WIDTH), lay3),
            _resident((None, ML_WIDTH + LRU_WIDTH, D_MODEL), lay3),
        ],
        out_specs=(
            pl.BlockSpec((tt, D_MODEL), tok),
            pl.BlockSpec((1, ML_HEADS, ML_DV, ML_DK), st4),
            pl.BlockSpec((1, ML_HEADS, ML_DK), st3),
            pl.BlockSpec((1, 1, LANES), st3),
            pl.BlockSpec((1, 1, LRU_WIDTH), st3),
            pl.BlockSpec((1, SUBLANES, LRU_WIDTH), st3),
        ),
        scratch_shapes=[pltpu.VMEM((tt, Z_NAT), F32),
                        pltpu.VMEM((LRU_WIDTH // LANES, SUBLANES * pitch, LANES), F32),
                        pltpu.VMEM((D_MODEL // LANES, SUBLANES * pitch, LANES), F32),
                        pltpu.VMEM((tt, ML_WIDTH), BF16)],
        compiler_params=_params("parallel", "arbitrary"),
        name="pmixer",
    )(x, p["mix_norm"], p["w_nat"], p["w_xy"], p["gbias"], p["ml_out_norm"], p["lru_conv_w"],
      p["lru_conv_b"], p["w_ai"], p["lru_b_a"], p["lru_b_i"], p["lru_lambda"], p["w_out"])


def _zproj_kernel(x_ref, g_ref, wnat_ref, wxy_ref, znat_ref, zxy_ref):
    xn = _rms(x_ref[...], g_ref[...]).astype(BF16)
    znat_ref[...] = _dot(xn, wnat_ref[...])
    zxy_ref[...] = _dot(xn, wxy_ref[...])


def _zproj(x, p, l):
    m = x.shape[0]
    lay3 = lambda i: (l, 0, 0)
    return pl.pallas_call(
        _zproj_kernel,
        out_shape=(jax.ShapeDtypeStruct((m, Z_NAT), F32), jax.ShapeDtypeStruct((m, Z_XY), F32)),
        grid=(1,),
        in_specs=[
            pl.BlockSpec((m, D_MODEL), lambda i: (0, 0)),
            pl.BlockSpec((None, 1, D_MODEL), lay3),
            pl.BlockSpec((None, D_MODEL, Z_NAT), lay3),
            pl.BlockSpec((None, D_MODEL, Z_XY), lay3),
        ],
        out_specs=(pl.BlockSpec((m, Z_NAT), lambda i: (0, 0)), pl.BlockSpec((m, Z_XY), lambda i: (0, 0))),
        compiler_params=_params("arbitrary"),
        name="zproj",
    )(x, p["mix_norm"], p["w_nat"], p["w_xy"])


def _smixer_kernel(znat_ref, zxy_ref, x_ref, c0_ref, n0_ref, m0_ref, h0_ref, cv0_ref,
                   gbias_ref, onorm_ref, convw_ref, convb_ref, wai_ref, ba_ref, bi_ref, lam_ref,
                   wout_ref,
                   y_ref, c_ref, n_ref, m_ref, h_ref, cv_ref,
                   ubuf_s, hm_s, *, cl, tt):
    c_ref[...] = c0_ref[...]
    n_ref[...] = n0_ref[...]
    m_ref[...] = m0_ref[...]

    for c in range(tt // cl):
        _mlstm_chunk(znat_ref, gbias_ref, onorm_ref, c_ref, n_ref, m_ref, hm_s, c * cl, cl)

    u = zxy_ref[:, 0:LRU_WIDTH]
    ubuf_s[0:SUBLANES, :] = cv0_ref[0]
    ubuf_s[SUBLANES:SUBLANES + tt, :] = u
    cv_ref[0] = ubuf_s[tt:tt + SUBLANES, :]
    conv = convb_ref[...] + u * convw_ref[CONV_W - 1:CONV_W, :]
    for j in range(1, CONV_W):
        conv = conv + (ubuf_s[pl.ds(SUBLANES - j, tt), :]
                       * convw_ref[CONV_W - 1 - j:CONV_W - j, :])
    a, xin = _lru_gates(conv, wai_ref, ba_ref, bi_ref, lam_ref)

    rmod = lax.broadcasted_iota(jnp.int32, (tt, LRU_WIDTH), 0) & (SUBLANES - 1)
    d = 1
    while d < SUBLANES:
        keep = rmod >= d
        xin = xin + a * jnp.where(keep, pltpu.roll(xin, d, axis=0), 0.0)
        a = a * jnp.where(keep, pltpu.roll(a, d, axis=0), 1.0)
        d *= 2
    carry = h0_ref[0]
    groups = []
    for gidx in range(tt // SUBLANES):
        lo = gidx * SUBLANES
        hb = xin[lo:lo + SUBLANES, :] + a[lo:lo + SUBLANES, :] * carry
        carry = hb[SUBLANES - 1:SUBLANES, :]
        groups.append(hb)
    h_ref[0] = carry
    h_lru = jnp.concatenate(groups, axis=0)

    hl = (jax.nn.gelu(zxy_ref[:, LRU_WIDTH:]) * h_lru).astype(BF16)
    out = (_dot(hm_s[...], wout_ref[0:ML_WIDTH, :])
           + _dot(hl, wout_ref[ML_WIDTH:ML_WIDTH + LRU_WIDTH, :]))
    y_ref[...] = x_ref[...] + out


def _smixer(znat, zxy, x, st, p, l, *, nb, t):
    tok = lambda b: (b, 0)
    lay3 = lambda b: (l, 0, 0)
    lay4 = lambda b: (l, 0, 0, 0)
    st3 = lambda b: (b, 0, 0)
    c0, n0, m0, h0, cv0 = st
    return pl.pallas_call(
        functools.partial(_smixer_kernel, cl=t, tt=t),
        out_shape=(
            jax.ShapeDtypeStruct((nb * t, D_MODEL), F32),
            jax.ShapeDtypeStruct((nb, ML_HEADS, ML_DV, ML_DK), F32),
            jax.ShapeDtypeStruct((nb, ML_HEADS, ML_DK), F32),
            jax.ShapeDtypeStruct((nb, 1, LANES), F32),
            jax.ShapeDtypeStruct((nb, 1, LRU_WIDTH), F32),
            jax.ShapeDtypeStruct((nb, SUBLANES, LRU_WIDTH), F32),
        ),
        grid=(nb,),
        in_specs=[
            pl.BlockSpec((t, Z_NAT), tok),
            pl.BlockSpec((t, Z_XY), tok),
            pl.BlockSpec((t, D_MODEL), tok),
            pl.BlockSpec((None, 1, ML_HEADS, ML_DV, ML_DK), lambda b: (l, b, 0, 0, 0)),
            pl.BlockSpec((None, 1, ML_HEADS, ML_DK), lambda b: (l, b, 0, 0)),
            pl.BlockSpec((None, 1, 1, LANES), lambda b: (l, b, 0, 0)),
            pl.BlockSpec((None, 1, 1, LRU_WIDTH), lambda b: (l, b, 0, 0)),
            pl.BlockSpec((None, 1, SUBLANES, LRU_WIDTH), lambda b: (l, b, 0, 0)),
            pl.BlockSpec((None, 1, LANES), lay3),
            pl.BlockSpec((None, ML_HEADS, ML_DV), lay3),
            pl.BlockSpec((None, CONV_W, LRU_WIDTH), lay3),
            pl.BlockSpec((None, 1, LRU_WIDTH), lay3),
            pl.BlockSpec((None, LRU_BLOCKS, LRU_BW, 2 * LRU_BW), lay4),
            pl.BlockSpec((None, 1, LRU_WIDTH), lay3),
            pl.BlockSpec((None, 1, LRU_WIDTH), lay3),
            pl.BlockSpec((None, 1, LRU_WIDTH), lay3),
            pl.BlockSpec((None, ML_WIDTH + LRU_WIDTH, D_MODEL), lay3),
        ],
        out_specs=(
            pl.BlockSpec((t, D_MODEL), tok),
            pl.BlockSpec((1, ML_HEADS, ML_DV, ML_DK), lambda b: (b, 0, 0, 0)),
            pl.BlockSpec((1, ML_HEADS, ML_DK), st3),
            pl.BlockSpec((1, 1, LANES), st3),
            pl.BlockSpec((1, 1, LRU_WIDTH), st3),
            pl.BlockSpec((1, SUBLANES, LRU_WIDTH), st3),
        ),
        scratch_shapes=[pltpu.VMEM((t + SUBLANES, LRU_WIDTH), F32),
                        pltpu.VMEM((t, ML_WIDTH), BF16)],
        compiler_params=_params("parallel"),
        name="smixer",
    )(znat, zxy, x, c0, n0, m0, h0, cv0, p["gbias"], p["ml_out_norm"], p["lru_conv_w"],
      p["lru_conv_b"], p["w_ai"], p["lru_b_a"], p["lru_b_i"], p["lru_lambda"], p["w_out"])


def _xattn_kernel(x_ref, g_ref, wq_ref, mk_ref, mv_ref, wo_ref, y_ref, o_s, *, ns, rows):
    x = x_ref[...]
    xn = _rms(x, g_ref[...]).astype(BF16)
    q = _dot(xn, wq_ref[...])
    for s in range(ns):
        r = slice(s * rows, (s + 1) * rows)
        for h in range(XA_HEADS):
            cols = slice(h * XA_DH, (h + 1) * XA_DH)
            sc = _nt_dot(q[r, cols].astype(BF16), mk_ref[s, :, cols].astype(BF16)) * (XA_DH ** -0.5)
            p = jnp.exp(sc - jnp.max(sc, axis=1, keepdims=True))
            p = p / jnp.sum(p, axis=1, keepdims=True)
            o_s[r, cols] = _dot(p.astype(BF16), mv_ref[s, :, cols].astype(BF16)).astype(BF16)
    y_ref[...] = x + _dot(o_s[...], wo_ref[...])


def _xattn(x, g, wq, mk, mv, wo, l, *, nb, t, tt, ns=1):
    nt = t // tt
    assert ns == 1 or nt == 1
    tok = lambda b, i: (b * nt + i, 0)
    lay3 = lambda b, i: (l, 0, 0)
    mem = lambda b, i: (l, b, 0, 0)
    return pl.pallas_call(
        functools.partial(_xattn_kernel, ns=ns, rows=tt),
        out_shape=jax.ShapeDtypeStruct((nb * t, D_MODEL), F32),
        grid=(nb // ns, nt),
        in_specs=[
            pl.BlockSpec((ns * tt, D_MODEL), tok),
            pl.BlockSpec((None, 1, D_MODEL), lay3),
            pl.BlockSpec((None, D_MODEL, D_MODEL), lay3),
            pl.BlockSpec((None, ns, MEM_LEN, D_MODEL), mem),
            pl.BlockSpec((None, ns, MEM_LEN, D_MODEL), mem),
            pl.BlockSpec((None, D_MODEL, D_MODEL), lay3),
        ],
        out_specs=pl.BlockSpec((ns * tt, D_MODEL), tok),
        scratch_shapes=[pltpu.VMEM((ns * tt, D_MODEL), BF16)],
        compiler_params=_params("parallel", "arbitrary"),
        name="xattn",
    )(x, g, wq, mk, mv, wo)


def _memkv_kernel(mem_ref, g_ref, wk_ref, wv_ref, k_ref, v_ref):
    mn = _rms(mem_ref[...], g_ref[...]).astype(BF16)
    k_ref[...] = _dot(mn, wk_ref[...])
    v_ref[...] = _dot(mn, wv_ref[...])


def _memkv(mem, g, wk, wv):
    nb = mem.shape[0]
    wspec = pl.BlockSpec((None, D_MODEL, D_MODEL), lambda l, b: (l, 0, 0))
    ospec = pl.BlockSpec((None, None, MEM_LEN, D_MODEL), lambda l, b: (l, b, 0, 0))
    out = jax.ShapeDtypeStruct((DEPTH, nb, MEM_LEN, D_MODEL), F32)
    return pl.pallas_call(
        _memkv_kernel,
        out_shape=(out, out),
        grid=(DEPTH, nb),
        in_specs=[pl.BlockSpec((None, MEM_LEN, D_MODEL), lambda l, b: (b, 0, 0)),
                  pl.BlockSpec((None, 1, D_MODEL), lambda l, b: (l, 0, 0)), wspec, wspec],
        out_specs=(ospec, ospec),
        compiler_params=_params("parallel", "parallel"),
        name="memkv",
    )(mem, g, wk, wv)


def kernel(x_prompt, x_sample, mem_prompt, state_mlstm_C, state_mlstm_n, state_mlstm_m, state_lru_h,
           state_lru_conv, cache_mem_k, cache_mem_v, ffn1_norm, ffn1_w_gate, ffn1_w_up, ffn1_w_down,
           mix_norm, w_in, ml_b_i, ml_b_f, ml_out_norm, lru_conv_w, lru_conv_b, lru_w_a, lru_b_a,
           lru_w_i, lru_b_i, lru_lambda, w_out, xattn_norm, mem_norm, xattn_w_q, xattn_w_k, xattn_w_v,
           xattn_w_o, ffn2_norm, ffn2_w_gate, ffn2_w_up, ffn2_w_down, final_norm):
    bp, tp, _ = x_prompt.shape
    bs, ts, _ = x_sample.shape
    xp = x_prompt.reshape(bp * tp, D_MODEL)
    xs = x_sample.reshape(bs * ts, D_MODEL)

    bf = lambda w: w.astype(BF16)
    row = lambda v: v.astype(F32).reshape(DEPTH, 1, -1)
    gate_pad = ((0, 0), (0, 0), (0, LANES - 2 * ML_HEADS))
    p = {
        "mix_norm": row(mix_norm),
        "w_nat": bf(jnp.concatenate([w_in[:, :, 0:W_I], jnp.pad(w_in[:, :, W_I:W_X], gate_pad)], axis=2)),
        "w_xy": bf(w_in[:, :, W_X:]),
        "gbias": jnp.pad(jnp.concatenate([ml_b_i, ml_b_f], axis=1).astype(F32)[:, None, :], gate_pad),
        "ml_out_norm": ml_out_norm.astype(F32),
        "lru_conv_w": lru_conv_w, "lru_conv_b": row(lru_conv_b),
        "w_ai": bf(jnp.concatenate([lru_w_a, lru_w_i], axis=-1)),
        "lru_b_a": row(lru_b_a), "lru_b_i": row(lru_b_i), "lru_lambda": row(lru_lambda),
        "w_out": bf(w_out),
    }
    ffn1 = (row(ffn1_norm), bf(ffn1_w_gate), bf(ffn1_w_up), bf(ffn1_w_down))
    ffn2 = (row(ffn2_norm), bf(ffn2_w_gate), bf(ffn2_w_up), bf(ffn2_w_down))
    xa_g, xa_q, xa_o = row(xattn_norm), bf(xattn_w_q), bf(xattn_w_o)
    fin = final_norm.astype(F32).reshape(1, D_MODEL)

    pk, pv = _memkv(mem_prompt, row(mem_norm), bf(xattn_w_k), bf(xattn_w_v))
    sk = cache_mem_k.reshape(DEPTH, bs, MEM_LEN, D_MODEL)
    sv = cache_mem_v.reshape(DEPTH, bs, MEM_LEN, D_MODEL)
    s_state = (
        state_mlstm_C.astype(F32), state_mlstm_n.astype(F32),
        jnp.pad(state_mlstm_m.astype(F32)[:, :, None, :], ((0, 0), (0, 0), (0, 0), (0, LANES - ML_HEADS))),
        state_lru_h.astype(F32)[:, :, None, :],
        jnp.pad(state_lru_conv.astype(F32), ((0, 0), (0, 0), (SUBLANES - (CONV_W - 1), 0), (0, 0))),
    )

    p_out = [[] for _ in range(5)]
    s_out = [[] for _ in range(5)]
    for l in range(DEPTH):
        last = l == DEPTH - 1
        xp = _ffn(xp, *ffn1, fin, l, tm=1024, tf=1024, final_norm=False)
        xp, *st = _pmixer(xp, p, l, nb=bp, t=tp, tt=256, cl=CHUNK)
        xp = _xattn(xp, xa_g, xa_q, pk, pv, xa_o, l, nb=bp, t=tp, tt=512)
        xp = _ffn_res(xp, *ffn2, fin, l, tm=512, tf=512, final_norm=last)
        for acc, v in zip(p_out, st):
            acc.append(v)
        xs = _ffn(xs, *ffn1, fin, l, tm=bs * ts, tf=2048, final_norm=False)
        znat, zxy = _zproj(xs, p, l)
        xs, *st = _smixer(znat, zxy, xs, s_state, p, l, nb=bs, t=ts)
        xs = _xattn(xs, xa_g, xa_q, sk, sv, xa_o, l, nb=bs, t=ts, tt=ts, ns=4)
        xs = _ffn_res(xs, *ffn2, fin, l, tm=bs * ts, tf=1024, final_norm=last)
        for acc, v in zip(s_out, st):
            acc.append(v)

    def states(acc):
        c, n, m, h, cv = (jnp.stack(a) for a in acc)
        return c, n, m[:, :, 0, :ML_HEADS], h[:, :, 0, :], cv[:, :, SUBLANES - (CONV_W - 1):, :]

    return (xp.reshape(bp, tp, D_MODEL), xs.reshape(bs, ts, D_MODEL),
            *states(p_out),
            pk.reshape(DEPTH, bp, MEM_LEN, XA_HEADS, XA_DH), pv.reshape(DEPTH, bp, MEM_LEN, XA_HEADS, XA_DH),
            *states(s_out))
```

```python
import functools

import jax
import jax.numpy as jnp
from jax import lax
from jax.experimental import pallas as pl
from jax.experimental.pallas import tpu as pltpu

F32 = jnp.float32
BF16 = jnp.bfloat16

D_MODEL = 1024
DEPTH = 2
ML_BLOCK = 128
ML_HEADS = 4
ML_DV = 256
ML_DK = 128
ML_WIDTH = ML_HEADS * ML_DV
LRU_WIDTH = 1024
LRU_BLOCKS = 8
LRU_BW = LRU_WIDTH // LRU_BLOCKS
CONV_W = 4
LRU_C = 8.0
MEM_LEN = 256
XA_HEADS = 4
XA_DH = D_MODEL // XA_HEADS
D_FF = 4 * D_MODEL
EPS = 1e-6

LANES = 128
SUBLANES = 8
VMEM_LIMIT = 56 * 1024 * 1024

Z_Q = 0
Z_K = Z_Q + ML_HEADS * ML_DK
Z_V = Z_K + ML_HEADS * ML_DK
Z_O = Z_V + ML_WIDTH
Z_G = Z_O + ML_WIDTH
Z_NAT = Z_G + LANES
Z_XY = 2 * LRU_WIDTH
W_I = Z_O + ML_WIDTH
W_X = W_I + 2 * ML_HEADS


def _rms(x, g):
    return x * lax.rsqrt(jnp.mean(x * x, axis=-1, keepdims=True) + EPS) * g


def _nt_dot(a, b):
    return lax.dot_general(a, b, (((1,), (1,)), ((), ())), preferred_element_type=F32)


def _tn_dot(a, b):
    return lax.dot_general(a, b, (((0,), (0,)), ((), ())), preferred_element_type=F32)


def _dot(a, b):
    return jnp.dot(a, b, preferred_element_type=F32)


def _params(*sem):
    return pltpu.CompilerParams(dimension_semantics=sem, vmem_limit_bytes=VMEM_LIMIT)


def _resident(shape, index_map):
    return pl.BlockSpec(shape, index_map, pipeline_mode=pl.Buffered(1))


def _ffn_kernel(x_ref, g_ref, wg_ref, wu_ref, wd_ref, fg_ref, o_ref, xn_s, acc_s, *, final_norm):
    j = pl.program_id(1)

    @pl.when(j == 0)
    def _():
        xn_s[...] = _rms(x_ref[...], g_ref[...]).astype(BF16)
        acc_s[...] = jnp.zeros_like(acc_s)

    xn = xn_s[...]
    g = _dot(xn, wg_ref[...])
    u = _dot(xn, wu_ref[...])
    h = (g * jax.nn.sigmoid(g) * u).astype(BF16)
    acc_s[...] += _dot(h, wd_ref[...])

    @pl.when(j == pl.num_programs(1) - 1)
    def _():
        y = x_ref[...] + 0.5 * acc_s[...]
        if final_norm:
            y = _rms(y, fg_ref[...])
        o_ref[...] = y


def _ffn(x, g, wg, wu, wd, fg, l, *, tm, tf, final_norm):
    m = x.shape[0]
    return pl.pallas_call(
        functools.partial(_ffn_kernel, final_norm=final_norm),
        out_shape=jax.ShapeDtypeStruct((m, D_MODEL), F32),
        grid=(m // tm, D_FF // tf),
        in_specs=[
            pl.BlockSpec((tm, D_MODEL), lambda i, j: (i, 0)),
            pl.BlockSpec((None, 1, D_MODEL), lambda i, j: (l, 0, 0)),
            pl.BlockSpec((None, D_MODEL, tf), lambda i, j: (l, 0, j)),
            pl.BlockSpec((None, D_MODEL, tf), lambda i, j: (l, 0, j)),
            pl.BlockSpec((None, tf, D_MODEL), lambda i, j: (l, j, 0)),
            pl.BlockSpec((1, D_MODEL), lambda i, j: (0, 0)),
        ],
        out_specs=pl.BlockSpec((tm, D_MODEL), lambda i, j: (i, 0)),
        scratch_shapes=[pltpu.VMEM((tm, D_MODEL), BF16), pltpu.VMEM((tm, D_MODEL), F32)],
        compiler_params=_params("parallel", "arbitrary"),
        name="ffn",
    )(x, g, wg, wu, wd, fg)


def _ffn_res_kernel(x_ref, g_ref, wg_ref, wu_ref, wd_ref, fg_ref, o_ref, h_s, *, tf, final_norm):
    x = x_ref[...]
    xn = _rms(x, g_ref[...]).astype(BF16)
    for j in range(D_FF // tf):
        cols = slice(j * tf, (j + 1) * tf)
        g = _dot(xn, wg_ref[:, cols])
        u = _dot(xn, wu_ref[:, cols])
        h_s[:, cols] = (g * jax.nn.sigmoid(g) * u).astype(BF16)
    y = x + 0.5 * _dot(h_s[...], wd_ref[...])
    if final_norm:
        y = _rms(y, fg_ref[...])
    o_ref[...] = y


def _ffn_res(x, g, wg, wu, wd, fg, l, *, tm, tf, final_norm):
    m = x.shape[0]
    return pl.pallas_call(
        functools.partial(_ffn_res_kernel, tf=tf, final_norm=final_norm),
        out_shape=jax.ShapeDtypeStruct((m, D_MODEL), F32),
        grid=(m // tm,),
        in_specs=[
            pl.BlockSpec((tm, D_MODEL), lambda i: (i, 0)),
            _resident((None, 1, D_MODEL), lambda i: (l, 0, 0)),
            _resident((None, D_MODEL, D_FF), lambda i: (l, 0, 0)),
            _resident((None, D_MODEL, D_FF), lambda i: (l, 0, 0)),
            _resident((None, D_FF, D_MODEL), lambda i: (l, 0, 0)),
            _resident((1, D_MODEL), lambda i: (0, 0)),
        ],
        out_specs=pl.BlockSpec((tm, D_MODEL), lambda i: (i, 0)),
        scratch_shapes=[pltpu.VMEM((tm, D_FF), BF16)],
        compiler_params=_params("parallel"),
        name="ffn_res",
    )(x, g, wg, wu, wd, fg)


def _log_sigmoid(x):
    return jnp.minimum(x, 0.0) - jnp.log1p(jnp.exp(-jnp.abs(x)))


def _softplus(x):
    return jnp.maximum(x, 0.0) + jnp.log1p(jnp.exp(-jnp.abs(x)))


def _lane_pick(x, lane_ids, idx):
    return jnp.sum(jnp.where(lane_ids == idx, x, 0.0), axis=1, keepdims=True)


def _mlstm_blocks(z_ref, gbias_ref, onorm_ref, c_ref, n_ref, m_ref, hm_s, rows, cl):
    lane1 = lax.broadcasted_iota(jnp.int32, (1, LANES), 1)
    m_vec = m_ref[0]
    m_heads = [_lane_pick(m_vec, lane1, h) for h in range(ML_HEADS)]
    for c in range(rows // cl):
        m_heads = _mlstm_chunk(z_ref, gbias_ref, onorm_ref, c_ref, n_ref, m_heads, hm_s, c * cl, cl)
    for h in range(ML_HEADS):
        m_vec = jnp.where(lane1 == h, m_heads[h], m_vec)
    m_ref[0] = m_vec


def _mlstm_chunk(z_ref, gbias_ref, onorm_ref, c_ref, n_ref, m_heads, hm_s, r0, cl):
    lane = lax.broadcasted_iota(jnp.int32, (cl, LANES), 1)
    row = lax.broadcasted_iota(jnp.int32, (cl, LANES), 0)
    tril = (lax.broadcasted_iota(jnp.int32, (cl, cl), 1)
            <= lax.broadcasted_iota(jnp.int32, (cl, cl), 0))

    gates = z_ref[r0:r0 + cl, Z_G:Z_G + LANES] + gbias_ref[...]
    bsum = _log_sigmoid(gates)
    d = 1
    while d < cl:
        bsum = bsum + jnp.where(row >= d, pltpu.roll(bsum, d, axis=0), 0.0)
        d *= 2
    mixed = jnp.where(lane < ML_HEADS, gates, bsum)
    if cl < LANES:
        mixed = jnp.concatenate([mixed, jnp.zeros((LANES - cl, LANES), F32)], axis=0)
    mixed_t = mixed.T

    m_next = []
    for h in range(ML_HEADS):
        b_col = _lane_pick(bsum, lane, ML_HEADS + h)
        ig_col = _lane_pick(gates, lane, h)
        b_row = mixed_t[ML_HEADS + h:ML_HEADS + h + 1, 0:cl]
        ig_row = mixed_t[h:h + 1, 0:cl]
        m_prev = m_heads[h]

        dmat = jnp.where(tril, b_col - b_row + ig_row, -jnp.inf)
        inter = b_col + m_prev
        m_t = jnp.maximum(inter, jnp.max(dmat, axis=1, keepdims=True))

        qf = z_ref[r0:r0 + cl, Z_Q + h * ML_DK:Z_Q + (h + 1) * ML_DK] * (ML_DK ** -0.5)
        kf = z_ref[r0:r0 + cl, Z_K + h * ML_DK:Z_K + (h + 1) * ML_DK]
        vf = z_ref[r0:r0 + cl, Z_V + h * ML_DV:Z_V + (h + 1) * ML_DV]
        q = qf.astype(BF16)
        k = kf.astype(BF16)

        s = _nt_dot(q, k) * jnp.exp(dmat - m_t)
        w_inter = jnp.exp(inter - m_t)
        c_old = c_ref[0, h]
        n_old = n_ref[0, h:h + 1, :]
        num = w_inter * _nt_dot(q, c_old.astype(BF16)) + _dot(s.astype(BF16), vf.astype(BF16))
        den = (w_inter * jnp.sum(qf * n_old, axis=1, keepdims=True)
               + jnp.sum(s, axis=1, keepdims=True))
        hh = num / jnp.maximum(jnp.abs(den), jnp.exp(-m_t))

        hh = hh * lax.rsqrt(jnp.mean(hh * hh, axis=1, keepdims=True) + EPS) * onorm_ref[h:h + 1, :]
        o_gate = jax.nn.sigmoid(z_ref[r0:r0 + cl, Z_O + h * ML_DV:Z_O + (h + 1) * ML_DV])
        hm_s[r0:r0 + cl, h * ML_DV:(h + 1) * ML_DV] = (o_gate * hh).astype(BF16)

        m_new = m_t[cl - 1:cl, :]
        b_last = b_col[cl - 1:cl, :]
        wgt = jnp.exp(b_last - b_col + ig_col - m_new)
        decay = jnp.exp(b_last + m_prev - m_new)
        c_ref[0, h] = decay * c_old + _tn_dot((wgt * vf).astype(BF16), k)
        n_ref[0, h:h + 1, :] = decay * n_old + jnp.sum(wgt * kf, axis=0, keepdims=True)
        m_next.append(m_new)
    return m_next


def _lru_gates(conv, wai_ref, ba_ref, bi_ref, lam_ref):
    conv_b = conv.astype(BF16)
    pre = [_dot(conv_b[:, n * LRU_BW:(n + 1) * LRU_BW], wai_ref[n])
           for n in range(LRU_BLOCKS)]
    r_gate = jax.nn.sigmoid(jnp.concatenate([p[:, :LRU_BW] for p in pre], axis=1) + ba_ref[...])
    i_gate = jax.nn.sigmoid(jnp.concatenate([p[:, LRU_BW:] for p in pre], axis=1) + bi_ref[...])
    log_a = r_gate * (-LRU_C * _softplus(-lam_ref[...]))
    a = jnp.exp(log_a)
    w = -jnp.tanh(log_a) * (a * a + 1.0)
    xin = jnp.where(w > 0.0, w * lax.rsqrt(w), 0.0) * (i_gate * conv)
    return a, xin


def _pmixer_project(x_ref, gmix_ref, wnat_ref, wxy_ref, z_dst, zxy_dst, pbuf, *, tt):
    seg = tt // SUBLANES
    pitch = seg + SUBLANES
    xn = _rms(x_ref[...], gmix_ref[...])
    z_dst[...] = _dot(xn.astype(BF16), wnat_ref[...])
    for k in range(LRU_WIDTH // LANES):
        for s in range(SUBLANES):
            pbuf[k, pl.ds(s * pitch, seg), :] = xn[s * seg:(s + 1) * seg, k * LANES:(k + 1) * LANES]
    xnp = jnp.stack(
        [jnp.concatenate([pbuf[k, pl.ds(j, SUBLANES, stride=pitch), :]
                          for k in range(LRU_WIDTH // LANES)], axis=1) for j in range(seg)], axis=0)
    zxy_dst[...] = _dot(xnp.reshape(tt, D_MODEL).astype(BF16), wxy_ref[...])


def _pmixer_mix(x_ref, z_s, zxy_s, gbias_ref, onorm_ref, convw_ref, convb_ref, wai_ref, ba_ref, bi_ref,
                lam_ref, wout_ref, y_ref, c_ref, n_ref, m_ref, h_ref, cv_ref, obuf, hm_s, *, cl, tt):
    seg = tt // SUBLANES
    pitch = seg + SUBLANES

    u3 = zxy_s[:, 0:LRU_WIDTH].reshape(seg, SUBLANES, LRU_WIDTH)
    tail = cv_ref[0]
    sub = lax.broadcasted_iota(jnp.int32, (SUBLANES, LRU_WIDTH), 0)
    wrap = []
    for i in range(CONV_W - 1):
        prev = pltpu.roll(u3[seg - (CONV_W - 1) + i], 1, axis=0)
        fill = tail[SUBLANES - (CONV_W - 1) + i:SUBLANES - (CONV_W - 1) + i + 1, :]
        wrap.append(jnp.where(sub == 0, fill, prev))
        cv_ref[0, SUBLANES - (CONV_W - 1) + i:SUBLANES - (CONV_W - 1) + i + 1, :] = (
            u3[seg - (CONV_W - 1) + i][SUBLANES - 1:SUBLANES, :])
    ext = jnp.concatenate([jnp.stack(wrap, axis=0), u3], axis=0)
    conv3 = convb_ref[...] + ext[CONV_W - 1:] * convw_ref[CONV_W - 1:CONV_W, :]
    for j in range(1, CONV_W):
        conv3 = conv3 + ext[CONV_W - 1 - j:CONV_W - 1 - j + seg] * convw_ref[CONV_W - 1 - j:CONV_W - j, :]

    a, xin = _lru_gates(conv3.reshape(tt, LRU_WIDTH), wai_ref, ba_ref, bi_ref, lam_ref)

    a3 = a.reshape(seg, SUBLANES, LRU_WIDTH)
    x3 = xin.reshape(seg, SUBLANES, LRU_WIDTH)
    hs = [x3[0]]
    ps = [a3[0]]
    for j in range(1, seg):
        hs.append(a3[j] * hs[-1] + x3[j])
        ps.append(a3[j] * ps[-1])
    carry = h_ref[0]
    cin = []
    for s in range(SUBLANES):
        cin.append(carry)
        carry = hs[-1][s:s + 1, :] + ps[-1][s:s + 1, :] * carry
    h_ref[0] = carry
    cin = jnp.concatenate(cin, axis=0)
    h3 = jnp.stack([hs[j] + ps[j] * cin for j in range(seg)], axis=0)

    hl = (jax.nn.gelu(zxy_s[:, LRU_WIDTH:]) * h3.reshape(tt, LRU_WIDTH)).astype(BF16)
    ol3 = _dot(hl, wout_ref[ML_WIDTH:ML_WIDTH + LRU_WIDTH, :]).reshape(seg, SUBLANES, D_MODEL)
    for j in range(seg):
        for k in range(D_MODEL // LANES):
            obuf[k, pl.ds(j, SUBLANES, stride=pitch), :] = ol3[j][:, k * LANES:(k + 1) * LANES]
    out_lru = jnp.concatenate(
        [jnp.concatenate([obuf[k, pl.ds(s * pitch, seg), :] for s in range(SUBLANES)], axis=0)
         for k in range(D_MODEL // LANES)], axis=1)

    _mlstm_blocks(z_s, gbias_ref, onorm_ref, c_ref, n_ref, m_ref, hm_s, tt, cl)

    y_ref[...] = x_ref[...] + _dot(hm_s[...], wout_ref[0:ML_WIDTH, :]) + out_lru


def _pmixer_kernel(x_ref, gmix_ref, wnat_ref, wxy_ref, gbias_ref, onorm_ref, convw_ref,
                   convb_ref, wai_ref, ba_ref, bi_ref, lam_ref, wout_ref,
                   y_ref, c_ref, n_ref, m_ref, h_ref, cv_ref,
                   z_s, zxy_s, pbuf, obuf, hm_s, *, cl, tt):
    @pl.when(pl.program_id(1) == 0)
    def _():
        c_ref[...] = jnp.zeros_like(c_ref)
        n_ref[...] = jnp.zeros_like(n_ref)
        m_ref[...] = jnp.zeros_like(m_ref)
        h_ref[...] = jnp.zeros_like(h_ref)
        cv_ref[...] = jnp.zeros_like(cv_ref)

    _pmixer_project(x_ref, gmix_ref, wnat_ref, wxy_ref, z_s, zxy_s, pbuf, tt=tt)
    _pmixer_mix(x_ref, z_s, zxy_s, gbias_ref, onorm_ref, convw_ref, convb_ref, wai_ref, ba_ref,
                bi_ref, lam_ref, wout_ref, y_ref, c_ref, n_ref, m_ref, h_ref, cv_ref, obuf, hm_s,
                cl=cl, tt=tt)


def _pmixer(x, p, l, *, nb, t, tt, cl):
    nt = t // tt
    pitch = tt // SUBLANES + SUBLANES
    tok = lambda b, i: (b * nt + i, 0)
    st4 = lambda b, i: (b, 0, 0, 0)
    st3 = lambda b, i: (b, 0, 0)
    lay3 = lambda b, i: (l, 0, 0)
    lay4 = lambda b, i: (l, 0, 0, 0)
    return pl.pallas_call(
        functools.partial(_pmixer_kernel, cl=cl, tt=tt),
        out_shape=(
            jax.ShapeDtypeStruct((nb * t, D_MODEL), F32),
            jax.ShapeDtypeStruct((nb, ML_HEADS, ML_DV, ML_DK), F32),
            jax.ShapeDtypeStruct((nb, ML_HEADS, ML_DK), F32),
            jax.ShapeDtypeStruct((nb, 1, LANES), F32),
            jax.ShapeDtypeStruct((nb, 1, LRU_WIDTH), F32),
            jax.ShapeDtypeStruct((nb, SUBLANES, LRU_WIDTH), F32),
        ),
        grid=(nb, nt),
        in_specs=[
            pl.BlockSpec((tt, D_MODEL), tok),
            _resident((None, 1, D_MODEL), lay3),
            _resident((None, D_MODEL, Z_NAT), lay3),
            _resident((None, D_MODEL, Z_XY), lay3),
            _resident((None, 1, LANES), lay3),
            _resident((None, ML_HEADS, ML_DV), lay3),
            _resident((None, CONV_W, LRU_WIDTH), lay3),
            _resident((None, 1, LRU_WIDTH), lay3),
            _resident((None, LRU_BLOCKS, LRU_BW, 2 * LRU_BW), lay4),
            _resident((None, 1, LRU_WIDTH), lay3),
            _resident((None, 1, LRU_WIDTH), lay3),
            _resident((None, 1, LRU_WIDTH), lay3),
            _resident((None, ML_WIDTH + LRU_WIDTH, D_MODEL), lay3),
        ],
        out_specs=(
            pl.BlockSpec((tt, D_MODEL), tok),
            pl.BlockSpec((1, ML_HEADS, ML_DV, ML_DK), st4),
            pl.BlockSpec((1, ML_HEADS, ML_DK), st3),
            pl.BlockSpec((1, 1, LANES), st3),
            pl.BlockSpec((1, 1, LRU_WIDTH), st3),
            pl.BlockSpec((1, SUBLANES, LRU_WIDTH), st3),
        ),
        scratch_shapes=[pltpu.VMEM((tt, Z_NAT), F32), pltpu.VMEM((tt, Z_XY), F32),
                        pltpu.VMEM((LRU_WIDTH // LANES, SUBLANES * pitch, LANES), F32),
                        pltpu.VMEM((D_MODEL // LANES, SUBLANES * pitch, LANES), F32),
                        pltpu.VMEM((tt, ML_WIDTH), BF16)],
        compiler_params=_params("parallel", "arbitrary"),
        name="pmixer",
    )(x, p["mix_norm"], p["w_nat"], p["w_xy"], p["gbias"], p["ml_out_norm"], p["lru_conv_w"],
      p["lru_conv_b"], p["w_ai"], p["lru_b_a"], p["lru_b_i"], p["lru_lambda"], p["w_out"])


def _zproj_kernel(x_ref, g_ref, wnat_ref, wxy_ref, znat_ref, zxy_ref):
    xn = _rms(x_ref[...], g_ref[...]).astype(BF16)
    znat_ref[...] = _dot(xn, wnat_ref[...])
    zxy_ref[...] = _dot(xn, wxy_ref[...])


def _zproj(x, p, l):
    m = x.shape[0]
    lay3 = lambda i: (l, 0, 0)
    return pl.pallas_call(
        _zproj_kernel,
        out_shape=(jax.ShapeDtypeStruct((m, Z_NAT), F32), jax.ShapeDtypeStruct((m, Z_XY), F32)),
        grid=(1,),
        in_specs=[
            pl.BlockSpec((m, D_MODEL), lambda i: (0, 0)),
            pl.BlockSpec((None, 1, D_MODEL), lay3),
            pl.BlockSpec((None, D_MODEL, Z_NAT), lay3),
            pl.BlockSpec((None, D_MODEL, Z_XY), lay3),
        ],
        out_specs=(pl.BlockSpec((m, Z_NAT), lambda i: (0, 0)), pl.BlockSpec((m, Z_XY), lambda i: (0, 0))),
        compiler_params=_params("arbitrary"),
        name="zproj",
    )(x, p["mix_norm"], p["w_nat"], p["w_xy"])


def _smixer_kernel(znat_ref, zxy_ref, x_ref, c0_ref, n0_ref, m0_ref, h0_ref, cv0_ref,
                   gbias_ref, onorm_ref, convw_ref, convb_ref, wai_ref, ba_ref, bi_ref, lam_ref,
                   wout_ref,
                   y_ref, c_ref, n_ref, m_ref, h_ref, cv_ref,
                   ubuf_s, hm_s, *, cl, tt):
    c_ref[...] = c0_ref[...]
    n_ref[...] = n0_ref[...]
    m_ref[...] = m0_ref[...]

    _mlstm_blocks(znat_ref, gbias_ref, onorm_ref, c_ref, n_ref, m_ref, hm_s, tt, cl)

    u = zxy_ref[:, 0:LRU_WIDTH]
    ubuf_s[0:SUBLANES, :] = cv0_ref[0]
    ubuf_s[SUBLANES:SUBLANES + tt, :] = u
    cv_ref[0] = ubuf_s[tt:tt + SUBLANES, :]
    conv = convb_ref[...] + u * convw_ref[CONV_W - 1:CONV_W, :]
    for j in range(1, CONV_W):
        conv = conv + (ubuf_s[pl.ds(SUBLANES - j, tt), :]
                       * convw_ref[CONV_W - 1 - j:CONV_W - j, :])
    a, xin = _lru_gates(conv, wai_ref, ba_ref, bi_ref, lam_ref)

    rmod = lax.broadcasted_iota(jnp.int32, (tt, LRU_WIDTH), 0) & (SUBLANES - 1)
    d = 1
    while d < SUBLANES:
        keep = rmod >= d
        xin = xin + a * jnp.where(keep, pltpu.roll(xin, d, axis=0), 0.0)
        a = a * jnp.where(keep, pltpu.roll(a, d, axis=0), 1.0)
        d *= 2
    carry = h0_ref[0]
    groups = []
    for gidx in range(tt // SUBLANES):
        lo = gidx * SUBLANES
        hb = xin[lo:lo + SUBLANES, :] + a[lo:lo + SUBLANES, :] * carry
        carry = hb[SUBLANES - 1:SUBLANES, :]
        groups.append(hb)
    h_ref[0] = carry
    h_lru = jnp.concatenate(groups, axis=0)

    hl = (jax.nn.gelu(zxy_ref[:, LRU_WIDTH:]) * h_lru).astype(BF16)
    out = (_dot(hm_s[...], wout_ref[0:ML_WIDTH, :])
           + _dot(hl, wout_ref[ML_WIDTH:ML_WIDTH + LRU_WIDTH, :]))
    y_ref[...] = x_ref[...] + out


def _smixer(znat, zxy, x, st, p, l, *, nb, t):
    tok = lambda b: (b, 0)
    lay3 = lambda b: (l, 0, 0)
    lay4 = lambda b: (l, 0, 0, 0)
    st3 = lambda b: (b, 0, 0)
    c0, n0, m0, h0, cv0 = st
    return pl.pallas_call(
        functools.partial(_smixer_kernel, cl=t, tt=t),
        out_shape=(
            jax.ShapeDtypeStruct((nb * t, D_MODEL), F32),
            jax.ShapeDtypeStruct((nb, ML_HEADS, ML_DV, ML_DK), F32),
            jax.ShapeDtypeStruct((nb, ML_HEADS, ML_DK), F32),
            jax.ShapeDtypeStruct((nb, 1, LANES), F32),
            jax.ShapeDtypeStruct((nb, 1, LRU_WIDTH), F32),
            jax.ShapeDtypeStruct((nb, SUBLANES, LRU_WIDTH), F32),
        ),
        grid=(nb,),
        in_specs=[
            pl.BlockSpec((t, Z_NAT), tok),
            pl.BlockSpec((t, Z_XY), tok),
            pl.BlockSpec((t, D_MODEL), tok),
            pl.BlockSpec((None, 1, ML_HEADS, ML_DV, ML_DK), lambda b: (l, b, 0, 0, 0)),
            pl.BlockSpec((None, 1, ML_HEADS, ML_DK), lambda b: (l, b, 0, 0)),
            pl.BlockSpec((None, 1, 1, LANES), lambda b: (l, b, 0, 0)),
            pl.BlockSpec((None, 1, 1, LRU_WIDTH), lambda b: (l, b, 0, 0)),
            pl.BlockSpec((None, 1, SUBLANES, LRU_WIDTH), lambda b: (l, b, 0, 0)),
            pl.BlockSpec((None, 1, LANES), lay3),
            pl.BlockSpec((None, ML_HEADS, ML_DV), lay3),
            pl.BlockSpec((None, CONV_W, LRU_WIDTH), lay3),
            pl.BlockSpec((None, 1, LRU_WIDTH), lay3),
            pl.BlockSpec((None, LRU_BLOCKS, LRU_BW, 2 * LRU_BW), lay4),
            pl.BlockSpec((None, 1, LRU_WIDTH), lay3),
            pl.BlockSpec((None, 1, LRU_WIDTH), lay3),
            pl.BlockSpec((None, 1, LRU_WIDTH), lay3),
            pl.BlockSpec((None, ML_WIDTH + LRU_WIDTH, D_MODEL), lay3),
        ],
        out_specs=(
            pl.BlockSpec((t, D_MODEL), tok),
            pl.BlockSpec((1, ML_HEADS, ML_DV, ML_DK), lambda b: (b, 0, 0, 0)),
            pl.BlockSpec((1, ML_HEADS, ML_DK), st3),
            pl.BlockSpec((1, 1, LANES), st3),
            pl.BlockSpec((1, 1, LRU_WIDTH), st3),
            pl.BlockSpec((1, SUBLANES, LRU_WIDTH), st3),
        ),
        scratch_shapes=[pltpu.VMEM((t + SUBLANES, LRU_WIDTH), F32),
                        pltpu.VMEM((t, ML_WIDTH), BF16)],
        compiler_params=_params("parallel"),
        name="smixer",
    )(znat, zxy, x, c0, n0, m0, h0, cv0, p["gbias"], p["ml_out_norm"], p["lru_conv_w"],
      p["lru_conv_b"], p["w_ai"], p["lru_b_a"], p["lru_b_i"], p["lru_lambda"], p["w_out"])


def _xattn_kernel(x_ref, g_ref, wq_ref, mk_ref, mv_ref, wo_ref, y_ref, o_s, *, ns, rows):
    x = x_ref[...]
    xn = _rms(x, g_ref[...]).astype(BF16)
    q = _dot(xn, wq_ref[...])
    for s in range(ns):
        r = slice(s * rows, (s + 1) * rows)
        for h in range(XA_HEADS):
            cols = slice(h * XA_DH, (h + 1) * XA_DH)
            sc = _nt_dot(q[r, cols].astype(BF16), mk_ref[s, :, cols].astype(BF16)) * (XA_DH ** -0.5)
            p = jnp.exp(sc - jnp.max(sc, axis=1, keepdims=True))
            p = p / jnp.sum(p, axis=1, keepdims=True)
            o_s[r, cols] = _dot(p.astype(BF16), mv_ref[s, :, cols].astype(BF16)).astype(BF16)
    y_ref[...] = x + _dot(o_s[...], wo_ref[...])


def _xattn(x, g, wq, mk, mv, wo, l, *, nb, t, tt, ns=1):
    nt = t // tt
    assert ns == 1 or nt == 1
    tok = lambda b, i: (b * nt + i, 0)
    lay3 = lambda b, i: (l, 0, 0)
    mem = lambda b, i: (l, b, 0, 0)
    return pl.pallas_call(
        functools.partial(_xattn_kernel, ns=ns, rows=tt),
        out_shape=jax.ShapeDtypeStruct((nb * t, D_MODEL), F32),
        grid=(nb // ns, nt),
        in_specs=[
            pl.BlockSpec((ns * tt, D_MODEL), tok),
            pl.BlockSpec((None, 1, D_MODEL), lay3),
            pl.BlockSpec((None, D_MODEL, D_MODEL), lay3),
            pl.BlockSpec((None, ns, MEM_LEN, D_MODEL), mem),
            pl.BlockSpec((None, ns, MEM_LEN, D_MODEL), mem),
            pl.BlockSpec((None, D_MODEL, D_MODEL), lay3),
        ],
        out_specs=pl.BlockSpec((ns * tt, D_MODEL), tok),
        scratch_shapes=[pltpu.VMEM((ns * tt, D_MODEL), BF16)],
        compiler_params=_params("parallel", "arbitrary"),
        name="xattn",
    )(x, g, wq, mk, mv, wo)


def _memkv_kernel(mem_ref, g_ref, wk_ref, wv_ref, k_ref, v_ref):
    mn = _rms(mem_ref[...], g_ref[...]).astype(BF16)
    k_ref[...] = _dot(mn, wk_ref[...])
    v_ref[...] = _dot(mn, wv_ref[...])


def _memkv(mem, g, wk, wv):
    nb = mem.shape[0]
    wspec = pl.BlockSpec((None, D_MODEL, D_MODEL), lambda l, b: (l, 0, 0))
    ospec = pl.BlockSpec((None, None, MEM_LEN, D_MODEL), lambda l, b: (l, b, 0, 0))
    out = jax.ShapeDtypeStruct((DEPTH, nb, MEM_LEN, D_MODEL), F32)
    return pl.pallas_call(
        _memkv_kernel,
        out_shape=(out, out),
        grid=(DEPTH, nb),
        in_specs=[pl.BlockSpec((None, MEM_LEN, D_MODEL), lambda l, b: (b, 0, 0)),
                  pl.BlockSpec((None, 1, D_MODEL), lambda l, b: (l, 0, 0)), wspec, wspec],
        out_specs=(ospec, ospec),
        compiler_params=_params("parallel", "parallel"),
        name="memkv",
    )(mem, g, wk, wv)


def kernel(x_prompt, x_sample, mem_prompt, state_mlstm_C, state_mlstm_n, state_mlstm_m, state_lru_h,
           state_lru_conv, cache_mem_k, cache_mem_v, ffn1_norm, ffn1_w_gate, ffn1_w_up, ffn1_w_down,
           mix_norm, w_in, ml_b_i, ml_b_f, ml_out_norm, lru_conv_w, lru_conv_b, lru_w_a, lru_b_a,
           lru_w_i, lru_b_i, lru_lambda, w_out, xattn_norm, mem_norm, xattn_w_q, xattn_w_k, xattn_w_v,
           xattn_w_o, ffn2_norm, ffn2_w_gate, ffn2_w_up, ffn2_w_down, final_norm):
    bp, tp, _ = x_prompt.shape
    bs, ts, _ = x_sample.shape
    xp = x_prompt.reshape(bp * tp, D_MODEL)
    xs = x_sample.reshape(bs * ts, D_MODEL)

    bf = lambda w: w.astype(BF16)
    row = lambda v: v.astype(F32).reshape(DEPTH, 1, -1)
    gate_pad = ((0, 0), (0, 0), (0, LANES - 2 * ML_HEADS))
    p = {
        "mix_norm": row(mix_norm),
        "w_nat": bf(jnp.concatenate([w_in[:, :, 0:W_I], jnp.pad(w_in[:, :, W_I:W_X], gate_pad)], axis=2)),
        "w_xy": bf(w_in[:, :, W_X:]),
        "gbias": jnp.pad(jnp.concatenate([ml_b_i, ml_b_f], axis=1).astype(F32)[:, None, :], gate_pad),
        "ml_out_norm": ml_out_norm.astype(F32),
        "lru_conv_w": lru_conv_w, "lru_conv_b": row(lru_conv_b),
        "w_ai": bf(jnp.concatenate([lru_w_a, lru_w_i], axis=-1)),
        "lru_b_a": row(lru_b_a), "lru_b_i": row(lru_b_i), "lru_lambda": row(lru_lambda),
        "w_out": bf(w_out),
    }
    ffn1 = (row(ffn1_norm), bf(ffn1_w_gate), bf(ffn1_w_up), bf(ffn1_w_down))
    ffn2 = (row(ffn2_norm), bf(ffn2_w_gate), bf(ffn2_w_up), bf(ffn2_w_down))
    xa_g, xa_q, xa_o = row(xattn_norm), bf(xattn_w_q), bf(xattn_w_o)
    fin = final_norm.astype(F32).reshape(1, D_MODEL)

    pk, pv = _memkv(mem_prompt, row(mem_norm), bf(xattn_w_k), bf(xattn_w_v))
    sk = cache_mem_k.reshape(DEPTH, bs, MEM_LEN, D_MODEL)
    sv = cache_mem_v.reshape(DEPTH, bs, MEM_LEN, D_MODEL)
    s_state = (
        state_mlstm_C.astype(F32), state_mlstm_n.astype(F32),
        jnp.pad(state_mlstm_m.astype(F32)[:, :, None, :], ((0, 0), (0, 0), (0, 0), (0, LANES - ML_HEADS))),
        state_lru_h.astype(F32)[:, :, None, :],
        jnp.pad(state_lru_conv.astype(F32), ((0, 0), (0, 0), (SUBLANES - (CONV_W - 1), 0), (0, 0))),
    )

    p_out = [[] for _ in range(5)]
    s_out = [[] for _ in range(5)]
    for l in range(DEPTH):
        last = l == DEPTH - 1
        xp = _ffn_res(xp, *ffn1, fin, l, tm=512, tf=512, final_norm=False)
        xp, *st = _pmixer(xp, p, l, nb=bp, t=tp, tt=512, cl=ML_BLOCK)
        xp = _xattn(xp, xa_g, xa_q, pk, pv, xa_o, l, nb=bp, t=tp, tt=512)
        xp = _ffn_res(xp, *ffn2, fin, l, tm=512, tf=512, final_norm=last)
        for acc, v in zip(p_out, st):
            acc.append(v)
        xs = _ffn(xs, *ffn1, fin, l, tm=bs * ts, tf=2048, final_norm=False)
        znat, zxy = _zproj(xs, p, l)
        xs, *st = _smixer(znat, zxy, xs, s_state, p, l, nb=bs, t=ts)
        xs = _xattn(xs, xa_g, xa_q, sk, sv, xa_o, l, nb=bs, t=ts, tt=ts, ns=4)
        xs = _ffn_res(xs, *ffn2, fin, l, tm=bs * ts, tf=1024, final_norm=last)
        for acc, v in zip(s_out, st):
            acc.append(v)

    def states(acc):
        c, n, m, h, cv = (jnp.stack(a) for a in acc)
        return c, n, m[:, :, 0, :ML_HEADS], h[:, :, 0, :], cv[:, :, SUBLANES - (CONV_W - 1):, :]

    return (xp.reshape(bp, tp, D_MODEL), xs.reshape(bs, ts, D_MODEL),
            *states(p_out),
            pk.reshape(DEPTH, bp, MEM_LEN, XA_HEADS, XA_DH), pv.reshape(DEPTH, bp, MEM_LEN, XA_HEADS, XA_DH),
            *states(s_out))
```

```python
import functools

import jax
import jax.numpy as jnp
from jax import lax
from jax.experimental import pallas as pl
from jax.experimental.pallas import tpu as pltpu

F32 = jnp.float32
BF16 = jnp.bfloat16

D_MODEL = 1024
DEPTH = 2
ML_BLOCK = 128
ML_HEADS = 4
ML_DV = 256
ML_DK = 128
ML_WIDTH = ML_HEADS * ML_DV
LRU_WIDTH = 1024
LRU_BLOCKS = 8
LRU_BW = LRU_WIDTH // LRU_BLOCKS
CONV_W = 4
LRU_C = 8.0
MEM_LEN = 256
XA_HEADS = 4
XA_DH = D_MODEL // XA_HEADS
D_FF = 4 * D_MODEL
EPS = 1e-6

LANES = 128
SUBLANES = 8
VMEM_LIMIT = 56 * 1024 * 1024

Z_Q = 0
Z_K = Z_Q + ML_HEADS * ML_DK
Z_V = Z_K + ML_HEADS * ML_DK
Z_O = Z_V + ML_WIDTH
Z_G = Z_O + ML_WIDTH
Z_NAT = Z_G + LANES
Z_XY = 2 * LRU_WIDTH
W_X = Z_G + 2 * ML_HEADS


def _rms(x, g):
    return x * lax.rsqrt(jnp.mean(x * x, axis=-1, keepdims=True) + EPS) * g


def _nt_dot(a, b):
    return lax.dot_general(a, b, (((1,), (1,)), ((), ())), preferred_element_type=F32)


def _tn_dot(a, b):
    return lax.dot_general(a, b, (((0,), (0,)), ((), ())), preferred_element_type=F32)


def _dot(a, b):
    return jnp.dot(a, b, preferred_element_type=F32)


def _params(*sem):
    return pltpu.CompilerParams(dimension_semantics=sem, vmem_limit_bytes=VMEM_LIMIT)


def _resident(shape, index_map):
    return pl.BlockSpec(shape, index_map, pipeline_mode=pl.Buffered(1))


def _ffn_body(x_ref, g_ref, wg_ref, wu_ref, wd_ref, fg_ref, o_ref, h_s, *, tf, final_norm):
    rows = x_ref.shape[0]
    x = x_ref[...]
    xn = _rms(x, g_ref[...]).astype(BF16)
    for j in range(D_FF // tf):
        cols = slice(j * tf, (j + 1) * tf)
        g = _dot(xn, wg_ref[:, cols])
        u = _dot(xn, wu_ref[:, cols])
        h_s[0:rows, cols] = (g * jax.nn.sigmoid(g) * u).astype(BF16)
    y = x + 0.5 * _dot(h_s[0:rows, :], wd_ref[...])
    if final_norm:
        y = _rms(y, fg_ref[...])
    o_ref[...] = y


def _ffn_kernel(xp_ref, xs_ref, g_ref, wg_ref, wu_ref, wd_ref, fg_ref, op_ref, os_ref, h_s, *, np_tiles, **kw):
    i = pl.program_id(0)

    @pl.when(i < np_tiles)
    def _():
        _ffn_body(xp_ref, g_ref, wg_ref, wu_ref, wd_ref, fg_ref, op_ref, h_s, **kw)

    @pl.when(i == np_tiles)
    def _():
        _ffn_body(xs_ref, g_ref, wg_ref, wu_ref, wd_ref, fg_ref, os_ref, h_s, **kw)


def _ffn(xp, xs, g, wg, wu, wd, fg, l, *, tm, tf, final_norm):
    np_tiles = xp.shape[0] // tm
    ms = xs.shape[0]
    ptile = lambda i: (jnp.minimum(i, np_tiles - 1), 0)
    whole = lambda i: (0, 0)
    lay = lambda i: (l, 0, 0)
    return pl.pallas_call(
        functools.partial(_ffn_kernel, np_tiles=np_tiles, tf=tf, final_norm=final_norm),
        out_shape=(jax.ShapeDtypeStruct(xp.shape, F32), jax.ShapeDtypeStruct(xs.shape, F32)),
        grid=(np_tiles + 1,),
        in_specs=[
            pl.BlockSpec((tm, D_MODEL), ptile),
            _resident((ms, D_MODEL), whole),
            _resident((None, 1, D_MODEL), lay),
            _resident((None, D_MODEL, D_FF), lay),
            _resident((None, D_MODEL, D_FF), lay),
            _resident((None, D_FF, D_MODEL), lay),
            _resident((1, D_MODEL), whole),
        ],
        out_specs=(pl.BlockSpec((tm, D_MODEL), ptile), pl.BlockSpec((ms, D_MODEL), whole)),
        scratch_shapes=[pltpu.VMEM((max(tm, ms), D_FF), BF16)],
        compiler_params=_params("arbitrary"),
        name="ffn",
    )(xp, xs, g, wg, wu, wd, fg)


def _log_sigmoid(x):
    return jnp.minimum(x, 0.0) - jnp.log1p(jnp.exp(-jnp.abs(x)))


def _softplus(x):
    return jnp.maximum(x, 0.0) + jnp.log1p(jnp.exp(-jnp.abs(x)))


def _lane_pick(x, lane_ids, idx):
    return jnp.sum(jnp.where(lane_ids == idx, x, 0.0), axis=1, keepdims=True)


def _mlstm_blocks(z_ref, gbias_ref, onorm_ref, c_ref, n_ref, m_ref, hm_s, rows, cl, base=0):
    lane1 = lax.broadcasted_iota(jnp.int32, (1, LANES), 1)
    m_vec = m_ref[0]
    m_heads = [_lane_pick(m_vec, lane1, h) for h in range(ML_HEADS)]
    for c in range(rows // cl):
        m_heads = _mlstm_chunk(z_ref, gbias_ref, onorm_ref, c_ref, n_ref, m_heads, hm_s, base + c * cl, cl)
    for h in range(ML_HEADS):
        m_vec = jnp.where(lane1 == h, m_heads[h], m_vec)
    m_ref[0] = m_vec


def _mlstm_chunk(z_ref, gbias_ref, onorm_ref, c_ref, n_ref, m_heads, hm_s, r0, cl):
    lane = lax.broadcasted_iota(jnp.int32, (cl, LANES), 1)
    row = lax.broadcasted_iota(jnp.int32, (cl, LANES), 0)
    tril = (lax.broadcasted_iota(jnp.int32, (cl, cl), 1)
            <= lax.broadcasted_iota(jnp.int32, (cl, cl), 0))

    gates = z_ref[r0:r0 + cl, Z_G:Z_G + LANES] + gbias_ref[...]
    bsum = _log_sigmoid(gates)
    d = 1
    while d < cl:
        bsum = bsum + jnp.where(row >= d, pltpu.roll(bsum, d, axis=0), 0.0)
        d *= 2
    mixed = jnp.where(lane < ML_HEADS, gates, bsum)
    if cl < LANES:
        mixed = jnp.concatenate([mixed, jnp.zeros((LANES - cl, LANES), F32)], axis=0)
    mixed_t = mixed.T

    m_next = []
    for h in range(ML_HEADS):
        b_col = _lane_pick(bsum, lane, ML_HEADS + h)
        ig_col = _lane_pick(gates, lane, h)
        b_row = mixed_t[ML_HEADS + h:ML_HEADS + h + 1, 0:cl]
        ig_row = mixed_t[h:h + 1, 0:cl]
        m_prev = m_heads[h]

        dmat = jnp.where(tril, b_col - b_row + ig_row, -jnp.inf)
        inter = b_col + m_prev
        m_t = jnp.maximum(inter, jnp.max(dmat, axis=1, keepdims=True))

        qf = z_ref[r0:r0 + cl, Z_Q + h * ML_DK:Z_Q + (h + 1) * ML_DK] * (ML_DK ** -0.5)
        kf = z_ref[r0:r0 + cl, Z_K + h * ML_DK:Z_K + (h + 1) * ML_DK]
        vf = z_ref[r0:r0 + cl, Z_V + h * ML_DV:Z_V + (h + 1) * ML_DV]
        q = qf.astype(BF16)
        k = kf.astype(BF16)

        s = _nt_dot(q, k) * jnp.exp(dmat - m_t)
        w_inter = jnp.exp(inter - m_t)
        c_old = c_ref[0, h]
        n_old = n_ref[0, h:h + 1, :]
        num = w_inter * _nt_dot(q, c_old.astype(BF16)) + _dot(s.astype(BF16), vf.astype(BF16))
        den = (w_inter * jnp.sum(qf * n_old, axis=1, keepdims=True)
               + jnp.sum(s, axis=1, keepdims=True))
        hh = num / jnp.maximum(jnp.abs(den), jnp.exp(-m_t))

        hh = hh * lax.rsqrt(jnp.mean(hh * hh, axis=1, keepdims=True) + EPS) * onorm_ref[h:h + 1, :]
        o_gate = jax.nn.sigmoid(z_ref[r0:r0 + cl, Z_O + h * ML_DV:Z_O + (h + 1) * ML_DV])
        hm_s[r0:r0 + cl, h * ML_DV:(h + 1) * ML_DV] = (o_gate * hh).astype(BF16)

        m_new = m_t[cl - 1:cl, :]
        b_last = b_col[cl - 1:cl, :]
        wgt = jnp.exp(b_last - b_col + ig_col - m_new)
        decay = jnp.exp(b_last + m_prev - m_new)
        c_ref[0, h] = decay * c_old + _tn_dot((wgt * vf).astype(BF16), k)
        n_ref[0, h:h + 1, :] = decay * n_old + jnp.sum(wgt * kf, axis=0, keepdims=True)
        m_next.append(m_new)
    return m_next


def _lru_gates(conv, wai_ref, ba_ref, bi_ref, lam_ref):
    conv_b = conv.astype(BF16)
    pre = [_dot(conv_b[:, n * LRU_BW:(n + 1) * LRU_BW], wai_ref[n])
           for n in range(LRU_BLOCKS)]
    r_gate = jax.nn.sigmoid(jnp.concatenate([p[:, :LRU_BW] for p in pre], axis=1) + ba_ref[...])
    i_gate = jax.nn.sigmoid(jnp.concatenate([p[:, LRU_BW:] for p in pre], axis=1) + bi_ref[...])
    log_a = r_gate * (-LRU_C * _softplus(-lam_ref[...]))
    a = jnp.exp(log_a)
    w = -jnp.tanh(log_a) * (a * a + 1.0)
    xin = jnp.where(w > 0.0, w * lax.rsqrt(w), 0.0) * (i_gate * conv)
    return a, xin


def _pmixer_project(x_ref, gmix_ref, wnat_ref, wxy_ref, z_dst, zxy_dst, pbuf, *, tt):
    seg = tt // SUBLANES
    pitch = seg + SUBLANES
    xn = _rms(x_ref[...], gmix_ref[...])
    z_dst[...] = _dot(xn.astype(BF16), wnat_ref[...])
    for k in range(LRU_WIDTH // LANES):
        for s in range(SUBLANES):
            pbuf[k, pl.ds(s * pitch, seg), :] = xn[s * seg:(s + 1) * seg, k * LANES:(k + 1) * LANES]
    xnp = jnp.stack(
        [jnp.concatenate([pbuf[k, pl.ds(j, SUBLANES, stride=pitch), :]
                          for k in range(LRU_WIDTH // LANES)], axis=1) for j in range(seg)], axis=0)
    zxy_dst[...] = _dot(xnp.reshape(tt, D_MODEL).astype(BF16), wxy_ref[...])


def _pmixer_mix(x_ref, z_s, zxy_s, gbias_ref, onorm_ref, convw_ref, convb_ref, wai_ref, ba_ref, bi_ref,
                lam_ref, wout_ref, y_ref, c_ref, n_ref, m_ref, h_ref, cv_ref, obuf, hm_s, *, cl, tt):
    seg = tt // SUBLANES
    pitch = seg + SUBLANES

    u3 = zxy_s[:, 0:LRU_WIDTH].reshape(seg, SUBLANES, LRU_WIDTH)
    tail = cv_ref[0]
    sub = lax.broadcasted_iota(jnp.int32, (SUBLANES, LRU_WIDTH), 0)
    wrap = []
    for i in range(CONV_W - 1):
        prev = pltpu.roll(u3[seg - (CONV_W - 1) + i], 1, axis=0)
        fill = tail[SUBLANES - (CONV_W - 1) + i:SUBLANES - (CONV_W - 1) + i + 1, :]
        wrap.append(jnp.where(sub == 0, fill, prev))
        cv_ref[0, SUBLANES - (CONV_W - 1) + i:SUBLANES - (CONV_W - 1) + i + 1, :] = (
            u3[seg - (CONV_W - 1) + i][SUBLANES - 1:SUBLANES, :])
    ext = jnp.concatenate([jnp.stack(wrap, axis=0), u3], axis=0)
    conv3 = convb_ref[...] + ext[CONV_W - 1:] * convw_ref[CONV_W - 1:CONV_W, :]
    for j in range(1, CONV_W):
        conv3 = conv3 + ext[CONV_W - 1 - j:CONV_W - 1 - j + seg] * convw_ref[CONV_W - 1 - j:CONV_W - j, :]

    a, xin = _lru_gates(conv3.reshape(tt, LRU_WIDTH), wai_ref, ba_ref, bi_ref, lam_ref)

    a3 = a.reshape(seg, SUBLANES, LRU_WIDTH)
    x3 = xin.reshape(seg, SUBLANES, LRU_WIDTH)
    hs = [x3[0]]
    ps = [a3[0]]
    for j in range(1, seg):
        hs.append(a3[j] * hs[-1] + x3[j])
        ps.append(a3[j] * ps[-1])
    carry = h_ref[0]
    cin = []
    for s in range(SUBLANES):
        cin.append(carry)
        carry = hs[-1][s:s + 1, :] + ps[-1][s:s + 1, :] * carry
    h_ref[0] = carry
    cin = jnp.concatenate(cin, axis=0)
    h3 = jnp.stack([hs[j] + ps[j] * cin for j in range(seg)], axis=0)

    hl = (jax.nn.gelu(zxy_s[:, LRU_WIDTH:]) * h3.reshape(tt, LRU_WIDTH)).astype(BF16)
    ol3 = _dot(hl, wout_ref[ML_WIDTH:ML_WIDTH + LRU_WIDTH, :]).reshape(seg, SUBLANES, D_MODEL)
    for j in range(seg):
        for k in range(D_MODEL // LANES):
            obuf[k, pl.ds(j, SUBLANES, stride=pitch), :] = ol3[j][:, k * LANES:(k + 1) * LANES]
    out_lru = jnp.concatenate(
        [jnp.concatenate([obuf[k, pl.ds(s * pitch, seg), :] for s in range(SUBLANES)], axis=0)
         for k in range(D_MODEL // LANES)], axis=1)

    _mlstm_blocks(z_s, gbias_ref, onorm_ref, c_ref, n_ref, m_ref, hm_s, tt, cl)

    y_ref[...] = x_ref[...] + _dot(hm_s[...], wout_ref[0:ML_WIDTH, :]) + out_lru


def _pmixer_kernel(x_ref, gmix_ref, wnat_ref, wxy_ref, gbias_ref, onorm_ref, convw_ref,
                   convb_ref, wai_ref, ba_ref, bi_ref, lam_ref, wout_ref,
                   y_ref, c_ref, n_ref, m_ref, h_ref, cv_ref,
                   z_s, zxy_s, pbuf, obuf, hm_s, *, cl, tt):
    @pl.when(pl.program_id(1) == 0)
    def _():
        c_ref[...] = jnp.zeros_like(c_ref)
        n_ref[...] = jnp.zeros_like(n_ref)
        m_ref[...] = jnp.zeros_like(m_ref)
        h_ref[...] = jnp.zeros_like(h_ref)
        cv_ref[...] = jnp.zeros_like(cv_ref)

    _pmixer_project(x_ref, gmix_ref, wnat_ref, wxy_ref, z_s, zxy_s, pbuf, tt=tt)
    _pmixer_mix(x_ref, z_s, zxy_s, gbias_ref, onorm_ref, convw_ref, convb_ref, wai_ref, ba_ref,
                bi_ref, lam_ref, wout_ref, y_ref, c_ref, n_ref, m_ref, h_ref, cv_ref, obuf, hm_s,
                cl=cl, tt=tt)


def _pmixer(x, p, l, *, nb, t, tt, cl):
    nt = t // tt
    pitch = tt // SUBLANES + SUBLANES
    tok = lambda b, i: (b * nt + i, 0)
    st4 = lambda b, i: (b, 0, 0, 0)
    st3 = lambda b, i: (b, 0, 0)
    lay3 = lambda b, i: (l, 0, 0)
    lay4 = lambda b, i: (l, 0, 0, 0)
    return pl.pallas_call(
        functools.partial(_pmixer_kernel, cl=cl, tt=tt),
        out_shape=(
            jax.ShapeDtypeStruct((nb * t, D_MODEL), F32),
            jax.ShapeDtypeStruct((nb, ML_HEADS, ML_DV, ML_DK), F32),
            jax.ShapeDtypeStruct((nb, ML_HEADS, ML_DK), F32),
            jax.ShapeDtypeStruct((nb, 1, LANES), F32),
            jax.ShapeDtypeStruct((nb, 1, LRU_WIDTH), F32),
            jax.ShapeDtypeStruct((nb, SUBLANES, LRU_WIDTH), F32),
        ),
        grid=(nb, nt),
        in_specs=[
            pl.BlockSpec((tt, D_MODEL), tok),
            _resident((None, 1, D_MODEL), lay3),
            _resident((None, D_MODEL, Z_NAT), lay3),
            _resident((None, D_MODEL, Z_XY), lay3),
            _resident((None, 1, LANES), lay3),
            _resident((None, ML_HEADS, ML_DV), lay3),
            _resident((None, CONV_W, LRU_WIDTH), lay3),
            _resident((None, 1, LRU_WIDTH), lay3),
            _resident((None, LRU_BLOCKS, LRU_BW, 2 * LRU_BW), lay4),
            _resident((None, 1, LRU_WIDTH), lay3),
            _resident((None, 1, LRU_WIDTH), lay3),
            _resident((None, 1, LRU_WIDTH), lay3),
            _resident((None, ML_WIDTH + LRU_WIDTH, D_MODEL), lay3),
        ],
        out_specs=(
            pl.BlockSpec((tt, D_MODEL), tok),
            pl.BlockSpec((1, ML_HEADS, ML_DV, ML_DK), st4),
            pl.BlockSpec((1, ML_HEADS, ML_DK), st3),
            pl.BlockSpec((1, 1, LANES), st3),
            pl.BlockSpec((1, 1, LRU_WIDTH), st3),
            pl.BlockSpec((1, SUBLANES, LRU_WIDTH), st3),
        ),
        scratch_shapes=[pltpu.VMEM((tt, Z_NAT), F32), pltpu.VMEM((tt, Z_XY), F32),
                        pltpu.VMEM((LRU_WIDTH // LANES, SUBLANES * pitch, LANES), F32),
                        pltpu.VMEM((D_MODEL // LANES, SUBLANES * pitch, LANES), F32),
                        pltpu.VMEM((tt, ML_WIDTH), BF16)],
        compiler_params=_params("parallel", "arbitrary"),
        name="pmixer",
    )(x, p["mix_norm"], p["w_in"], p["w_xy"], p["gbias"], p["ml_out_norm"], p["lru_conv_w"],
      p["lru_conv_b"], p["w_ai"], p["lru_b_a"], p["lru_b_i"], p["lru_lambda"], p["w_out"])


def _zproj_kernel(x_ref, g_ref, wnat_ref, wxy_ref, znat_ref, zxy_ref):
    xn = _rms(x_ref[...], g_ref[...]).astype(BF16)
    znat_ref[...] = _dot(xn, wnat_ref[...])
    zxy_ref[...] = _dot(xn, wxy_ref[...])


def _zproj(x, p, l):
    m = x.shape[0]
    lay3 = lambda i: (l, 0, 0)
    return pl.pallas_call(
        _zproj_kernel,
        out_shape=(jax.ShapeDtypeStruct((m, Z_NAT), F32), jax.ShapeDtypeStruct((m, Z_XY), F32)),
        grid=(1,),
        in_specs=[
            pl.BlockSpec((m, D_MODEL), lambda i: (0, 0)),
            pl.BlockSpec((None, 1, D_MODEL), lay3),
            pl.BlockSpec((None, D_MODEL, Z_NAT), lay3),
            pl.BlockSpec((None, D_MODEL, Z_XY), lay3),
        ],
        out_specs=(pl.BlockSpec((m, Z_NAT), lambda i: (0, 0)), pl.BlockSpec((m, Z_XY), lambda i: (0, 0))),
        compiler_params=_params("arbitrary"),
        name="zproj",
    )(x, p["mix_norm"], p["w_in"], p["w_xy"])


def _smixer_kernel(znat_ref, zxy_ref, x_ref, c0_ref, n0_ref, m0_ref, h0_ref, cv0_ref,
                   gbias_ref, onorm_ref, convw_ref, convb_ref, wai_ref, ba_ref, bi_ref, lam_ref,
                   wout_ref,
                   y_ref, c_ref, n_ref, m_ref, h_ref, cv_ref,
                   ubuf_s, hm_s, *, ns, t):
    c_ref[...] = c0_ref[...]
    n_ref[...] = n0_ref[...]
    m_ref[...] = m0_ref[...]
    pad = t + SUBLANES

    convs = []
    for s in range(ns):
        one = pl.ds(s, 1)
        _mlstm_blocks(znat_ref, gbias_ref, onorm_ref, c_ref.at[one], n_ref.at[one], m_ref.at[one], hm_s,
                      t, t, base=s * t)
        u = zxy_ref[s * t:(s + 1) * t, 0:LRU_WIDTH]
        ubuf_s[s * pad:s * pad + SUBLANES, :] = cv0_ref[s]
        ubuf_s[s * pad + SUBLANES:(s + 1) * pad, :] = u
        cv_ref[s] = ubuf_s[s * pad + t:(s + 1) * pad, :]
        conv = convb_ref[...] + u * convw_ref[CONV_W - 1:CONV_W, :]
        for j in range(1, CONV_W):
            conv = conv + (ubuf_s[pl.ds(s * pad + SUBLANES - j, t), :]
                           * convw_ref[CONV_W - 1 - j:CONV_W - j, :])
        convs.append(conv)
    a, xin = _lru_gates(jnp.concatenate(convs, axis=0), wai_ref, ba_ref, bi_ref, lam_ref)

    rmod = lax.broadcasted_iota(jnp.int32, (ns * t, LRU_WIDTH), 0) & (SUBLANES - 1)
    d = 1
    while d < SUBLANES:
        keep = rmod >= d
        xin = xin + a * jnp.where(keep, pltpu.roll(xin, d, axis=0), 0.0)
        a = a * jnp.where(keep, pltpu.roll(a, d, axis=0), 1.0)
        d *= 2
    groups = []
    for s in range(ns):
        carry = h0_ref[s]
        for gidx in range(t // SUBLANES):
            lo = s * t + gidx * SUBLANES
            hb = xin[lo:lo + SUBLANES, :] + a[lo:lo + SUBLANES, :] * carry
            carry = hb[SUBLANES - 1:SUBLANES, :]
            groups.append(hb)
        h_ref[s] = carry
    h_lru = jnp.concatenate(groups, axis=0)

    hl = (jax.nn.gelu(zxy_ref[:, LRU_WIDTH:]) * h_lru).astype(BF16)
    out = (_dot(hm_s[...], wout_ref[0:ML_WIDTH, :])
           + _dot(hl, wout_ref[ML_WIDTH:ML_WIDTH + LRU_WIDTH, :]))
    y_ref[...] = x_ref[...] + out


def _smixer(znat, zxy, x, st, p, l, *, nb, t, ns):
    tok = lambda b: (b, 0)
    lay3 = lambda b: (l, 0, 0)
    lay4 = lambda b: (l, 0, 0, 0)
    st3 = lambda b: (b, 0, 0)
    lst = lambda b: (l, b, 0, 0)
    c0, n0, m0, h0, cv0 = st
    return pl.pallas_call(
        functools.partial(_smixer_kernel, ns=ns, t=t),
        out_shape=(
            jax.ShapeDtypeStruct((nb * t, D_MODEL), F32),
            jax.ShapeDtypeStruct((nb, ML_HEADS, ML_DV, ML_DK), F32),
            jax.ShapeDtypeStruct((nb, ML_HEADS, ML_DK), F32),
            jax.ShapeDtypeStruct((nb, 1, LANES), F32),
            jax.ShapeDtypeStruct((nb, 1, LRU_WIDTH), F32),
            jax.ShapeDtypeStruct((nb, SUBLANES, LRU_WIDTH), F32),
        ),
        grid=(nb // ns,),
        in_specs=[
            pl.BlockSpec((ns * t, Z_NAT), tok),
            pl.BlockSpec((ns * t, Z_XY), tok),
            pl.BlockSpec((ns * t, D_MODEL), tok),
            pl.BlockSpec((None, ns, ML_HEADS, ML_DV, ML_DK), lambda b: (l, b, 0, 0, 0)),
            pl.BlockSpec((None, ns, ML_HEADS, ML_DK), lst),
            pl.BlockSpec((None, ns, 1, LANES), lst),
            pl.BlockSpec((None, ns, 1, LRU_WIDTH), lst),
            pl.BlockSpec((None, ns, SUBLANES, LRU_WIDTH), lst),
            pl.BlockSpec((None, 1, LANES), lay3),
            pl.BlockSpec((None, ML_HEADS, ML_DV), lay3),
            pl.BlockSpec((None, CONV_W, LRU_WIDTH), lay3),
            pl.BlockSpec((None, 1, LRU_WIDTH), lay3),
            pl.BlockSpec((None, LRU_BLOCKS, LRU_BW, 2 * LRU_BW), lay4),
            pl.BlockSpec((None, 1, LRU_WIDTH), lay3),
            pl.BlockSpec((None, 1, LRU_WIDTH), lay3),
            pl.BlockSpec((None, 1, LRU_WIDTH), lay3),
            pl.BlockSpec((None, ML_WIDTH + LRU_WIDTH, D_MODEL), lay3),
        ],
        out_specs=(
            pl.BlockSpec((ns * t, D_MODEL), tok),
            pl.BlockSpec((ns, ML_HEADS, ML_DV, ML_DK), lambda b: (b, 0, 0, 0)),
            pl.BlockSpec((ns, ML_HEADS, ML_DK), st3),
            pl.BlockSpec((ns, 1, LANES), st3),
            pl.BlockSpec((ns, 1, LRU_WIDTH), st3),
            pl.BlockSpec((ns, SUBLANES, LRU_WIDTH), st3),
        ),
        scratch_shapes=[pltpu.VMEM((ns * (t + SUBLANES), LRU_WIDTH), F32),
                        pltpu.VMEM((ns * t, ML_WIDTH), BF16)],
        compiler_params=_params("parallel"),
        name="smixer",
    )(znat, zxy, x, c0, n0, m0, h0, cv0, p["gbias"], p["ml_out_norm"], p["lru_conv_w"],
      p["lru_conv_b"], p["w_ai"], p["lru_b_a"], p["lru_b_i"], p["lru_lambda"], p["w_out"])


def _xattn_kernel(x_ref, g_ref, wq_ref, mk_ref, mv_ref, wo_ref, y_ref, o_s, *, ns, rows):
    x = x_ref[...]
    xn = _rms(x, g_ref[...]).astype(BF16)
    q = _dot(xn, wq_ref[...])
    for s in range(ns):
        r = slice(s * rows, (s + 1) * rows)
        for h in range(XA_HEADS):
            cols = slice(h * XA_DH, (h + 1) * XA_DH)
            sc = _nt_dot(q[r, cols].astype(BF16), mk_ref[s, :, cols].astype(BF16)) * (XA_DH ** -0.5)
            p = jnp.exp(sc - jnp.max(sc, axis=1, keepdims=True))
            p = p / jnp.sum(p, axis=1, keepdims=True)
            o_s[r, cols] = _dot(p.astype(BF16), mv_ref[s, :, cols].astype(BF16)).astype(BF16)
    y_ref[...] = x + _dot(o_s[...], wo_ref[...])


def _xattn(x, g, wq, mk, mv, wo, l, *, nb, t, tt, ns=1):
    nt = t // tt
    assert ns == 1 or nt == 1
    tok = lambda b, i: (b * nt + i, 0)
    lay3 = lambda b, i: (l, 0, 0)
    mem = lambda b, i: (l, b, 0, 0)
    return pl.pallas_call(
        functools.partial(_xattn_kernel, ns=ns, rows=tt),
        out_shape=jax.ShapeDtypeStruct((nb * t, D_MODEL), F32),
        grid=(nb // ns, nt),
        in_specs=[
            pl.BlockSpec((ns * tt, D_MODEL), tok),
            pl.BlockSpec((None, 1, D_MODEL), lay3),
            pl.BlockSpec((None, D_MODEL, D_MODEL), lay3),
            pl.BlockSpec((None, ns, MEM_LEN, D_MODEL), mem),
            pl.BlockSpec((None, ns, MEM_LEN, D_MODEL), mem),
            pl.BlockSpec((None, D_MODEL, D_MODEL), lay3),
        ],
        out_specs=pl.BlockSpec((ns * tt, D_MODEL), tok),
        scratch_shapes=[pltpu.VMEM((ns * tt, D_MODEL), BF16)],
        compiler_params=_params("parallel", "arbitrary"),
        name="xattn",
    )(x, g, wq, mk, mv, wo)


def _memkv_kernel(mem_ref, g_ref, wk_ref, wv_ref, k_ref, v_ref):
    mn = _rms(mem_ref[...], g_ref[...]).astype(BF16)
    k_ref[...] = _dot(mn, wk_ref[...])
    v_ref[...] = _dot(mn, wv_ref[...])


def _memkv(mem, g, wk, wv):
    nb = mem.shape[0]
    wspec = pl.BlockSpec((None, D_MODEL, D_MODEL), lambda l, b: (l, 0, 0))
    ospec = pl.BlockSpec((None, None, MEM_LEN, D_MODEL), lambda l, b: (l, b, 0, 0))
    out = jax.ShapeDtypeStruct((DEPTH, nb, MEM_LEN, D_MODEL), F32)
    return pl.pallas_call(
        _memkv_kernel,
        out_shape=(out, out),
        grid=(DEPTH, nb),
        in_specs=[pl.BlockSpec((None, MEM_LEN, D_MODEL), lambda l, b: (b, 0, 0)),
                  pl.BlockSpec((None, 1, D_MODEL), lambda l, b: (l, 0, 0)), wspec, wspec],
        out_specs=(ospec, ospec),
        compiler_params=_params("parallel", "parallel"),
        name="memkv",
    )(mem, g, wk, wv)


def kernel(x_prompt, x_sample, mem_prompt, state_mlstm_C, state_mlstm_n, state_mlstm_m, state_lru_h,
           state_lru_conv, cache_mem_k, cache_mem_v, ffn1_norm, ffn1_w_gate, ffn1_w_up, ffn1_w_down,
           mix_norm, w_in, ml_b_i, ml_b_f, ml_out_norm, lru_conv_w, lru_conv_b, lru_w_a, lru_b_a,
           lru_w_i, lru_b_i, lru_lambda, w_out, xattn_norm, mem_norm, xattn_w_q, xattn_w_k, xattn_w_v,
           xattn_w_o, ffn2_norm, ffn2_w_gate, ffn2_w_up, ffn2_w_down, final_norm):
    bp, tp, _ = x_prompt.shape
    bs, ts, _ = x_sample.shape
    xp = x_prompt.reshape(bp * tp, D_MODEL)
    xs = x_sample.reshape(bs * ts, D_MODEL)

    bf = lambda w: w.astype(BF16)
    row = lambda v: v.astype(F32).reshape(DEPTH, 1, -1)
    gate_pad = ((0, 0), (0, 0), (0, LANES - 2 * ML_HEADS))
    w_in_b = bf(w_in)
    p = {
        "mix_norm": row(mix_norm),
        "w_in": w_in_b,
        "w_xy": w_in_b[:, :, W_X:],
        "gbias": jnp.pad(jnp.concatenate([ml_b_i, ml_b_f], axis=1).astype(F32)[:, None, :], gate_pad),
        "ml_out_norm": ml_out_norm.astype(F32),
        "lru_conv_w": lru_conv_w, "lru_conv_b": row(lru_conv_b),
        "w_ai": bf(jnp.concatenate([lru_w_a, lru_w_i], axis=-1)),
        "lru_b_a": row(lru_b_a), "lru_b_i": row(lru_b_i), "lru_lambda": row(lru_lambda),
        "w_out": bf(w_out),
    }
    ffn1 = (row(ffn1_norm), bf(ffn1_w_gate), bf(ffn1_w_up), bf(ffn1_w_down))
    ffn2 = (row(ffn2_norm), bf(ffn2_w_gate), bf(ffn2_w_up), bf(ffn2_w_down))
    xa_g, xa_q, xa_o = row(xattn_norm), bf(xattn_w_q), bf(xattn_w_o)
    fin = final_norm.astype(F32).reshape(1, D_MODEL)

    pk, pv = _memkv(mem_prompt, row(mem_norm), bf(xattn_w_k), bf(xattn_w_v))
    sk = cache_mem_k.reshape(DEPTH, bs, MEM_LEN, D_MODEL)
    sv = cache_mem_v.reshape(DEPTH, bs, MEM_LEN, D_MODEL)
    s_state = (
        state_mlstm_C.astype(F32), state_mlstm_n.astype(F32),
        jnp.pad(state_mlstm_m.astype(F32)[:, :, None, :], ((0, 0), (0, 0), (0, 0), (0, LANES - ML_HEADS))),
        state_lru_h.astype(F32)[:, :, None, :],
        jnp.pad(state_lru_conv.astype(F32), ((0, 0), (0, 0), (SUBLANES - (CONV_W - 1), 0), (0, 0))),
    )

    p_out = [[] for _ in range(5)]
    s_out = [[] for _ in range(5)]
    for l in range(DEPTH):
        last = l == DEPTH - 1
        xp, xs = _ffn(xp, xs, *ffn1, fin, l, tm=512, tf=512, final_norm=False)
        xp, *st = _pmixer(xp, p, l, nb=bp, t=tp, tt=512, cl=ML_BLOCK)
        xp = _xattn(xp, xa_g, xa_q, pk, pv, xa_o, l, nb=bp, t=tp, tt=1024)
        for acc, v in zip(p_out, st):
            acc.append(v)
        znat, zxy = _zproj(xs, p, l)
        xs, *st = _smixer(znat, zxy, xs, s_state, p, l, nb=bs, t=ts, ns=4)
        xs = _xattn(xs, xa_g, xa_q, sk, sv, xa_o, l, nb=bs, t=ts, tt=ts, ns=4)
        for acc, v in zip(s_out, st):
            acc.append(v)
        xp, xs = _ffn(xp, xs, *ffn2, fin, l, tm=512, tf=512, final_norm=last)

    def states(acc):
        c, n, m, h, cv = (jnp.stack(a) for a in acc)
        return c, n, m[:, :, 0, :ML_HEADS], h[:, :, 0, :], cv[:, :, SUBLANES - (CONV_W - 1):, :]

    return (xp.reshape(bp, tp, D_MODEL), xs.reshape(bs, ts, D_MODEL),
            *states(p_out),
            pk.reshape(DEPTH, bp, MEM_LEN, XA_HEADS, XA_DH), pv.reshape(DEPTH, bp, MEM_LEN, XA_HEADS, XA_DH),
            *states(s_out))
```

```python
import functools

import jax
import jax.numpy as jnp
from jax import lax
from jax.experimental import pallas as pl
from jax.experimental.pallas import tpu as pltpu

F32 = jnp.float32
BF16 = jnp.bfloat16

D_MODEL = 1024
DEPTH = 2
ML_BLOCK = 128
ML_HEADS = 4
ML_DV = 256
ML_DK = 128
ML_WIDTH = ML_HEADS * ML_DV
LRU_WIDTH = 1024
LRU_BLOCKS = 8
LRU_BW = LRU_WIDTH // LRU_BLOCKS
CONV_W = 4
LRU_C = 8.0
MEM_LEN = 256
XA_HEADS = 4
XA_DH = D_MODEL // XA_HEADS
D_FF = 4 * D_MODEL
EPS = 1e-6

LANES = 128
SUBLANES = 8
VMEM_LIMIT = 56 * 1024 * 1024

Z_Q = 0
Z_K = Z_Q + ML_HEADS * ML_DK
Z_V = Z_K + ML_HEADS * ML_DK
Z_O = Z_V + ML_WIDTH
Z_G = Z_O + ML_WIDTH
Z_NAT = Z_G + LANES
Z_XY = 2 * LRU_WIDTH
W_X = Z_G + 2 * ML_HEADS


def _rms(x, g):
    return x * lax.rsqrt(jnp.mean(x * x, axis=-1, keepdims=True) + EPS) * g


def _nt_dot(a, b):
    return lax.dot_general(a, b, (((1,), (1,)), ((), ())), preferred_element_type=F32)


def _tn_dot(a, b):
    return lax.dot_general(a, b, (((0,), (0,)), ((), ())), preferred_element_type=F32)


def _dot(a, b):
    return jnp.dot(a, b, preferred_element_type=F32)


def _params(*sem):
    return pltpu.CompilerParams(dimension_semantics=sem, vmem_limit_bytes=VMEM_LIMIT)


def _resident(shape, index_map):
    return pl.BlockSpec(shape, index_map, pipeline_mode=pl.Buffered(1))


def _ffn_body(parts, g_ref, wg_ref, wu_ref, wd_ref, fg_ref, h_s, *, tf, final_norm):
    xs = [x_ref[...] for x_ref, _ in parts]
    rows = sum(x.shape[0] for x in xs)
    xn = jnp.concatenate([_rms(x, g_ref[...]).astype(BF16) for x in xs], axis=0)
    for j in range(D_FF // tf):
        cols = slice(j * tf, (j + 1) * tf)
        g = _dot(xn, wg_ref[:, cols])
        u = _dot(xn, wu_ref[:, cols])
        h_s[0:rows, cols] = (g * jax.nn.sigmoid(g) * u).astype(BF16)
    d = _dot(h_s[0:rows, :], wd_ref[...])
    lo = 0
    for x, (_, o_ref) in zip(xs, parts):
        y = x + 0.5 * d[lo:lo + x.shape[0], :]
        if final_norm:
            y = _rms(y, fg_ref[...])
        o_ref[...] = y
        lo += x.shape[0]


def _ffn_kernel(xp_ref, xs_ref, g_ref, wg_ref, wu_ref, wd_ref, fg_ref, op_ref, os_ref, h_s, **kw):
    last = pl.num_programs(0) - 1

    @pl.when(pl.program_id(0) < last)
    def _():
        _ffn_body([(xp_ref, op_ref)], g_ref, wg_ref, wu_ref, wd_ref, fg_ref, h_s, **kw)

    @pl.when(pl.program_id(0) == last)
    def _():
        _ffn_body([(xp_ref, op_ref), (xs_ref, os_ref)], g_ref, wg_ref, wu_ref, wd_ref, fg_ref, h_s, **kw)


def _ffn(xp, xs, g, wg, wu, wd, fg, l, *, tm, tf, final_norm):
    np_tiles = xp.shape[0] // tm
    ms = xs.shape[0]
    ptile = lambda i: (i, 0)
    whole = lambda i: (0, 0)
    lay = lambda i: (l, 0, 0)
    return pl.pallas_call(
        functools.partial(_ffn_kernel, tf=tf, final_norm=final_norm),
        out_shape=(jax.ShapeDtypeStruct(xp.shape, F32), jax.ShapeDtypeStruct(xs.shape, F32)),
        grid=(np_tiles,),
        in_specs=[
            pl.BlockSpec((tm, D_MODEL), ptile),
            _resident((ms, D_MODEL), whole),
            _resident((None, 1, D_MODEL), lay),
            _resident((None, D_MODEL, D_FF), lay),
            _resident((None, D_MODEL, D_FF), lay),
            _resident((None, D_FF, D_MODEL), lay),
            _resident((1, D_MODEL), whole),
        ],
        out_specs=(pl.BlockSpec((tm, D_MODEL), ptile), pl.BlockSpec((ms, D_MODEL), whole)),
        scratch_shapes=[pltpu.VMEM((tm + ms, D_FF), BF16)],
        compiler_params=_params("arbitrary"),
        name="ffn",
    )(xp, xs, g, wg, wu, wd, fg)


def _log_sigmoid(x):
    return jnp.minimum(x, 0.0) - jnp.log1p(jnp.exp(-jnp.abs(x)))


def _softplus(x):
    return jnp.maximum(x, 0.0) + jnp.log1p(jnp.exp(-jnp.abs(x)))


def _lane_pick(x, lane_ids, idx):
    return jnp.sum(jnp.where(lane_ids == idx, x, 0.0), axis=1, keepdims=True)


def _mlstm_blocks(z_ref, gbias_ref, onorm_ref, c_ref, n_ref, m_ref, hm_s, rows, cl, base=0):
    lane1 = lax.broadcasted_iota(jnp.int32, (1, LANES), 1)
    m_vec = m_ref[0]
    m_heads = [_lane_pick(m_vec, lane1, h) for h in range(ML_HEADS)]
    for c in range(rows // cl):
        m_heads = _mlstm_chunk(z_ref, gbias_ref, onorm_ref, c_ref, n_ref, m_heads, hm_s, base + c * cl, cl)
    for h in range(ML_HEADS):
        m_vec = jnp.where(lane1 == h, m_heads[h], m_vec)
    m_ref[0] = m_vec


def _mlstm_chunk(z_ref, gbias_ref, onorm_ref, c_ref, n_ref, m_heads, hm_s, r0, cl):
    lane = lax.broadcasted_iota(jnp.int32, (cl, LANES), 1)
    row = lax.broadcasted_iota(jnp.int32, (cl, LANES), 0)
    tril = (lax.broadcasted_iota(jnp.int32, (cl, cl), 1)
            <= lax.broadcasted_iota(jnp.int32, (cl, cl), 0))

    gates = z_ref[r0:r0 + cl, Z_G:Z_G + LANES] + gbias_ref[...]
    bsum = _log_sigmoid(gates)
    d = 1
    while d < cl:
        bsum = bsum + jnp.where(row >= d, pltpu.roll(bsum, d, axis=0), 0.0)
        d *= 2
    mixed = jnp.where(lane < ML_HEADS, gates, bsum)
    if cl < LANES:
        mixed = jnp.concatenate([mixed, jnp.zeros((LANES - cl, LANES), F32)], axis=0)
    mixed_t = mixed.T

    m_next = []
    for h in range(ML_HEADS):
        b_col = _lane_pick(bsum, lane, ML_HEADS + h)
        ig_col = _lane_pick(gates, lane, h)
        b_row = mixed_t[ML_HEADS + h:ML_HEADS + h + 1, 0:cl]
        ig_row = mixed_t[h:h + 1, 0:cl]
        m_prev = m_heads[h]

        dmat = jnp.where(tril, b_col - b_row + ig_row, -jnp.inf)
        inter = b_col + m_prev
        m_t = jnp.maximum(inter, jnp.max(dmat, axis=1, keepdims=True))

        qf = z_ref[r0:r0 + cl, Z_Q + h * ML_DK:Z_Q + (h + 1) * ML_DK] * (ML_DK ** -0.5)
        kf = z_ref[r0:r0 + cl, Z_K + h * ML_DK:Z_K + (h + 1) * ML_DK]
        vf = z_ref[r0:r0 + cl, Z_V + h * ML_DV:Z_V + (h + 1) * ML_DV]
        q = qf.astype(BF16)
        k = kf.astype(BF16)

        s = _nt_dot(q, k) * jnp.exp(dmat - m_t)
        w_inter = jnp.exp(inter - m_t)
        c_old = c_ref[0, h]
        n_old = n_ref[0, h:h + 1, :]
        num = w_inter * _nt_dot(q, c_old.astype(BF16)) + _dot(s.astype(BF16), vf.astype(BF16))
        den = (w_inter * jnp.sum(qf * n_old, axis=1, keepdims=True)
               + jnp.sum(s, axis=1, keepdims=True))
        hh = num / jnp.maximum(jnp.abs(den), jnp.exp(-m_t))

        hh = hh * lax.rsqrt(jnp.mean(hh * hh, axis=1, keepdims=True) + EPS) * onorm_ref[h:h + 1, :]
        o_gate = jax.nn.sigmoid(z_ref[r0:r0 + cl, Z_O + h * ML_DV:Z_O + (h + 1) * ML_DV])
        hm_s[r0:r0 + cl, h * ML_DV:(h + 1) * ML_DV] = (o_gate * hh).astype(BF16)

        m_new = m_t[cl - 1:cl, :]
        b_last = b_col[cl - 1:cl, :]
        wgt = jnp.exp(b_last - b_col + ig_col - m_new)
        decay = jnp.exp(b_last + m_prev - m_new)
        c_ref[0, h] = decay * c_old + _tn_dot((wgt * vf).astype(BF16), k)
        n_ref[0, h:h + 1, :] = decay * n_old + jnp.sum(wgt * kf, axis=0, keepdims=True)
        m_next.append(m_new)
    return m_next


def _lru_gates(conv, wai_ref, ba_ref, bi_ref, lam_ref):
    conv_b = conv.astype(BF16)
    pre = [_dot(conv_b[:, n * LRU_BW:(n + 1) * LRU_BW], wai_ref[n])
           for n in range(LRU_BLOCKS)]
    r_gate = jax.nn.sigmoid(jnp.concatenate([p[:, :LRU_BW] for p in pre], axis=1) + ba_ref[...])
    i_gate = jax.nn.sigmoid(jnp.concatenate([p[:, LRU_BW:] for p in pre], axis=1) + bi_ref[...])
    log_a = r_gate * (-LRU_C * _softplus(-lam_ref[...]))
    a = jnp.exp(log_a)
    w = -jnp.tanh(log_a) * (a * a + 1.0)
    xin = jnp.where(w > 0.0, w * lax.rsqrt(w), 0.0) * (i_gate * conv)
    return a, xin


def _pmixer_project(x_ref, gmix_ref, wnat_ref, wxy_ref, z_dst, zxy_dst, pbuf, *, tt):
    seg = tt // SUBLANES
    pitch = seg + SUBLANES
    xn = _rms(x_ref[...], gmix_ref[...])
    z_dst[...] = _dot(xn.astype(BF16), wnat_ref[...])
    for k in range(LRU_WIDTH // LANES):
        for s in range(SUBLANES):
            pbuf[k, pl.ds(s * pitch, seg), :] = xn[s * seg:(s + 1) * seg, k * LANES:(k + 1) * LANES]
    xnp = jnp.stack(
        [jnp.concatenate([pbuf[k, pl.ds(j, SUBLANES, stride=pitch), :]
                          for k in range(LRU_WIDTH // LANES)], axis=1) for j in range(seg)], axis=0)
    zxy_dst[...] = _dot(xnp.reshape(tt, D_MODEL).astype(BF16), wxy_ref[...])


def _pmixer_mix(x_ref, z_s, zxy_s, gbias_ref, onorm_ref, convw_ref, convb_ref, wai_ref, ba_ref, bi_ref,
                lam_ref, wout_ref, y_ref, c_ref, n_ref, m_ref, h_ref, cv_ref, obuf, hm_s, *, cl, tt):
    seg = tt // SUBLANES
    pitch = seg + SUBLANES

    u3 = zxy_s[:, 0:LRU_WIDTH].reshape(seg, SUBLANES, LRU_WIDTH)
    tail = cv_ref[0]
    sub = lax.broadcasted_iota(jnp.int32, (SUBLANES, LRU_WIDTH), 0)
    wrap = []
    for i in range(CONV_W - 1):
        prev = pltpu.roll(u3[seg - (CONV_W - 1) + i], 1, axis=0)
        fill = tail[SUBLANES - (CONV_W - 1) + i:SUBLANES - (CONV_W - 1) + i + 1, :]
        wrap.append(jnp.where(sub == 0, fill, prev))
        cv_ref[0, SUBLANES - (CONV_W - 1) + i:SUBLANES - (CONV_W - 1) + i + 1, :] = (
            u3[seg - (CONV_W - 1) + i][SUBLANES - 1:SUBLANES, :])
    ext = jnp.concatenate([jnp.stack(wrap, axis=0), u3], axis=0)
    conv3 = convb_ref[...] + ext[CONV_W - 1:] * convw_ref[CONV_W - 1:CONV_W, :]
    for j in range(1, CONV_W):
        conv3 = conv3 + ext[CONV_W - 1 - j:CONV_W - 1 - j + seg] * convw_ref[CONV_W - 1 - j:CONV_W - j, :]

    a, xin = _lru_gates(conv3.reshape(tt, LRU_WIDTH), wai_ref, ba_ref, bi_ref, lam_ref)

    a3 = a.reshape(seg, SUBLANES, LRU_WIDTH)
    x3 = xin.reshape(seg, SUBLANES, LRU_WIDTH)
    hs = [x3[0]]
    ps = [a3[0]]
    for j in range(1, seg):
        hs.append(a3[j] * hs[-1] + x3[j])
        ps.append(a3[j] * ps[-1])
    carry = h_ref[0]
    cin = []
    for s in range(SUBLANES):
        cin.append(carry)
        carry = hs[-1][s:s + 1, :] + ps[-1][s:s + 1, :] * carry
    h_ref[0] = carry
    cin = jnp.concatenate(cin, axis=0)
    h3 = jnp.stack([hs[j] + ps[j] * cin for j in range(seg)], axis=0)

    hl = (jax.nn.gelu(zxy_s[:, LRU_WIDTH:]) * h3.reshape(tt, LRU_WIDTH)).astype(BF16)
    ol3 = _dot(hl, wout_ref[ML_WIDTH:ML_WIDTH + LRU_WIDTH, :]).reshape(seg, SUBLANES, D_MODEL)
    for j in range(seg):
        for k in range(D_MODEL // LANES):
            obuf[k, pl.ds(j, SUBLANES, stride=pitch), :] = ol3[j][:, k * LANES:(k + 1) * LANES]
    out_lru = jnp.concatenate(
        [jnp.concatenate([obuf[k, pl.ds(s * pitch, seg), :] for s in range(SUBLANES)], axis=0)
         for k in range(D_MODEL // LANES)], axis=1)

    _mlstm_blocks(z_s, gbias_ref, onorm_ref, c_ref, n_ref, m_ref, hm_s, tt, cl)

    y_ref[...] = x_ref[...] + _dot(hm_s[...], wout_ref[0:ML_WIDTH, :]) + out_lru


def _pmixer_kernel(x_ref, gmix_ref, wnat_ref, wxy_ref, gbias_ref, onorm_ref, convw_ref,
                   convb_ref, wai_ref, ba_ref, bi_ref, lam_ref, wout_ref,
                   y_ref, c_ref, n_ref, m_ref, h_ref, cv_ref,
                   z_s, zxy_s, pbuf, obuf, hm_s, *, cl, tt):
    @pl.when(pl.program_id(1) == 0)
    def _():
        c_ref[...] = jnp.zeros_like(c_ref)
        n_ref[...] = jnp.zeros_like(n_ref)
        m_ref[...] = jnp.zeros_like(m_ref)
        h_ref[...] = jnp.zeros_like(h_ref)
        cv_ref[...] = jnp.zeros_like(cv_ref)

    _pmixer_project(x_ref, gmix_ref, wnat_ref, wxy_ref, z_s, zxy_s, pbuf, tt=tt)
    _pmixer_mix(x_ref, z_s, zxy_s, gbias_ref, onorm_ref, convw_ref, convb_ref, wai_ref, ba_ref,
                bi_ref, lam_ref, wout_ref, y_ref, c_ref, n_ref, m_ref, h_ref, cv_ref, obuf, hm_s,
                cl=cl, tt=tt)


def _pmixer(x, p, l, *, nb, t, tt, cl):
    nt = t // tt
    pitch = tt // SUBLANES + SUBLANES
    tok = lambda b, i: (b * nt + i, 0)
    st4 = lambda b, i: (b, 0, 0, 0)
    st3 = lambda b, i: (b, 0, 0)
    lay3 = lambda b, i: (l, 0, 0)
    lay4 = lambda b, i: (l, 0, 0, 0)
    return pl.pallas_call(
        functools.partial(_pmixer_kernel, cl=cl, tt=tt),
        out_shape=(
            jax.ShapeDtypeStruct((nb * t, D_MODEL), F32),
            jax.ShapeDtypeStruct((nb, ML_HEADS, ML_DV, ML_DK), F32),
            jax.ShapeDtypeStruct((nb, ML_HEADS, ML_DK), F32),
            jax.ShapeDtypeStruct((nb, 1, LANES), F32),
            jax.ShapeDtypeStruct((nb, 1, LRU_WIDTH), F32),
            jax.ShapeDtypeStruct((nb, SUBLANES, LRU_WIDTH), F32),
        ),
        grid=(nb, nt),
        in_specs=[
            pl.BlockSpec((tt, D_MODEL), tok),
            _resident((None, 1, D_MODEL), lay3),
            _resident((None, D_MODEL, Z_NAT), lay3),
            _resident((None, D_MODEL, Z_XY), lay3),
            _resident((None, 1, LANES), lay3),
            _resident((None, ML_HEADS, ML_DV), lay3),
            _resident((None, CONV_W, LRU_WIDTH), lay3),
            _resident((None, 1, LRU_WIDTH), lay3),
            _resident((None, LRU_BLOCKS, LRU_BW, 2 * LRU_BW), lay4),
            _resident((None, 1, LRU_WIDTH), lay3),
            _resident((None, 1, LRU_WIDTH), lay3),
            _resident((None, 1, LRU_WIDTH), lay3),
            _resident((None, ML_WIDTH + LRU_WIDTH, D_MODEL), lay3),
        ],
        out_specs=(
            pl.BlockSpec((tt, D_MODEL), tok),
            pl.BlockSpec((1, ML_HEADS, ML_DV, ML_DK), st4),
            pl.BlockSpec((1, ML_HEADS, ML_DK), st3),
            pl.BlockSpec((1, 1, LANES), st3),
            pl.BlockSpec((1, 1, LRU_WIDTH), st3),
            pl.BlockSpec((1, SUBLANES, LRU_WIDTH), st3),
        ),
        scratch_shapes=[pltpu.VMEM((tt, Z_NAT), F32), pltpu.VMEM((tt, Z_XY), F32),
                        pltpu.VMEM((LRU_WIDTH // LANES, SUBLANES * pitch, LANES), F32),
                        pltpu.VMEM((D_MODEL // LANES, SUBLANES * pitch, LANES), F32),
                        pltpu.VMEM((tt, ML_WIDTH), BF16)],
        compiler_params=_params("parallel", "arbitrary"),
        name="pmixer",
    )(x, p["mix_norm"], p["w_in"], p["w_xy"], p["gbias"], p["ml_out_norm"], p["lru_conv_w"],
      p["lru_conv_b"], p["w_ai"], p["lru_b_a"], p["lru_b_i"], p["lru_lambda"], p["w_out"])


def _zproj_kernel(x_ref, g_ref, wnat_ref, wxy_ref, znat_ref, zxy_ref):
    xn = _rms(x_ref[...], g_ref[...]).astype(BF16)
    znat_ref[...] = _dot(xn, wnat_ref[...])
    zxy_ref[...] = _dot(xn, wxy_ref[...])


def _zproj(x, p, l):
    m = x.shape[0]
    lay3 = lambda i: (l, 0, 0)
    return pl.pallas_call(
        _zproj_kernel,
        out_shape=(jax.ShapeDtypeStruct((m, Z_NAT), F32), jax.ShapeDtypeStruct((m, Z_XY), F32)),
        grid=(1,),
        in_specs=[
            pl.BlockSpec((m, D_MODEL), lambda i: (0, 0)),
            pl.BlockSpec((None, 1, D_MODEL), lay3),
            pl.BlockSpec((None, D_MODEL, Z_NAT), lay3),
            pl.BlockSpec((None, D_MODEL, Z_XY), lay3),
        ],
        out_specs=(pl.BlockSpec((m, Z_NAT), lambda i: (0, 0)), pl.BlockSpec((m, Z_XY), lambda i: (0, 0))),
        compiler_params=_params("arbitrary"),
        name="zproj",
    )(x, p["mix_norm"], p["w_in"], p["w_xy"])


def _smixer_kernel(znat_ref, zxy_ref, x_ref, c0_ref, n0_ref, m0_ref, h0_ref, cv0_ref,
                   gbias_ref, onorm_ref, convw_ref, convb_ref, wai_ref, ba_ref, bi_ref, lam_ref,
                   wout_ref,
                   y_ref, c_ref, n_ref, m_ref, h_ref, cv_ref,
                   ubuf_s, hm_s, *, ns, t):
    c_ref[...] = c0_ref[...]
    n_ref[...] = n0_ref[...]
    m_ref[...] = m0_ref[...]
    pad = t + SUBLANES

    convs = []
    for s in range(ns):
        one = pl.ds(s, 1)
        _mlstm_blocks(znat_ref, gbias_ref, onorm_ref, c_ref.at[one], n_ref.at[one], m_ref.at[one], hm_s,
                      t, t, base=s * t)
        u = zxy_ref[s * t:(s + 1) * t, 0:LRU_WIDTH]
        ubuf_s[s * pad:s * pad + SUBLANES, :] = cv0_ref[s]
        ubuf_s[s * pad + SUBLANES:(s + 1) * pad, :] = u
        cv_ref[s] = ubuf_s[s * pad + t:(s + 1) * pad, :]
        conv = convb_ref[...] + u * convw_ref[CONV_W - 1:CONV_W, :]
        for j in range(1, CONV_W):
            conv = conv + (ubuf_s[pl.ds(s * pad + SUBLANES - j, t), :]
                           * convw_ref[CONV_W - 1 - j:CONV_W - j, :])
        convs.append(conv)
    a, xin = _lru_gates(jnp.concatenate(convs, axis=0), wai_ref, ba_ref, bi_ref, lam_ref)

    rmod = lax.broadcasted_iota(jnp.int32, (ns * t, LRU_WIDTH), 0) & (SUBLANES - 1)
    d = 1
    while d < SUBLANES:
        keep = rmod >= d
        xin = xin + a * jnp.where(keep, pltpu.roll(xin, d, axis=0), 0.0)
        a = a * jnp.where(keep, pltpu.roll(a, d, axis=0), 1.0)
        d *= 2
    groups = []
    for s in range(ns):
        carry = h0_ref[s]
        for gidx in range(t // SUBLANES):
            lo = s * t + gidx * SUBLANES
            hb = xin[lo:lo + SUBLANES, :] + a[lo:lo + SUBLANES, :] * carry
            carry = hb[SUBLANES - 1:SUBLANES, :]
            groups.append(hb)
        h_ref[s] = carry
    h_lru = jnp.concatenate(groups, axis=0)

    hl = (jax.nn.gelu(zxy_ref[:, LRU_WIDTH:]) * h_lru).astype(BF16)
    out = (_dot(hm_s[...], wout_ref[0:ML_WIDTH, :])
           + _dot(hl, wout_ref[ML_WIDTH:ML_WIDTH + LRU_WIDTH, :]))
    y_ref[...] = x_ref[...] + out


def _smixer(znat, zxy, x, st, p, l, *, nb, t, ns):
    tok = lambda b: (b, 0)
    lay3 = lambda b: (l, 0, 0)
    lay4 = lambda b: (l, 0, 0, 0)
    st3 = lambda b: (b, 0, 0)
    lst = lambda b: (l, b, 0, 0)
    c0, n0, m0, h0, cv0 = st
    return pl.pallas_call(
        functools.partial(_smixer_kernel, ns=ns, t=t),
        out_shape=(
            jax.ShapeDtypeStruct((nb * t, D_MODEL), F32),
            jax.ShapeDtypeStruct((nb, ML_HEADS, ML_DV, ML_DK), F32),
            jax.ShapeDtypeStruct((nb, ML_HEADS, ML_DK), F32),
            jax.ShapeDtypeStruct((nb, 1, LANES), F32),
            jax.ShapeDtypeStruct((nb, 1, LRU_WIDTH), F32),
            jax.ShapeDtypeStruct((nb, SUBLANES, LRU_WIDTH), F32),
        ),
        grid=(nb // ns,),
        in_specs=[
            pl.BlockSpec((ns * t, Z_NAT), tok),
            pl.BlockSpec((ns * t, Z_XY), tok),
            pl.BlockSpec((ns * t, D_MODEL), tok),
            pl.BlockSpec((None, ns, ML_HEADS, ML_DV, ML_DK), lambda b: (l, b, 0, 0, 0)),
            pl.BlockSpec((None, ns, ML_HEADS, ML_DK), lst),
            pl.BlockSpec((None, ns, 1, LANES), lst),
            pl.BlockSpec((None, ns, 1, LRU_WIDTH), lst),
            pl.BlockSpec((None, ns, SUBLANES, LRU_WIDTH), lst),
            pl.BlockSpec((None, 1, LANES), lay3),
            pl.BlockSpec((None, ML_HEADS, ML_DV), lay3),
            pl.BlockSpec((None, CONV_W, LRU_WIDTH), lay3),
            pl.BlockSpec((None, 1, LRU_WIDTH), lay3),
            pl.BlockSpec((None, LRU_BLOCKS, LRU_BW, 2 * LRU_BW), lay4),
            pl.BlockSpec((None, 1, LRU_WIDTH), lay3),
            pl.BlockSpec((None, 1, LRU_WIDTH), lay3),
            pl.BlockSpec((None, 1, LRU_WIDTH), lay3),
            pl.BlockSpec((None, ML_WIDTH + LRU_WIDTH, D_MODEL), lay3),
        ],
        out_specs=(
            pl.BlockSpec((ns * t, D_MODEL), tok),
            pl.BlockSpec((ns, ML_HEADS, ML_DV, ML_DK), lambda b: (b, 0, 0, 0)),
            pl.BlockSpec((ns, ML_HEADS, ML_DK), st3),
            pl.BlockSpec((ns, 1, LANES), st3),
            pl.BlockSpec((ns, 1, LRU_WIDTH), st3),
            pl.BlockSpec((ns, SUBLANES, LRU_WIDTH), st3),
        ),
        scratch_shapes=[pltpu.VMEM((ns * (t + SUBLANES), LRU_WIDTH), F32),
                        pltpu.VMEM((ns * t, ML_WIDTH), BF16)],
        compiler_params=_params("parallel"),
        name="smixer",
    )(znat, zxy, x, c0, n0, m0, h0, cv0, p["gbias"], p["ml_out_norm"], p["lru_conv_w"],
      p["lru_conv_b"], p["w_ai"], p["lru_b_a"], p["lru_b_i"], p["lru_lambda"], p["w_out"])


def _xattn_kernel(x_ref, g_ref, wq_ref, mk_ref, mv_ref, wo_ref, y_ref, o_s, *, ns, rows, split):
    step = rows // split
    chunks = [(s, slice(s * rows + c * step, s * rows + (c + 1) * step)) for s in range(ns) for c in range(split)]
    heads = [slice(h * XA_DH, (h + 1) * XA_DH) for h in range(XA_HEADS)]

    def scores(s, r):
        q = _dot(_rms(x_ref[r, :], g_ref[...]).astype(BF16), wq_ref[...])
        return [_nt_dot(q[:, c].astype(BF16), mk_ref[s, :, c].astype(BF16)) * (XA_DH ** -0.5) for c in heads]

    def attend(s, r, sc):
        for c, sch in zip(heads, sc):
            p = jnp.exp(sch - jnp.max(sch, axis=1, keepdims=True))
            p = p / jnp.sum(p, axis=1, keepdims=True)
            o_s[r, c] = _dot(p.astype(BF16), mv_ref[s, :, c].astype(BF16)).astype(BF16)

    def project(r):
        y_ref[r, :] = x_ref[r, :] + _dot(o_s[r, :], wo_ref[...])

    pending = None
    for s, r in chunks:
        sc = scores(s, r)
        if pending is not None:
            ps, pr, psc = pending
            attend(ps, pr, psc)
            project(pr)
        pending = (s, r, sc)
    ps, pr, psc = pending
    attend(ps, pr, psc)
    project(pr)


def _xattn(x, g, wq, mk, mv, wo, l, *, nb, t, tt, ns=1, split=1):
    nt = t // tt
    assert ns == 1 or nt == 1
    tok = lambda b, i: (b * nt + i, 0)
    lay3 = lambda b, i: (l, 0, 0)
    mem = lambda b, i: (l, b, 0, 0)
    return pl.pallas_call(
        functools.partial(_xattn_kernel, ns=ns, rows=tt, split=split),
        out_shape=jax.ShapeDtypeStruct((nb * t, D_MODEL), F32),
        grid=(nb // ns, nt),
        in_specs=[
            pl.BlockSpec((ns * tt, D_MODEL), tok),
            pl.BlockSpec((None, 1, D_MODEL), lay3),
            pl.BlockSpec((None, D_MODEL, D_MODEL), lay3),
            pl.BlockSpec((None, ns, MEM_LEN, D_MODEL), mem),
            pl.BlockSpec((None, ns, MEM_LEN, D_MODEL), mem),
            pl.BlockSpec((None, D_MODEL, D_MODEL), lay3),
        ],
        out_specs=pl.BlockSpec((ns * tt, D_MODEL), tok),
        scratch_shapes=[pltpu.VMEM((ns * tt, D_MODEL), BF16)],
        compiler_params=_params("parallel", "arbitrary"),
        name="xattn",
    )(x, g, wq, mk, mv, wo)


def _memkv_kernel(mem_ref, g_ref, wk_ref, wv_ref, k_ref, v_ref):
    mn = _rms(mem_ref[...], g_ref[...]).astype(BF16)
    k_ref[...] = _dot(mn, wk_ref[...])
    v_ref[...] = _dot(mn, wv_ref[...])


def _memkv(mem, g, wk, wv):
    nb = mem.shape[0]
    wspec = pl.BlockSpec((None, D_MODEL, D_MODEL), lambda l, b: (l, 0, 0))
    ospec = pl.BlockSpec((None, None, MEM_LEN, D_MODEL), lambda l, b: (l, b, 0, 0))
    out = jax.ShapeDtypeStruct((DEPTH, nb, MEM_LEN, D_MODEL), F32)
    return pl.pallas_call(
        _memkv_kernel,
        out_shape=(out, out),
        grid=(DEPTH, nb),
        in_specs=[pl.BlockSpec((None, MEM_LEN, D_MODEL), lambda l, b: (b, 0, 0)),
                  pl.BlockSpec((None, 1, D_MODEL), lambda l, b: (l, 0, 0)), wspec, wspec],
        out_specs=(ospec, ospec),
        compiler_params=_params("parallel", "parallel"),
        name="memkv",
    )(mem, g, wk, wv)


def kernel(x_prompt, x_sample, mem_prompt, state_mlstm_C, state_mlstm_n, state_mlstm_m, state_lru_h,
           state_lru_conv, cache_mem_k, cache_mem_v, ffn1_norm, ffn1_w_gate, ffn1_w_up, ffn1_w_down,
           mix_norm, w_in, ml_b_i, ml_b_f, ml_out_norm, lru_conv_w, lru_conv_b, lru_w_a, lru_b_a,
           lru_w_i, lru_b_i, lru_lambda, w_out, xattn_norm, mem_norm, xattn_w_q, xattn_w_k, xattn_w_v,
           xattn_w_o, ffn2_norm, ffn2_w_gate, ffn2_w_up, ffn2_w_down, final_norm):
    bp, tp, _ = x_prompt.shape
    bs, ts, _ = x_sample.shape
    xp = x_prompt.reshape(bp * tp, D_MODEL)
    xs = x_sample.reshape(bs * ts, D_MODEL)

    bf = lambda w: w.astype(BF16)
    row = lambda v: v.astype(F32).reshape(DEPTH, 1, -1)
    gate_pad = ((0, 0), (0, 0), (0, LANES - 2 * ML_HEADS))
    w_in_b = bf(w_in)
    p = {
        "mix_norm": row(mix_norm),
        "w_in": w_in_b,
        "w_xy": w_in_b[:, :, W_X:],
        "gbias": jnp.pad(jnp.concatenate([ml_b_i, ml_b_f], axis=1).astype(F32)[:, None, :], gate_pad),
        "ml_out_norm": ml_out_norm.astype(F32),
        "lru_conv_w": lru_conv_w, "lru_conv_b": row(lru_conv_b),
        "w_ai": bf(jnp.concatenate([lru_w_a, lru_w_i], axis=-1)),
        "lru_b_a": row(lru_b_a), "lru_b_i": row(lru_b_i), "lru_lambda": row(lru_lambda),
        "w_out": bf(w_out),
    }
    ffn1 = (row(ffn1_norm), bf(ffn1_w_gate), bf(ffn1_w_up), bf(ffn1_w_down))
    ffn2 = (row(ffn2_norm), bf(ffn2_w_gate), bf(ffn2_w_up), bf(ffn2_w_down))
    xa_g, xa_q, xa_o = row(xattn_norm), bf(xattn_w_q), bf(xattn_w_o)
    fin = final_norm.astype(F32).reshape(1, D_MODEL)

    pk, pv = _memkv(mem_prompt, row(mem_norm), bf(xattn_w_k), bf(xattn_w_v))
    sk = cache_mem_k.reshape(DEPTH, bs, MEM_LEN, D_MODEL)
    sv = cache_mem_v.reshape(DEPTH, bs, MEM_LEN, D_MODEL)
    s_state = (
        state_mlstm_C.astype(F32), state_mlstm_n.astype(F32),
        jnp.pad(state_mlstm_m.astype(F32)[:, :, None, :], ((0, 0), (0, 0), (0, 0), (0, LANES - ML_HEADS))),
        state_lru_h.astype(F32)[:, :, None, :],
        jnp.pad(state_lru_conv.astype(F32), ((0, 0), (0, 0), (SUBLANES - (CONV_W - 1), 0), (0, 0))),
    )

    p_out = [[] for _ in range(5)]
    s_out = [[] for _ in range(5)]
    for l in range(DEPTH):
        last = l == DEPTH - 1
        xp, xs = _ffn(xp, xs, *ffn1, fin, l, tm=512, tf=512, final_norm=False)
        xp, *st = _pmixer(xp, p, l, nb=bp, t=tp, tt=512, cl=ML_BLOCK)
        xp = _xattn(xp, xa_g, xa_q, pk, pv, xa_o, l, nb=bp, t=tp, tt=1024, split=4)
        for acc, v in zip(p_out, st):
            acc.append(v)
        znat, zxy = _zproj(xs, p, l)
        xs, *st = _smixer(znat, zxy, xs, s_state, p, l, nb=bs, t=ts, ns=4)
        xs = _xattn(xs, xa_g, xa_q, sk, sv, xa_o, l, nb=bs, t=ts, tt=ts, ns=4)
        for acc, v in zip(s_out, st):
            acc.append(v)
        xp, xs = _ffn(xp, xs, *ffn2, fin, l, tm=512, tf=512, final_norm=last)

    def states(acc):
        c, n, m, h, cv = (jnp.stack(a) for a in acc)
        return c, n, m[:, :, 0, :ML_HEADS], h[:, :, 0, :], cv[:, :, SUBLANES - (CONV_W - 1):, :]

    return (xp.reshape(bp, tp, D_MODEL), xs.reshape(bs, ts, D_MODEL),
            *states(p_out),
            pk.reshape(DEPTH, bp, MEM_LEN, XA_HEADS, XA_DH), pv.reshape(DEPTH, bp, MEM_LEN, XA_HEADS, XA_DH),
            *states(s_out))
```

```python
import functools

import jax
import jax.numpy as jnp
from jax import lax
from jax.experimental import pallas as pl
from jax.experimental.pallas import tpu as pltpu

F32 = jnp.float32
BF16 = jnp.bfloat16

D_MODEL = 1024
DEPTH = 2
ML_BLOCK = 128
ML_HEADS = 4
ML_DV = 256
ML_DK = 128
ML_WIDTH = ML_HEADS * ML_DV
LRU_WIDTH = 1024
LRU_BLOCKS = 8
LRU_BW = LRU_WIDTH // LRU_BLOCKS
CONV_W = 4
LRU_C = 8.0
MEM_LEN = 256
XA_HEADS = 4
XA_DH = D_MODEL // XA_HEADS
D_FF = 4 * D_MODEL
EPS = 1e-6

LANES = 128
SUBLANES = 8
VMEM_LIMIT = 56 * 1024 * 1024

Z_Q = 0
Z_K = Z_Q + ML_HEADS * ML_DK
Z_V = Z_K + ML_HEADS * ML_DK
Z_O = Z_V + ML_WIDTH
Z_G = Z_O + ML_WIDTH
Z_NAT = Z_G + LANES
Z_CHUNK = 768
Z_XY = 2 * LRU_WIDTH
W_X = Z_G + 2 * ML_HEADS


def _rms(x, g):
    return x * lax.rsqrt(jnp.mean(x * x, axis=-1, keepdims=True) + EPS) * g


def _nt_dot(a, b):
    return lax.dot_general(a, b, (((1,), (1,)), ((), ())), preferred_element_type=F32)


def _tn_dot(a, b):
    return lax.dot_general(a, b, (((0,), (0,)), ((), ())), preferred_element_type=F32)


def _dot(a, b):
    return jnp.dot(a, b, preferred_element_type=F32)


def _params(*sem):
    return pltpu.CompilerParams(dimension_semantics=sem, vmem_limit_bytes=VMEM_LIMIT)


def _resident(shape, index_map):
    return pl.BlockSpec(shape, index_map, pipeline_mode=pl.Buffered(1))


def _ffn_body(parts, g_ref, wg_ref, wu_ref, wd_ref, fg_ref, h_s, *, tf, final_norm):
    xs = [x_ref[...] for x_ref, _ in parts]
    rows = sum(x.shape[0] for x in xs)
    xn = jnp.concatenate([_rms(x, g_ref[...]).astype(BF16) for x in xs], axis=0)
    for j in range(D_FF // tf):
        cols = slice(j * tf, (j + 1) * tf)
        g = _dot(xn, wg_ref[:, cols])
        u = _dot(xn, wu_ref[:, cols])
        h_s[0:rows, cols] = (g * jax.nn.sigmoid(g) * u).astype(BF16)
    d = _dot(h_s[0:rows, :], wd_ref[...])
    lo = 0
    for x, (_, o_ref) in zip(xs, parts):
        y = x + 0.5 * d[lo:lo + x.shape[0], :]
        if final_norm:
            y = _rms(y, fg_ref[...])
        o_ref[...] = y
        lo += x.shape[0]


def _ffn_kernel(xp_ref, xs_ref, g_ref, wg_ref, wu_ref, wd_ref, fg_ref, op_ref, os_ref, h_s, **kw):
    last = pl.num_programs(0) - 1

    @pl.when(pl.program_id(0) < last)
    def _():
        _ffn_body([(xp_ref, op_ref)], g_ref, wg_ref, wu_ref, wd_ref, fg_ref, h_s, **kw)

    @pl.when(pl.program_id(0) == last)
    def _():
        _ffn_body([(xp_ref, op_ref), (xs_ref, os_ref)], g_ref, wg_ref, wu_ref, wd_ref, fg_ref, h_s, **kw)


def _ffn(xp, xs, g, wg, wu, wd, fg, l, *, tm, tf, final_norm):
    np_tiles = xp.shape[0] // tm
    ms = xs.shape[0]
    ptile = lambda i: (i, 0)
    whole = lambda i: (0, 0)
    lay = lambda i: (l, 0, 0)
    return pl.pallas_call(
        functools.partial(_ffn_kernel, tf=tf, final_norm=final_norm),
        out_shape=(jax.ShapeDtypeStruct(xp.shape, F32), jax.ShapeDtypeStruct(xs.shape, F32)),
        grid=(np_tiles,),
        in_specs=[
            pl.BlockSpec((tm, D_MODEL), ptile),
            _resident((ms, D_MODEL), whole),
            _resident((None, 1, D_MODEL), lay),
            _resident((None, D_MODEL, D_FF), lay),
            _resident((None, D_MODEL, D_FF), lay),
            _resident((None, D_FF, D_MODEL), lay),
            _resident((1, D_MODEL), whole),
        ],
        out_specs=(pl.BlockSpec((tm, D_MODEL), ptile), pl.BlockSpec((ms, D_MODEL), whole)),
        scratch_shapes=[pltpu.VMEM((tm + ms, D_FF), BF16)],
        compiler_params=_params("arbitrary"),
        name="ffn",
    )(xp, xs, g, wg, wu, wd, fg)


def _log_sigmoid(x):
    return jnp.minimum(x, 0.0) - jnp.log1p(jnp.exp(-jnp.abs(x)))


def _softplus(x):
    return jnp.maximum(x, 0.0) + jnp.log1p(jnp.exp(-jnp.abs(x)))


def _lane_pick(x, lane_ids, idx):
    return jnp.sum(jnp.where(lane_ids == idx, x, 0.0), axis=1, keepdims=True)


def _mlstm_blocks(z_ref, gbias_ref, onorm_ref, c_ref, n_ref, m_ref, hm_s, rows, cl, base=0):
    lane1 = lax.broadcasted_iota(jnp.int32, (1, LANES), 1)
    m_vec = m_ref[0]
    m_heads = [_lane_pick(m_vec, lane1, h) for h in range(ML_HEADS)]
    for c in range(rows // cl):
        m_heads = _mlstm_chunk(z_ref, gbias_ref, onorm_ref, c_ref, n_ref, m_heads, hm_s, base + c * cl, cl)
    for h in range(ML_HEADS):
        m_vec = jnp.where(lane1 == h, m_heads[h], m_vec)
    m_ref[0] = m_vec


def _mlstm_chunk(z_ref, gbias_ref, onorm_ref, c_ref, n_ref, m_heads, hm_s, r0, cl):
    lane = lax.broadcasted_iota(jnp.int32, (cl, LANES), 1)
    row = lax.broadcasted_iota(jnp.int32, (cl, LANES), 0)
    tril = (lax.broadcasted_iota(jnp.int32, (cl, cl), 1)
            <= lax.broadcasted_iota(jnp.int32, (cl, cl), 0))

    gates = z_ref[r0:r0 + cl, Z_G:Z_G + LANES] + gbias_ref[...]
    bsum = _log_sigmoid(gates)
    d = 1
    while d < cl:
        bsum = bsum + jnp.where(row >= d, pltpu.roll(bsum, d, axis=0), 0.0)
        d *= 2
    mixed = jnp.where(lane < ML_HEADS, gates, bsum)
    if cl < LANES:
        mixed = jnp.concatenate([mixed, jnp.zeros((LANES - cl, LANES), F32)], axis=0)
    mixed_t = mixed.T

    m_next = []
    for h in range(ML_HEADS):
        b_col = _lane_pick(bsum, lane, ML_HEADS + h)
        ig_col = _lane_pick(gates, lane, h)
        b_row = mixed_t[ML_HEADS + h:ML_HEADS + h + 1, 0:cl]
        ig_row = mixed_t[h:h + 1, 0:cl]
        m_prev = m_heads[h]

        dmat = jnp.where(tril, b_col - b_row + ig_row, -jnp.inf)
        inter = b_col + m_prev
        m_t = jnp.maximum(inter, jnp.max(dmat, axis=1, keepdims=True))

        qf = z_ref[r0:r0 + cl, Z_Q + h * ML_DK:Z_Q + (h + 1) * ML_DK] * (ML_DK ** -0.5)
        kf = z_ref[r0:r0 + cl, Z_K + h * ML_DK:Z_K + (h + 1) * ML_DK]
        vf = z_ref[r0:r0 + cl, Z_V + h * ML_DV:Z_V + (h + 1) * ML_DV]
        q = qf.astype(BF16)
        k = kf.astype(BF16)

        s = _nt_dot(q, k) * jnp.exp(dmat - m_t)
        w_inter = jnp.exp(inter - m_t)
        c_old = c_ref[0, h]
        n_old = n_ref[0, h:h + 1, :]
        num = w_inter * _nt_dot(q, c_old.astype(BF16)) + _dot(s.astype(BF16), vf.astype(BF16))
        den = (w_inter * jnp.sum(qf * n_old, axis=1, keepdims=True)
               + jnp.sum(s, axis=1, keepdims=True))
        hh = num / jnp.maximum(jnp.abs(den), jnp.exp(-m_t))

        hh = hh * lax.rsqrt(jnp.mean(hh * hh, axis=1, keepdims=True) + EPS) * onorm_ref[h:h + 1, :]
        o_gate = jax.nn.sigmoid(z_ref[r0:r0 + cl, Z_O + h * ML_DV:Z_O + (h + 1) * ML_DV])
        hm_s[r0:r0 + cl, h * ML_DV:(h + 1) * ML_DV] = (o_gate * hh).astype(BF16)

        m_new = m_t[cl - 1:cl, :]
        b_last = b_col[cl - 1:cl, :]
        wgt = jnp.exp(b_last - b_col + ig_col - m_new)
        decay = jnp.exp(b_last + m_prev - m_new)
        c_ref[0, h] = decay * c_old + _tn_dot((wgt * vf).astype(BF16), k)
        n_ref[0, h:h + 1, :] = decay * n_old + jnp.sum(wgt * kf, axis=0, keepdims=True)
        m_next.append(m_new)
    return m_next


def _lru_gates(conv, wai_ref, ba_ref, bi_ref, lam_ref):
    conv_b = conv.astype(BF16)
    pre = [_dot(conv_b[:, n * LRU_BW:(n + 1) * LRU_BW], wai_ref[n])
           for n in range(LRU_BLOCKS)]
    r_gate = jax.nn.sigmoid(jnp.concatenate([p[:, :LRU_BW] for p in pre], axis=1) + ba_ref[...])
    i_gate = jax.nn.sigmoid(jnp.concatenate([p[:, LRU_BW:] for p in pre], axis=1) + bi_ref[...])
    log_a = r_gate * (-LRU_C * _softplus(-lam_ref[...]))
    a = jnp.exp(log_a)
    w = -jnp.tanh(log_a) * (a * a + 1.0)
    xin = jnp.where(w > 0.0, w * lax.rsqrt(w), 0.0) * (i_gate * conv)
    return a, xin


def _pmixer_project(x_ref, gmix_ref, wxy_ref, zxy_dst, pbuf, xnb_s, *, tt):
    seg = tt // SUBLANES
    pitch = seg + SUBLANES
    xn = _rms(x_ref[...], gmix_ref[...])
    xnb_s[...] = xn.astype(BF16)
    for k in range(LRU_WIDTH // LANES):
        for s in range(SUBLANES):
            pbuf[k, pl.ds(s * pitch, seg), :] = xn[s * seg:(s + 1) * seg, k * LANES:(k + 1) * LANES]
    xnp = jnp.stack(
        [jnp.concatenate([pbuf[k, pl.ds(j, SUBLANES, stride=pitch), :]
                          for k in range(LRU_WIDTH // LANES)], axis=1) for j in range(seg)], axis=0)
    zxy_dst[...] = _dot(xnp.reshape(tt, D_MODEL).astype(BF16), wxy_ref[...])


def _pmixer_mix(x_ref, xnb_s, wnat_ref, z_s, zxy_s, gbias_ref, onorm_ref, convw_ref, convb_ref, wai_ref, ba_ref, bi_ref,
                lam_ref, wout_ref, y_ref, c_ref, n_ref, m_ref, h_ref, cv_ref, obuf, hm_s, *, cl, tt):
    seg = tt // SUBLANES
    pitch = seg + SUBLANES

    u3 = zxy_s[:, 0:LRU_WIDTH].reshape(seg, SUBLANES, LRU_WIDTH)
    tail = cv_ref[0]
    sub = lax.broadcasted_iota(jnp.int32, (SUBLANES, LRU_WIDTH), 0)
    wrap = []
    for i in range(CONV_W - 1):
        prev = pltpu.roll(u3[seg - (CONV_W - 1) + i], 1, axis=0)
        fill = tail[SUBLANES - (CONV_W - 1) + i:SUBLANES - (CONV_W - 1) + i + 1, :]
        wrap.append(jnp.where(sub == 0, fill, prev))
        cv_ref[0, SUBLANES - (CONV_W - 1) + i:SUBLANES - (CONV_W - 1) + i + 1, :] = (
            u3[seg - (CONV_W - 1) + i][SUBLANES - 1:SUBLANES, :])
    ext = jnp.concatenate([jnp.stack(wrap, axis=0), u3], axis=0)
    conv3 = convb_ref[...] + ext[CONV_W - 1:] * convw_ref[CONV_W - 1:CONV_W, :]
    for j in range(1, CONV_W):
        conv3 = conv3 + ext[CONV_W - 1 - j:CONV_W - 1 - j + seg] * convw_ref[CONV_W - 1 - j:CONV_W - j, :]

    conv = conv3.reshape(tt, LRU_WIDTH)
    bounds = [min(k * Z_CHUNK, Z_NAT) for k in range(-(-Z_NAT // Z_CHUNK) + 1)]
    zcols = list(zip(bounds[:-1], bounds[1:]))

    def project(k):
        lo, hi = zcols[k]
        z_s[:, lo:hi] = _dot(xnb_s[...], wnat_ref[:, lo:hi])

    project(0)
    nrow = len(zcols) - 1
    step = tt // nrow
    a_parts, x_parts = [], []
    for c in range(nrow):
        r = slice(c * step, (c + 1) * step)
        a_c, x_c = _lru_gates(conv[r], wai_ref, ba_ref, bi_ref, lam_ref)
        a_parts.append(a_c)
        x_parts.append(x_c)
        project(c + 1)
    a = jnp.concatenate(a_parts, axis=0)
    xin = jnp.concatenate(x_parts, axis=0)

    a3 = a.reshape(seg, SUBLANES, LRU_WIDTH)
    x3 = xin.reshape(seg, SUBLANES, LRU_WIDTH)
    hs = [x3[0]]
    ps = [a3[0]]
    for j in range(1, seg):
        hs.append(a3[j] * hs[-1] + x3[j])
        ps.append(a3[j] * ps[-1])
    carry = h_ref[0]
    cin = []
    for s in range(SUBLANES):
        cin.append(carry)
        carry = hs[-1][s:s + 1, :] + ps[-1][s:s + 1, :] * carry
    h_ref[0] = carry
    cin = jnp.concatenate(cin, axis=0)
    h3 = jnp.stack([hs[j] + ps[j] * cin for j in range(seg)], axis=0)

    hl = (jax.nn.gelu(zxy_s[:, LRU_WIDTH:]) * h3.reshape(tt, LRU_WIDTH)).astype(BF16)
    ol3 = _dot(hl, wout_ref[ML_WIDTH:ML_WIDTH + LRU_WIDTH, :]).reshape(seg, SUBLANES, D_MODEL)
    for j in range(seg):
        for k in range(D_MODEL // LANES):
            obuf[k, pl.ds(j, SUBLANES, stride=pitch), :] = ol3[j][:, k * LANES:(k + 1) * LANES]
    out_lru = jnp.concatenate(
        [jnp.concatenate([obuf[k, pl.ds(s * pitch, seg), :] for s in range(SUBLANES)], axis=0)
         for k in range(D_MODEL // LANES)], axis=1)

    _mlstm_blocks(z_s, gbias_ref, onorm_ref, c_ref, n_ref, m_ref, hm_s, tt, cl)

    y_ref[...] = x_ref[...] + _dot(hm_s[...], wout_ref[0:ML_WIDTH, :]) + out_lru


def _pmixer_kernel(x_ref, gmix_ref, wnat_ref, wxy_ref, gbias_ref, onorm_ref, convw_ref,
                   convb_ref, wai_ref, ba_ref, bi_ref, lam_ref, wout_ref,
                   y_ref, c_ref, n_ref, m_ref, h_ref, cv_ref,
                   z_s, zxy_s, pbuf, obuf, hm_s, xnb_s, *, cl, tt):
    @pl.when(pl.program_id(1) == 0)
    def _():
        c_ref[...] = jnp.zeros_like(c_ref)
        n_ref[...] = jnp.zeros_like(n_ref)
        m_ref[...] = jnp.zeros_like(m_ref)
        h_ref[...] = jnp.zeros_like(h_ref)
        cv_ref[...] = jnp.zeros_like(cv_ref)

    _pmixer_project(x_ref, gmix_ref, wxy_ref, zxy_s, pbuf, xnb_s, tt=tt)
    _pmixer_mix(x_ref, xnb_s, wnat_ref, z_s, zxy_s, gbias_ref, onorm_ref, convw_ref, convb_ref, wai_ref, ba_ref,
                bi_ref, lam_ref, wout_ref, y_ref, c_ref, n_ref, m_ref, h_ref, cv_ref, obuf, hm_s,
                cl=cl, tt=tt)


def _pmixer(x, p, l, *, nb, t, tt, cl):
    nt = t // tt
    pitch = tt // SUBLANES + SUBLANES
    tok = lambda b, i: (b * nt + i, 0)
    st4 = lambda b, i: (b, 0, 0, 0)
    st3 = lambda b, i: (b, 0, 0)
    lay3 = lambda b, i: (l, 0, 0)
    lay4 = lambda b, i: (l, 0, 0, 0)
    return pl.pallas_call(
        functools.partial(_pmixer_kernel, cl=cl, tt=tt),
        out_shape=(
            jax.ShapeDtypeStruct((nb * t, D_MODEL), F32),
            jax.ShapeDtypeStruct((nb, ML_HEADS, ML_DV, ML_DK), F32),
            jax.ShapeDtypeStruct((nb, ML_HEADS, ML_DK), F32),
            jax.ShapeDtypeStruct((nb, 1, LANES), F32),
            jax.ShapeDtypeStruct((nb, 1, LRU_WIDTH), F32),
            jax.ShapeDtypeStruct((nb, SUBLANES, LRU_WIDTH), F32),
        ),
        grid=(nb, nt),
        in_specs=[
            pl.BlockSpec((tt, D_MODEL), tok),
            _resident((None, 1, D_MODEL), lay3),
            _resident((None, D_MODEL, Z_NAT), lay3),
            _resident((None, D_MODEL, Z_XY), lay3),
            _resident((None, 1, LANES), lay3),
            _resident((None, ML_HEADS, ML_DV), lay3),
            _resident((None, CONV_W, LRU_WIDTH), lay3),
            _resident((None, 1, LRU_WIDTH), lay3),
            _resident((None, LRU_BLOCKS, LRU_BW, 2 * LRU_BW), lay4),
            _resident((None, 1, LRU_WIDTH), lay3),
            _resident((None, 1, LRU_WIDTH), lay3),
            _resident((None, 1, LRU_WIDTH), lay3),
            _resident((None, ML_WIDTH + LRU_WIDTH, D_MODEL), lay3),
        ],
        out_specs=(
            pl.BlockSpec((tt, D_MODEL), tok),
            pl.BlockSpec((1, ML_HEADS, ML_DV, ML_DK), st4),
            pl.BlockSpec((1, ML_HEADS, ML_DK), st3),
            pl.BlockSpec((1, 1, LANES), st3),
            pl.BlockSpec((1, 1, LRU_WIDTH), st3),
            pl.BlockSpec((1, SUBLANES, LRU_WIDTH), st3),
        ),
        scratch_shapes=[pltpu.VMEM((tt, Z_NAT), F32), pltpu.VMEM((tt, Z_XY), F32),
                        pltpu.VMEM((LRU_WIDTH // LANES, SUBLANES * pitch, LANES), F32),
                        pltpu.VMEM((D_MODEL // LANES, SUBLANES * pitch, LANES), F32),
                        pltpu.VMEM((tt, ML_WIDTH), BF16), pltpu.VMEM((tt, D_MODEL), BF16)],
        compiler_params=_params("parallel", "arbitrary"),
        name="pmixer",
    )(x, p["mix_norm"], p["w_in"], p["w_xy"], p["gbias"], p["ml_out_norm"], p["lru_conv_w"],
      p["lru_conv_b"], p["w_ai"], p["lru_b_a"], p["lru_b_i"], p["lru_lambda"], p["w_out"])


def _zproj_kernel(x_ref, g_ref, wnat_ref, wxy_ref, znat_ref, zxy_ref):
    xn = _rms(x_ref[...], g_ref[...]).astype(BF16)
    znat_ref[...] = _dot(xn, wnat_ref[...])
    zxy_ref[...] = _dot(xn, wxy_ref[...])


def _zproj(x, p, l):
    m = x.shape[0]
    lay3 = lambda i: (l, 0, 0)
    return pl.pallas_call(
        _zproj_kernel,
        out_shape=(jax.ShapeDtypeStruct((m, Z_NAT), F32), jax.ShapeDtypeStruct((m, Z_XY), F32)),
        grid=(1,),
        in_specs=[
            pl.BlockSpec((m, D_MODEL), lambda i: (0, 0)),
            pl.BlockSpec((None, 1, D_MODEL), lay3),
            pl.BlockSpec((None, D_MODEL, Z_NAT), lay3),
            pl.BlockSpec((None, D_MODEL, Z_XY), lay3),
        ],
        out_specs=(pl.BlockSpec((m, Z_NAT), lambda i: (0, 0)), pl.BlockSpec((m, Z_XY), lambda i: (0, 0))),
        compiler_params=_params("arbitrary"),
        name="zproj",
    )(x, p["mix_norm"], p["w_in"], p["w_xy"])


def _smixer_kernel(znat_ref, zxy_ref, x_ref, c0_ref, n0_ref, m0_ref, h0_ref, cv0_ref,
                   gbias_ref, onorm_ref, convw_ref, convb_ref, wai_ref, ba_ref, bi_ref, lam_ref,
                   wout_ref,
                   y_ref, c_ref, n_ref, m_ref, h_ref, cv_ref,
                   ubuf_s, hm_s, *, ns, t):
    c_ref[...] = c0_ref[...]
    n_ref[...] = n0_ref[...]
    m_ref[...] = m0_ref[...]
    pad = t + SUBLANES

    convs = []
    for s in range(ns):
        one = pl.ds(s, 1)
        _mlstm_blocks(znat_ref, gbias_ref, onorm_ref, c_ref.at[one], n_ref.at[one], m_ref.at[one], hm_s,
                      t, t, base=s * t)
        u = zxy_ref[s * t:(s + 1) * t, 0:LRU_WIDTH]
        ubuf_s[s * pad:s * pad + SUBLANES, :] = cv0_ref[s]
        ubuf_s[s * pad + SUBLANES:(s + 1) * pad, :] = u
        cv_ref[s] = ubuf_s[s * pad + t:(s + 1) * pad, :]
        conv = convb_ref[...] + u * convw_ref[CONV_W - 1:CONV_W, :]
        for j in range(1, CONV_W):
            conv = conv + (ubuf_s[pl.ds(s * pad + SUBLANES - j, t), :]
                           * convw_ref[CONV_W - 1 - j:CONV_W - j, :])
        convs.append(conv)
    a, xin = _lru_gates(jnp.concatenate(convs, axis=0), wai_ref, ba_ref, bi_ref, lam_ref)

    rmod = lax.broadcasted_iota(jnp.int32, (ns * t, LRU_WIDTH), 0) & (SUBLANES - 1)
    d = 1
    while d < SUBLANES:
        keep = rmod >= d
        xin = xin + a * jnp.where(keep, pltpu.roll(xin, d, axis=0), 0.0)
        a = a * jnp.where(keep, pltpu.roll(a, d, axis=0), 1.0)
        d *= 2
    groups = []
    for s in range(ns):
        carry = h0_ref[s]
        for gidx in range(t // SUBLANES):
            lo = s * t + gidx * SUBLANES
            hb = xin[lo:lo + SUBLANES, :] + a[lo:lo + SUBLANES, :] * carry
            carry = hb[SUBLANES - 1:SUBLANES, :]
            groups.append(hb)
        h_ref[s] = carry
    h_lru = jnp.concatenate(groups, axis=0)

    hl = (jax.nn.gelu(zxy_ref[:, LRU_WIDTH:]) * h_lru).astype(BF16)
    out = (_dot(hm_s[...], wout_ref[0:ML_WIDTH, :])
           + _dot(hl, wout_ref[ML_WIDTH:ML_WIDTH + LRU_WIDTH, :]))
    y_ref[...] = x_ref[...] + out


def _smixer(znat, zxy, x, st, p, l, *, nb, t, ns):
    tok = lambda b: (b, 0)
    lay3 = lambda b: (l, 0, 0)
    lay4 = lambda b: (l, 0, 0, 0)
    st3 = lambda b: (b, 0, 0)
    lst = lambda b: (l, b, 0, 0)
    c0, n0, m0, h0, cv0 = st
    return pl.pallas_call(
        functools.partial(_smixer_kernel, ns=ns, t=t),
        out_shape=(
            jax.ShapeDtypeStruct((nb * t, D_MODEL), F32),
            jax.ShapeDtypeStruct((nb, ML_HEADS, ML_DV, ML_DK), F32),
            jax.ShapeDtypeStruct((nb, ML_HEADS, ML_DK), F32),
            jax.ShapeDtypeStruct((nb, 1, LANES), F32),
            jax.ShapeDtypeStruct((nb, 1, LRU_WIDTH), F32),
            jax.ShapeDtypeStruct((nb, SUBLANES, LRU_WIDTH), F32),
        ),
        grid=(nb // ns,),
        in_specs=[
            pl.BlockSpec((ns * t, Z_NAT), tok),
            pl.BlockSpec((ns * t, Z_XY), tok),
            pl.BlockSpec((ns * t, D_MODEL), tok),
            pl.BlockSpec((None, ns, ML_HEADS, ML_DV, ML_DK), lambda b: (l, b, 0, 0, 0)),
            pl.BlockSpec((None, ns, ML_HEADS, ML_DK), lst),
            pl.BlockSpec((None, ns, 1, LANES), lst),
            pl.BlockSpec((None, ns, 1, LRU_WIDTH), lst),
            pl.BlockSpec((None, ns, SUBLANES, LRU_WIDTH), lst),
            pl.BlockSpec((None, 1, LANES), lay3),
            pl.BlockSpec((None, ML_HEADS, ML_DV), lay3),
            pl.BlockSpec((None, CONV_W, LRU_WIDTH), lay3),
            pl.BlockSpec((None, 1, LRU_WIDTH), lay3),
            pl.BlockSpec((None, LRU_BLOCKS, LRU_BW, 2 * LRU_BW), lay4),
            pl.BlockSpec((None, 1, LRU_WIDTH), lay3),
            pl.BlockSpec((None, 1, LRU_WIDTH), lay3),
            pl.BlockSpec((None, 1, LRU_WIDTH), lay3),
            pl.BlockSpec((None, ML_WIDTH + LRU_WIDTH, D_MODEL), lay3),
        ],
        out_specs=(
            pl.BlockSpec((ns * t, D_MODEL), tok),
            pl.BlockSpec((ns, ML_HEADS, ML_DV, ML_DK), lambda b: (b, 0, 0, 0)),
            pl.BlockSpec((ns, ML_HEADS, ML_DK), st3),
            pl.BlockSpec((ns, 1, LANES), st3),
            pl.BlockSpec((ns, 1, LRU_WIDTH), st3),
            pl.BlockSpec((ns, SUBLANES, LRU_WIDTH), st3),
        ),
        scratch_shapes=[pltpu.VMEM((ns * (t + SUBLANES), LRU_WIDTH), F32),
                        pltpu.VMEM((ns * t, ML_WIDTH), BF16)],
        compiler_params=_params("parallel"),
        name="smixer",
    )(znat, zxy, x, c0, n0, m0, h0, cv0, p["gbias"], p["ml_out_norm"], p["lru_conv_w"],
      p["lru_conv_b"], p["w_ai"], p["lru_b_a"], p["lru_b_i"], p["lru_lambda"], p["w_out"])


def _xattn_kernel(x_ref, g_ref, wq_ref, mk_ref, mv_ref, wo_ref, y_ref, o_s, *, ns, rows, split):
    step = rows // split
    chunks = [(s, slice(s * rows + c * step, s * rows + (c + 1) * step)) for s in range(ns) for c in range(split)]
    heads = [slice(h * XA_DH, (h + 1) * XA_DH) for h in range(XA_HEADS)]

    def scores(s, r):
        q = _dot(_rms(x_ref[r, :], g_ref[...]).astype(BF16), wq_ref[...])
        return [_nt_dot(q[:, c].astype(BF16), mk_ref[s, :, c].astype(BF16)) * (XA_DH ** -0.5) for c in heads]

    def attend(s, r, sc):
        for c, sch in zip(heads, sc):
            p = jnp.exp(sch - jnp.max(sch, axis=1, keepdims=True))
            p = p / jnp.sum(p, axis=1, keepdims=True)
            o_s[r, c] = _dot(p.astype(BF16), mv_ref[s, :, c].astype(BF16)).astype(BF16)

    def project(r):
        y_ref[r, :] = x_ref[r, :] + _dot(o_s[r, :], wo_ref[...])

    pending = None
    for s, r in chunks:
        sc = scores(s, r)
        if pending is not None:
            ps, pr, psc = pending
            attend(ps, pr, psc)
            project(pr)
        pending = (s, r, sc)
    ps, pr, psc = pending
    attend(ps, pr, psc)
    project(pr)


def _xattn(x, g, wq, mk, mv, wo, l, *, nb, t, tt, ns=1, split=1):
    nt = t // tt
    assert ns == 1 or nt == 1
    tok = lambda b, i: (b * nt + i, 0)
    lay3 = lambda b, i: (l, 0, 0)
    mem = lambda b, i: (l, b, 0, 0)
    return pl.pallas_call(
        functools.partial(_xattn_kernel, ns=ns, rows=tt, split=split),
        out_shape=jax.ShapeDtypeStruct((nb * t, D_MODEL), F32),
        grid=(nb // ns, nt),
        in_specs=[
            pl.BlockSpec((ns * tt, D_MODEL), tok),
            pl.BlockSpec((None, 1, D_MODEL), lay3),
            pl.BlockSpec((None, D_MODEL, D_MODEL), lay3),
            pl.BlockSpec((None, ns, MEM_LEN, D_MODEL), mem),
            pl.BlockSpec((None, ns, MEM_LEN, D_MODEL), mem),
            pl.BlockSpec((None, D_MODEL, D_MODEL), lay3),
        ],
        out_specs=pl.BlockSpec((ns * tt, D_MODEL), tok),
        scratch_shapes=[pltpu.VMEM((ns * tt, D_MODEL), BF16)],
        compiler_params=_params("parallel", "arbitrary"),
        name="xattn",
    )(x, g, wq, mk, mv, wo)


def _memkv_kernel(mem_ref, g_ref, wk_ref, wv_ref, k_ref, v_ref):
    mn = _rms(mem_ref[...], g_ref[...]).astype(BF16)
    k_ref[...] = _dot(mn, wk_ref[...])
    v_ref[...] = _dot(mn, wv_ref[...])


def _memkv(mem, g, wk, wv):
    nb = mem.shape[0]
    wspec = pl.BlockSpec((None, D_MODEL, D_MODEL), lambda l, b: (l, 0, 0))
    ospec = pl.BlockSpec((None, None, MEM_LEN, D_MODEL), lambda l, b: (l, b, 0, 0))
    out = jax.ShapeDtypeStruct((DEPTH, nb, MEM_LEN, D_MODEL), F32)
    return pl.pallas_call(
        _memkv_kernel,
        out_shape=(out, out),
        grid=(DEPTH, nb),
        in_specs=[pl.BlockSpec((None, MEM_LEN, D_MODEL), lambda l, b: (b, 0, 0)),
                  pl.BlockSpec((None, 1, D_MODEL), lambda l, b: (l, 0, 0)), wspec, wspec],
        out_specs=(ospec, ospec),
        compiler_params=_params("parallel", "parallel"),
        name="memkv",
    )(mem, g, wk, wv)


def kernel(x_prompt, x_sample, mem_prompt, state_mlstm_C, state_mlstm_n, state_mlstm_m, state_lru_h,
           state_lru_conv, cache_mem_k, cache_mem_v, ffn1_norm, ffn1_w_gate, ffn1_w_up, ffn1_w_down,
           mix_norm, w_in, ml_b_i, ml_b_f, ml_out_norm, lru_conv_w, lru_conv_b, lru_w_a, lru_b_a,
           lru_w_i, lru_b_i, lru_lambda, w_out, xattn_norm, mem_norm, xattn_w_q, xattn_w_k, xattn_w_v,
           xattn_w_o, ffn2_norm, ffn2_w_gate, ffn2_w_up, ffn2_w_down, final_norm):
    bp, tp, _ = x_prompt.shape
    bs, ts, _ = x_sample.shape
    xp = x_prompt.reshape(bp * tp, D_MODEL)
    xs = x_sample.reshape(bs * ts, D_MODEL)

    bf = lambda w: w.astype(BF16)
    row = lambda v: v.astype(F32).reshape(DEPTH, 1, -1)
    gate_pad = ((0, 0), (0, 0), (0, LANES - 2 * ML_HEADS))
    w_in_b = bf(w_in)
    p = {
        "mix_norm": row(mix_norm),
        "w_in": w_in_b,
        "w_xy": w_in_b[:, :, W_X:],
        "gbias": jnp.pad(jnp.concatenate([ml_b_i, ml_b_f], axis=1).astype(F32)[:, None, :], gate_pad),
        "ml_out_norm": ml_out_norm.astype(F32),
        "lru_conv_w": lru_conv_w, "lru_conv_b": row(lru_conv_b),
        "w_ai": bf(jnp.concatenate([lru_w_a, lru_w_i], axis=-1)),
        "lru_b_a": row(lru_b_a), "lru_b_i": row(lru_b_i), "lru_lambda": row(lru_lambda),
        "w_out": bf(w_out),
    }
    ffn1 = (row(ffn1_norm), bf(ffn1_w_gate), bf(ffn1_w_up), bf(ffn1_w_down))
    ffn2 = (row(ffn2_norm), bf(ffn2_w_gate), bf(ffn2_w_up), bf(ffn2_w_down))
    xa_g, xa_q, xa_o = row(xattn_norm), bf(xattn_w_q), bf(xattn_w_o)
    fin = final_norm.astype(F32).reshape(1, D_MODEL)

    pk, pv = _memkv(mem_prompt, row(mem_norm), bf(xattn_w_k), bf(xattn_w_v))
    sk = cache_mem_k.reshape(DEPTH, bs, MEM_LEN, D_MODEL)
    sv = cache_mem_v.reshape(DEPTH, bs, MEM_LEN, D_MODEL)
    s_state = (
        state_mlstm_C.astype(F32), state_mlstm_n.astype(F32),
        jnp.pad(state_mlstm_m.astype(F32)[:, :, None, :], ((0, 0), (0, 0), (0, 0), (0, LANES - ML_HEADS))),
        state_lru_h.astype(F32)[:, :, None, :],
        jnp.pad(state_lru_conv.astype(F32), ((0, 0), (0, 0), (SUBLANES - (CONV_W - 1), 0), (0, 0))),
    )

    p_out = [[] for _ in range(5)]
    s_out = [[] for _ in range(5)]
    for l in range(DEPTH):
        last = l == DEPTH - 1
        xp, xs = _ffn(xp, xs, *ffn1, fin, l, tm=512, tf=512, final_norm=False)
        xp, *st = _pmixer(xp, p, l, nb=bp, t=tp, tt=512, cl=ML_BLOCK)
        xp = _xattn(xp, xa_g, xa_q, pk, pv, xa_o, l, nb=bp, t=tp, tt=1024, split=4)
        for acc, v in zip(p_out, st):
            acc.append(v)
        znat, zxy = _zproj(xs, p, l)
        xs, *st = _smixer(znat, zxy, xs, s_state, p, l, nb=bs, t=ts, ns=4)
        xs = _xattn(xs, xa_g, xa_q, sk, sv, xa_o, l, nb=bs, t=ts, tt=ts, ns=4)
        for acc, v in zip(s_out, st):
            acc.append(v)
        xp, xs = _ffn(xp, xs, *ffn2, fin, l, tm=512, tf=512, final_norm=last)

    def states(acc):
        c, n, m, h, cv = (jnp.stack(a) for a in acc)
        return c, n, m[:, :, 0, :ML_HEADS], h[:, :, 0, :], cv[:, :, SUBLANES - (CONV_W - 1):, :]

    return (xp.reshape(bp, tp, D_MODEL), xs.reshape(bs, ts, D_MODEL),
            *states(p_out),
            pk.reshape(DEPTH, bp, MEM_LEN, XA_HEADS, XA_DH), pv.reshape(DEPTH, bp, MEM_LEN, XA_HEADS, XA_DH),
            *states(s_out))
```

```python
import functools

import jax
import jax.numpy as jnp
from jax import lax
from jax.experimental import pallas as pl
from jax.experimental.pallas import tpu as pltpu

F32 = jnp.float32
BF16 = jnp.bfloat16

D_MODEL = 1024
DEPTH = 2
ML_BLOCK = 128
ML_HEADS = 4
ML_DV = 256
ML_DK = 128
ML_WIDTH = ML_HEADS * ML_DV
LRU_WIDTH = 1024
LRU_BLOCKS = 8
LRU_BW = LRU_WIDTH // LRU_BLOCKS
CONV_W = 4
LRU_C = 8.0
MEM_LEN = 256
XA_HEADS = 4
XA_DH = D_MODEL // XA_HEADS
D_FF = 4 * D_MODEL
EPS = 1e-6

LANES = 128
SUBLANES = 8
VMEM_LIMIT = 56 * 1024 * 1024

Z_Q = 0
Z_K = Z_Q + ML_HEADS * ML_DK
Z_V = Z_K + ML_HEADS * ML_DK
Z_O = Z_V + ML_WIDTH
Z_G = Z_O + ML_WIDTH
Z_NAT = Z_G + LANES
Z_CHUNK = 768
Z_XY = 2 * LRU_WIDTH
W_X = Z_G + 2 * ML_HEADS


def _rms(x, g):
    return x * lax.rsqrt(jnp.mean(x * x, axis=-1, keepdims=True) + EPS) * g


def _nt_dot(a, b):
    return lax.dot_general(a, b, (((1,), (1,)), ((), ())), preferred_element_type=F32)


def _tn_dot(a, b):
    return lax.dot_general(a, b, (((0,), (0,)), ((), ())), preferred_element_type=F32)


def _dot(a, b):
    return jnp.dot(a, b, preferred_element_type=F32)


def _params(*sem):
    return pltpu.CompilerParams(dimension_semantics=sem, vmem_limit_bytes=VMEM_LIMIT)


def _resident(shape, index_map):
    return pl.BlockSpec(shape, index_map, pipeline_mode=pl.Buffered(1))


def _ffn_body(parts, g_ref, wg_ref, wu_ref, wd_ref, fg_ref, h_s, *, tf, final_norm):
    xs = [x_ref[...] for x_ref, _ in parts]
    rows = sum(x.shape[0] for x in xs)
    xn = jnp.concatenate([_rms(x, g_ref[...]).astype(BF16) for x in xs], axis=0)
    for j in range(D_FF // tf):
        cols = slice(j * tf, (j + 1) * tf)
        g = _dot(xn, wg_ref[:, cols])
        u = _dot(xn, wu_ref[:, cols])
        h_s[0:rows, cols] = (g * jax.nn.sigmoid(g) * u).astype(BF16)
    d = _dot(h_s[0:rows, :], wd_ref[...])
    lo = 0
    for x, (_, o_ref) in zip(xs, parts):
        y = x + 0.5 * d[lo:lo + x.shape[0], :]
        if final_norm:
            y = _rms(y, fg_ref[...])
        o_ref[...] = y
        lo += x.shape[0]


def _ffn_kernel(xp_ref, xs_ref, g_ref, wg_ref, wu_ref, wd_ref, fg_ref, op_ref, os_ref, h_s, **kw):
    last = pl.num_programs(0) - 1

    @pl.when(pl.program_id(0) < last)
    def _():
        _ffn_body([(xp_ref, op_ref)], g_ref, wg_ref, wu_ref, wd_ref, fg_ref, h_s, **kw)

    @pl.when(pl.program_id(0) == last)
    def _():
        _ffn_body([(xp_ref, op_ref), (xs_ref, os_ref)], g_ref, wg_ref, wu_ref, wd_ref, fg_ref, h_s, **kw)


def _ffn(xp, xs, g, wg, wu, wd, fg, l, *, tm, tf, final_norm):
    np_tiles = xp.shape[0] // tm
    ms = xs.shape[0]
    ptile = lambda i: (i, 0)
    whole = lambda i: (0, 0)
    lay = lambda i: (l, 0, 0)
    return pl.pallas_call(
        functools.partial(_ffn_kernel, tf=tf, final_norm=final_norm),
        out_shape=(jax.ShapeDtypeStruct(xp.shape, F32), jax.ShapeDtypeStruct(xs.shape, F32)),
        grid=(np_tiles,),
        in_specs=[
            pl.BlockSpec((tm, D_MODEL), ptile),
            _resident((ms, D_MODEL), whole),
            _resident((None, 1, D_MODEL), lay),
            _resident((None, D_MODEL, D_FF), lay),
            _resident((None, D_MODEL, D_FF), lay),
            _resident((None, D_FF, D_MODEL), lay),
            _resident((1, D_MODEL), whole),
        ],
        out_specs=(pl.BlockSpec((tm, D_MODEL), ptile), pl.BlockSpec((ms, D_MODEL), whole)),
        scratch_shapes=[pltpu.VMEM((tm + ms, D_FF), BF16)],
        compiler_params=_params("arbitrary"),
        name="ffn",
    )(xp, xs, g, wg, wu, wd, fg)


def _log_sigmoid(x):
    return jnp.minimum(x, 0.0) - jnp.log1p(jnp.exp(-jnp.abs(x)))


def _softplus(x):
    return jnp.maximum(x, 0.0) + jnp.log1p(jnp.exp(-jnp.abs(x)))


def _lane_pick(x, lane_ids, idx):
    return jnp.sum(jnp.where(lane_ids == idx, x, 0.0), axis=1, keepdims=True)


def _mlstm_blocks(z_ref, gbias_ref, onorm_ref, c_ref, n_ref, m_ref, hm_s, rows, cl, base=0):
    lane1 = lax.broadcasted_iota(jnp.int32, (1, LANES), 1)
    m_vec = m_ref[0]
    m_heads = [_lane_pick(m_vec, lane1, h) for h in range(ML_HEADS)]
    for c in range(rows // cl):
        m_heads = _mlstm_chunk(z_ref, gbias_ref, onorm_ref, c_ref, n_ref, m_heads, hm_s, base + c * cl, cl)
    for h in range(ML_HEADS):
        m_vec = jnp.where(lane1 == h, m_heads[h], m_vec)
    m_ref[0] = m_vec


def _mlstm_chunk(z_ref, gbias_ref, onorm_ref, c_ref, n_ref, m_heads, hm_s, r0, cl):
    lane = lax.broadcasted_iota(jnp.int32, (cl, LANES), 1)
    row = lax.broadcasted_iota(jnp.int32, (cl, LANES), 0)
    tril = (lax.broadcasted_iota(jnp.int32, (cl, cl), 1)
            <= lax.broadcasted_iota(jnp.int32, (cl, cl), 0))

    gates = z_ref[r0:r0 + cl, Z_G:Z_G + LANES] + gbias_ref[...]
    bsum = _log_sigmoid(gates)
    d = 1
    while d < cl:
        bsum = bsum + jnp.where(row >= d, pltpu.roll(bsum, d, axis=0), 0.0)
        d *= 2
    mixed = jnp.where(lane < ML_HEADS, gates, bsum)
    if cl < LANES:
        mixed = jnp.concatenate([mixed, jnp.zeros((LANES - cl, LANES), F32)], axis=0)
    mixed_t = mixed.T

    m_next = []
    for h in range(ML_HEADS):
        b_col = _lane_pick(bsum, lane, ML_HEADS + h)
        ig_col = _lane_pick(gates, lane, h)
        b_row = mixed_t[ML_HEADS + h:ML_HEADS + h + 1, 0:cl]
        ig_row = mixed_t[h:h + 1, 0:cl]
        m_prev = m_heads[h]

        dmat = jnp.where(tril, b_col - b_row + ig_row, -jnp.inf)
        inter = b_col + m_prev
        m_t = jnp.maximum(inter, jnp.max(dmat, axis=1, keepdims=True))

        qf = z_ref[r0:r0 + cl, Z_Q + h * ML_DK:Z_Q + (h + 1) * ML_DK] * (ML_DK ** -0.5)
        kf = z_ref[r0:r0 + cl, Z_K + h * ML_DK:Z_K + (h + 1) * ML_DK]
        vf = z_ref[r0:r0 + cl, Z_V + h * ML_DV:Z_V + (h + 1) * ML_DV]
        q = qf.astype(BF16)
        k = kf.astype(BF16)

        s = _nt_dot(q, k) * jnp.exp(dmat - m_t)
        w_inter = jnp.exp(inter - m_t)
        c_old = c_ref[0, h]
        n_old = n_ref[0, h:h + 1, :]
        num = w_inter * _nt_dot(q, c_old.astype(BF16)) + _dot(s.astype(BF16), vf.astype(BF16))
        den = (w_inter * jnp.sum(qf * n_old, axis=1, keepdims=True)
               + jnp.sum(s, axis=1, keepdims=True))
        hh = num / jnp.maximum(jnp.abs(den), jnp.exp(-m_t))

        hh = hh * lax.rsqrt(jnp.mean(hh * hh, axis=1, keepdims=True) + EPS) * onorm_ref[h:h + 1, :]
        o_gate = jax.nn.sigmoid(z_ref[r0:r0 + cl, Z_O + h * ML_DV:Z_O + (h + 1) * ML_DV])
        hm_s[r0:r0 + cl, h * ML_DV:(h + 1) * ML_DV] = (o_gate * hh).astype(BF16)

        m_new = m_t[cl - 1:cl, :]
        b_last = b_col[cl - 1:cl, :]
        wgt = jnp.exp(b_last - b_col + ig_col - m_new)
        decay = jnp.exp(b_last + m_prev - m_new)
        c_ref[0, h] = decay * c_old + _tn_dot((wgt * vf).astype(BF16), k)
        n_ref[0, h:h + 1, :] = decay * n_old + jnp.sum(wgt * kf, axis=0, keepdims=True)
        m_next.append(m_new)
    return m_next


def _lru_gates(conv, wai_ref, ba_ref, bi_ref, lam_ref):
    conv_b = conv.astype(BF16)
    pre = [_dot(conv_b[:, n * LRU_BW:(n + 1) * LRU_BW], wai_ref[n])
           for n in range(LRU_BLOCKS)]
    r_gate = jax.nn.sigmoid(jnp.concatenate([p[:, :LRU_BW] for p in pre], axis=1) + ba_ref[...])
    i_gate = jax.nn.sigmoid(jnp.concatenate([p[:, LRU_BW:] for p in pre], axis=1) + bi_ref[...])
    log_a = r_gate * (-LRU_C * _softplus(-lam_ref[...]))
    a = jnp.exp(log_a)
    w = -jnp.tanh(log_a) * (a * a + 1.0)
    xin = jnp.where(w > 0.0, w * lax.rsqrt(w), 0.0) * (i_gate * conv)
    return a, xin


def _pmixer_project(x_ref, gmix_ref, wxy_ref, zxy_dst, pbuf, xnb_s, *, tt):
    seg = tt // SUBLANES
    pitch = seg + SUBLANES
    xn = _rms(x_ref[...], gmix_ref[...])
    xnb_s[...] = xn.astype(BF16)
    for k in range(LRU_WIDTH // LANES):
        for s in range(SUBLANES):
            pbuf[k, pl.ds(s * pitch, seg), :] = xn[s * seg:(s + 1) * seg, k * LANES:(k + 1) * LANES]
    xnp = jnp.stack(
        [jnp.concatenate([pbuf[k, pl.ds(j, SUBLANES, stride=pitch), :]
                          for k in range(LRU_WIDTH // LANES)], axis=1) for j in range(seg)], axis=0)
    zxy_dst[...] = _dot(xnp.reshape(tt, D_MODEL).astype(BF16), wxy_ref[...])


def _pmixer_mix(x_ref, xnb_s, wnat_ref, z_s, zxy_s, gbias_ref, onorm_ref, convw_ref, convb_ref, wai_ref, ba_ref, bi_ref,
                lam_ref, wout_ref, y_ref, c_ref, n_ref, m_ref, h_ref, cv_ref, obuf, hm_s, *, cl, tt, zc):
    seg = tt // SUBLANES
    pitch = seg + SUBLANES

    u3 = zxy_s[:, 0:LRU_WIDTH].reshape(seg, SUBLANES, LRU_WIDTH)
    tail = cv_ref[0]
    sub = lax.broadcasted_iota(jnp.int32, (SUBLANES, LRU_WIDTH), 0)
    wrap = []
    for i in range(CONV_W - 1):
        prev = pltpu.roll(u3[seg - (CONV_W - 1) + i], 1, axis=0)
        fill = tail[SUBLANES - (CONV_W - 1) + i:SUBLANES - (CONV_W - 1) + i + 1, :]
        wrap.append(jnp.where(sub == 0, fill, prev))
        cv_ref[0, SUBLANES - (CONV_W - 1) + i:SUBLANES - (CONV_W - 1) + i + 1, :] = (
            u3[seg - (CONV_W - 1) + i][SUBLANES - 1:SUBLANES, :])
    ext = jnp.concatenate([jnp.stack(wrap, axis=0), u3], axis=0)
    conv3 = convb_ref[...] + ext[CONV_W - 1:] * convw_ref[CONV_W - 1:CONV_W, :]
    for j in range(1, CONV_W):
        conv3 = conv3 + ext[CONV_W - 1 - j:CONV_W - 1 - j + seg] * convw_ref[CONV_W - 1 - j:CONV_W - j, :]

    conv = conv3.reshape(tt, LRU_WIDTH)
    bounds = [min(k * zc, Z_NAT) for k in range(-(-Z_NAT // zc) + 1)]
    zcols = list(zip(bounds[:-1], bounds[1:]))

    def project(k):
        lo, hi = zcols[k]
        z_s[:, lo:hi] = _dot(xnb_s[...], wnat_ref[:, lo:hi])

    project(0)
    nrow = len(zcols) - 1
    step = tt // nrow
    a_parts, x_parts = [], []
    for c in range(nrow):
        r = slice(c * step, (c + 1) * step)
        a_c, x_c = _lru_gates(conv[r], wai_ref, ba_ref, bi_ref, lam_ref)
        a_parts.append(a_c)
        x_parts.append(x_c)
        project(c + 1)
    a = jnp.concatenate(a_parts, axis=0)
    xin = jnp.concatenate(x_parts, axis=0)

    a3 = a.reshape(seg, SUBLANES, LRU_WIDTH)
    x3 = xin.reshape(seg, SUBLANES, LRU_WIDTH)
    hs = [x3[0]]
    ps = [a3[0]]
    for j in range(1, seg):
        hs.append(a3[j] * hs[-1] + x3[j])
        ps.append(a3[j] * ps[-1])
    carry = h_ref[0]
    cin = []
    for s in range(SUBLANES):
        cin.append(carry)
        carry = hs[-1][s:s + 1, :] + ps[-1][s:s + 1, :] * carry
    h_ref[0] = carry
    cin = jnp.concatenate(cin, axis=0)
    h3 = jnp.stack([hs[j] + ps[j] * cin for j in range(seg)], axis=0)

    hl = (jax.nn.gelu(zxy_s[:, LRU_WIDTH:]) * h3.reshape(tt, LRU_WIDTH)).astype(BF16)
    ol3 = _dot(hl, wout_ref[ML_WIDTH:ML_WIDTH + LRU_WIDTH, :]).reshape(seg, SUBLANES, D_MODEL)
    for j in range(seg):
        for k in range(D_MODEL // LANES):
            obuf[k, pl.ds(j, SUBLANES, stride=pitch), :] = ol3[j][:, k * LANES:(k + 1) * LANES]
    out_lru = jnp.concatenate(
        [jnp.concatenate([obuf[k, pl.ds(s * pitch, seg), :] for s in range(SUBLANES)], axis=0)
         for k in range(D_MODEL // LANES)], axis=1)

    _mlstm_blocks(z_s, gbias_ref, onorm_ref, c_ref, n_ref, m_ref, hm_s, tt, cl)

    y_ref[...] = x_ref[...] + _dot(hm_s[...], wout_ref[0:ML_WIDTH, :]) + out_lru


def _pmixer_kernel(x_ref, gmix_ref, wnat_ref, wxy_ref, gbias_ref, onorm_ref, convw_ref,
                   convb_ref, wai_ref, ba_ref, bi_ref, lam_ref, wout_ref,
                   y_ref, c_ref, n_ref, m_ref, h_ref, cv_ref,
                   z_s, zxy_s, pbuf, obuf, hm_s, xnb_s, *, cl, tt, zc):
    @pl.when(pl.program_id(1) == 0)
    def _():
        c_ref[...] = jnp.zeros_like(c_ref)
        n_ref[...] = jnp.zeros_like(n_ref)
        m_ref[...] = jnp.zeros_like(m_ref)
        h_ref[...] = jnp.zeros_like(h_ref)
        cv_ref[...] = jnp.zeros_like(cv_ref)

    _pmixer_project(x_ref, gmix_ref, wxy_ref, zxy_s, pbuf, xnb_s, tt=tt)
    _pmixer_mix(x_ref, xnb_s, wnat_ref, z_s, zxy_s, gbias_ref, onorm_ref, convw_ref, convb_ref, wai_ref, ba_ref,
                bi_ref, lam_ref, wout_ref, y_ref, c_ref, n_ref, m_ref, h_ref, cv_ref, obuf, hm_s,
                cl=cl, tt=tt, zc=zc)


def _pmixer(x, p, l, *, nb, t, tt, cl, zc=Z_CHUNK):
    nt = t // tt
    pitch = tt // SUBLANES + SUBLANES
    tok = lambda b, i: (b * nt + i, 0)
    st4 = lambda b, i: (b, 0, 0, 0)
    st3 = lambda b, i: (b, 0, 0)
    lay3 = lambda b, i: (l, 0, 0)
    lay4 = lambda b, i: (l, 0, 0, 0)
    return pl.pallas_call(
        functools.partial(_pmixer_kernel, cl=cl, tt=tt, zc=zc),
        out_shape=(
            jax.ShapeDtypeStruct((nb * t, D_MODEL), F32),
            jax.ShapeDtypeStruct((nb, ML_HEADS, ML_DV, ML_DK), F32),
            jax.ShapeDtypeStruct((nb, ML_HEADS, ML_DK), F32),
            jax.ShapeDtypeStruct((nb, 1, LANES), F32),
            jax.ShapeDtypeStruct((nb, 1, LRU_WIDTH), F32),
            jax.ShapeDtypeStruct((nb, SUBLANES, LRU_WIDTH), F32),
        ),
        grid=(nb, nt),
        in_specs=[
            pl.BlockSpec((tt, D_MODEL), tok),
            _resident((None, 1, D_MODEL), lay3),
            _resident((None, D_MODEL, Z_NAT), lay3),
            _resident((None, D_MODEL, Z_XY), lay3),
            _resident((None, 1, LANES), lay3),
            _resident((None, ML_HEADS, ML_DV), lay3),
            _resident((None, CONV_W, LRU_WIDTH), lay3),
            _resident((None, 1, LRU_WIDTH), lay3),
            _resident((None, LRU_BLOCKS, LRU_BW, 2 * LRU_BW), lay4),
            _resident((None, 1, LRU_WIDTH), lay3),
            _resident((None, 1, LRU_WIDTH), lay3),
            _resident((None, 1, LRU_WIDTH), lay3),
            _resident((None, ML_WIDTH + LRU_WIDTH, D_MODEL), lay3),
        ],
        out_specs=(
            pl.BlockSpec((tt, D_MODEL), tok),
            pl.BlockSpec((1, ML_HEADS, ML_DV, ML_DK), st4),
            pl.BlockSpec((1, ML_HEADS, ML_DK), st3),
            pl.BlockSpec((1, 1, LANES), st3),
            pl.BlockSpec((1, 1, LRU_WIDTH), st3),
            pl.BlockSpec((1, SUBLANES, LRU_WIDTH), st3),
        ),
        scratch_shapes=[pltpu.VMEM((tt, Z_NAT), F32), pltpu.VMEM((tt, Z_XY), F32),
                        pltpu.VMEM((LRU_WIDTH // LANES, SUBLANES * pitch, LANES), F32),
                        pltpu.VMEM((D_MODEL // LANES, SUBLANES * pitch, LANES), F32),
                        pltpu.VMEM((tt, ML_WIDTH), BF16), pltpu.VMEM((tt, D_MODEL), BF16)],
        compiler_params=_params("parallel", "arbitrary"),
        name="pmixer",
    )(x, p["mix_norm"], p["w_in"], p["w_xy"], p["gbias"], p["ml_out_norm"], p["lru_conv_w"],
      p["lru_conv_b"], p["w_ai"], p["lru_b_a"], p["lru_b_i"], p["lru_lambda"], p["w_out"])


def _zproj_kernel(x_ref, g_ref, wnat_ref, wxy_ref, znat_ref, zxy_ref):
    xn = _rms(x_ref[...], g_ref[...]).astype(BF16)
    znat_ref[...] = _dot(xn, wnat_ref[...])
    zxy_ref[...] = _dot(xn, wxy_ref[...])


def _zproj(x, p, l):
    m = x.shape[0]
    lay3 = lambda i: (l, 0, 0)
    return pl.pallas_call(
        _zproj_kernel,
        out_shape=(jax.ShapeDtypeStruct((m, Z_NAT), F32), jax.ShapeDtypeStruct((m, Z_XY), F32)),
        grid=(1,),
        in_specs=[
            pl.BlockSpec((m, D_MODEL), lambda i: (0, 0)),
            pl.BlockSpec((None, 1, D_MODEL), lay3),
            pl.BlockSpec((None, D_MODEL, Z_NAT), lay3),
            pl.BlockSpec((None, D_MODEL, Z_XY), lay3),
        ],
        out_specs=(pl.BlockSpec((m, Z_NAT), lambda i: (0, 0)), pl.BlockSpec((m, Z_XY), lambda i: (0, 0))),
        compiler_params=_params("arbitrary"),
        name="zproj",
    )(x, p["mix_norm"], p["w_in"], p["w_xy"])


def _smixer_kernel(znat_ref, zxy_ref, x_ref, c0_ref, n0_ref, m0_ref, h0_ref, cv0_ref,
                   gbias_ref, onorm_ref, convw_ref, convb_ref, wai_ref, ba_ref, bi_ref, lam_ref,
                   wout_ref,
                   y_ref, c_ref, n_ref, m_ref, h_ref, cv_ref,
                   ubuf_s, hm_s, *, ns, t):
    c_ref[...] = c0_ref[...]
    n_ref[...] = n0_ref[...]
    m_ref[...] = m0_ref[...]
    pad = t + SUBLANES

    convs = []
    for s in range(ns):
        one = pl.ds(s, 1)
        _mlstm_blocks(znat_ref, gbias_ref, onorm_ref, c_ref.at[one], n_ref.at[one], m_ref.at[one], hm_s,
                      t, t, base=s * t)
        u = zxy_ref[s * t:(s + 1) * t, 0:LRU_WIDTH]
        ubuf_s[s * pad:s * pad + SUBLANES, :] = cv0_ref[s]
        ubuf_s[s * pad + SUBLANES:(s + 1) * pad, :] = u
        cv_ref[s] = ubuf_s[s * pad + t:(s + 1) * pad, :]
        conv = convb_ref[...] + u * convw_ref[CONV_W - 1:CONV_W, :]
        for j in range(1, CONV_W):
            conv = conv + (ubuf_s[pl.ds(s * pad + SUBLANES - j, t), :]
                           * convw_ref[CONV_W - 1 - j:CONV_W - j, :])
        convs.append(conv)
    a, xin = _lru_gates(jnp.concatenate(convs, axis=0), wai_ref, ba_ref, bi_ref, lam_ref)

    rmod = lax.broadcasted_iota(jnp.int32, (ns * t, LRU_WIDTH), 0) & (SUBLANES - 1)
    d = 1
    while d < SUBLANES:
        keep = rmod >= d
        xin = xin + a * jnp.where(keep, pltpu.roll(xin, d, axis=0), 0.0)
        a = a * jnp.where(keep, pltpu.roll(a, d, axis=0), 1.0)
        d *= 2
    groups = []
    for s in range(ns):
        carry = h0_ref[s]
        for gidx in range(t // SUBLANES):
            lo = s * t + gidx * SUBLANES
            hb = xin[lo:lo + SUBLANES, :] + a[lo:lo + SUBLANES, :] * carry
            carry = hb[SUBLANES - 1:SUBLANES, :]
            groups.append(hb)
        h_ref[s] = carry
    h_lru = jnp.concatenate(groups, axis=0)

    hl = (jax.nn.gelu(zxy_ref[:, LRU_WIDTH:]) * h_lru).astype(BF16)
    out = (_dot(hm_s[...], wout_ref[0:ML_WIDTH, :])
           + _dot(hl, wout_ref[ML_WIDTH:ML_WIDTH + LRU_WIDTH, :]))
    y_ref[...] = x_ref[...] + out


def _smixer(znat, zxy, x, st, p, l, *, nb, t, ns):
    tok = lambda b: (b, 0)
    lay3 = lambda b: (l, 0, 0)
    lay4 = lambda b: (l, 0, 0, 0)
    st3 = lambda b: (b, 0, 0)
    lst = lambda b: (l, b, 0, 0)
    c0, n0, m0, h0, cv0 = st
    return pl.pallas_call(
        functools.partial(_smixer_kernel, ns=ns, t=t),
        out_shape=(
            jax.ShapeDtypeStruct((nb * t, D_MODEL), F32),
            jax.ShapeDtypeStruct((nb, ML_HEADS, ML_DV, ML_DK), F32),
            jax.ShapeDtypeStruct((nb, ML_HEADS, ML_DK), F32),
            jax.ShapeDtypeStruct((nb, 1, LANES), F32),
            jax.ShapeDtypeStruct((nb, 1, LRU_WIDTH), F32),
            jax.ShapeDtypeStruct((nb, SUBLANES, LRU_WIDTH), F32),
        ),
        grid=(nb // ns,),
        in_specs=[
            pl.BlockSpec((ns * t, Z_NAT), tok),
            pl.BlockSpec((ns * t, Z_XY), tok),
            pl.BlockSpec((ns * t, D_MODEL), tok),
            pl.BlockSpec((None, ns, ML_HEADS, ML_DV, ML_DK), lambda b: (l, b, 0, 0, 0)),
            pl.BlockSpec((None, ns, ML_HEADS, ML_DK), lst),
            pl.BlockSpec((None, ns, 1, LANES), lst),
            pl.BlockSpec((None, ns, 1, LRU_WIDTH), lst),
            pl.BlockSpec((None, ns, SUBLANES, LRU_WIDTH), lst),
            pl.BlockSpec((None, 1, LANES), lay3),
            pl.BlockSpec((None, ML_HEADS, ML_DV), lay3),
            pl.BlockSpec((None, CONV_W, LRU_WIDTH), lay3),
            pl.BlockSpec((None, 1, LRU_WIDTH), lay3),
            pl.BlockSpec((None, LRU_BLOCKS, LRU_BW, 2 * LRU_BW), lay4),
            pl.BlockSpec((None, 1, LRU_WIDTH), lay3),
            pl.BlockSpec((None, 1, LRU_WIDTH), lay3),
            pl.BlockSpec((None, 1, LRU_WIDTH), lay3),
            pl.BlockSpec((None, ML_WIDTH + LRU_WIDTH, D_MODEL), lay3),
        ],
        out_specs=(
            pl.BlockSpec((ns * t, D_MODEL), tok),
            pl.BlockSpec((ns, ML_HEADS, ML_DV, ML_DK), lambda b: (b, 0, 0, 0)),
            pl.BlockSpec((ns, ML_HEADS, ML_DK), st3),
            pl.BlockSpec((ns, 1, LANES), st3),
            pl.BlockSpec((ns, 1, LRU_WIDTH), st3),
            pl.BlockSpec((ns, SUBLANES, LRU_WIDTH), st3),
        ),
        scratch_shapes=[pltpu.VMEM((ns * (t + SUBLANES), LRU_WIDTH), F32),
                        pltpu.VMEM((ns * t, ML_WIDTH), BF16)],
        compiler_params=_params("parallel"),
        name="smixer",
    )(znat, zxy, x, c0, n0, m0, h0, cv0, p["gbias"], p["ml_out_norm"], p["lru_conv_w"],
      p["lru_conv_b"], p["w_ai"], p["lru_b_a"], p["lru_b_i"], p["lru_lambda"], p["w_out"])


def _xattn_kernel(x_ref, g_ref, wq_ref, mk_ref, mv_ref, wo_ref, y_ref, o_s, *, ns, rows, split):
    step = rows // split
    chunks = [(s, slice(s * rows + c * step, s * rows + (c + 1) * step)) for s in range(ns) for c in range(split)]
    heads = [slice(h * XA_DH, (h + 1) * XA_DH) for h in range(XA_HEADS)]

    def scores(s, r):
        q = _dot(_rms(x_ref[r, :], g_ref[...]).astype(BF16), wq_ref[...])
        return [_nt_dot(q[:, c].astype(BF16), mk_ref[s, :, c].astype(BF16)) * (XA_DH ** -0.5) for c in heads]

    def attend(s, r, sc):
        for c, sch in zip(heads, sc):
            p = jnp.exp(sch - jnp.max(sch, axis=1, keepdims=True))
            p = p / jnp.sum(p, axis=1, keepdims=True)
            o_s[r, c] = _dot(p.astype(BF16), mv_ref[s, :, c].astype(BF16)).astype(BF16)

    def project(r):
        y_ref[r, :] = x_ref[r, :] + _dot(o_s[r, :], wo_ref[...])

    pending = None
    for s, r in chunks:
        sc = scores(s, r)
        if pending is not None:
            ps, pr, psc = pending
            attend(ps, pr, psc)
            project(pr)
        pending = (s, r, sc)
    ps, pr, psc = pending
    attend(ps, pr, psc)
    project(pr)


def _xattn(x, g, wq, mk, mv, wo, l, *, nb, t, tt, ns=1, split=1):
    nt = t // tt
    assert ns == 1 or nt == 1
    tok = lambda b, i: (b * nt + i, 0)
    lay3 = lambda b, i: (l, 0, 0)
    mem = lambda b, i: (l, b, 0, 0)
    return pl.pallas_call(
        functools.partial(_xattn_kernel, ns=ns, rows=tt, split=split),
        out_shape=jax.ShapeDtypeStruct((nb * t, D_MODEL), F32),
        grid=(nb // ns, nt),
        in_specs=[
            pl.BlockSpec((ns * tt, D_MODEL), tok),
            pl.BlockSpec((None, 1, D_MODEL), lay3),
            pl.BlockSpec((None, D_MODEL, D_MODEL), lay3),
            pl.BlockSpec((None, ns, MEM_LEN, D_MODEL), mem),
            pl.BlockSpec((None, ns, MEM_LEN, D_MODEL), mem),
            pl.BlockSpec((None, D_MODEL, D_MODEL), lay3),
        ],
        out_specs=pl.BlockSpec((ns * tt, D_MODEL), tok),
        scratch_shapes=[pltpu.VMEM((ns * tt, D_MODEL), BF16)],
        compiler_params=_params("parallel", "arbitrary"),
        name="xattn",
    )(x, g, wq, mk, mv, wo)


def _memkv_kernel(mem_ref, g_ref, wk_ref, wv_ref, k_ref, v_ref):
    mn = _rms(mem_ref[...], g_ref[...]).astype(BF16)
    k_ref[...] = _dot(mn, wk_ref[...])
    v_ref[...] = _dot(mn, wv_ref[...])


def _memkv(mem, g, wk, wv):
    nb = mem.shape[0]
    wspec = pl.BlockSpec((None, D_MODEL, D_MODEL), lambda l, b: (l, 0, 0))
    ospec = pl.BlockSpec((None, None, MEM_LEN, D_MODEL), lambda l, b: (l, b, 0, 0))
    out = jax.ShapeDtypeStruct((DEPTH, nb, MEM_LEN, D_MODEL), F32)
    return pl.pallas_call(
        _memkv_kernel,
        out_shape=(out, out),
        grid=(DEPTH, nb),
        in_specs=[pl.BlockSpec((None, MEM_LEN, D_MODEL), lambda l, b: (b, 0, 0)),
                  pl.BlockSpec((None, 1, D_MODEL), lambda l, b: (l, 0, 0)), wspec, wspec],
        out_specs=(ospec, ospec),
        compiler_params=_params("parallel", "parallel"),
        name="memkv",
    )(mem, g, wk, wv)


def kernel(x_prompt, x_sample, mem_prompt, state_mlstm_C, state_mlstm_n, state_mlstm_m, state_lru_h,
           state_lru_conv, cache_mem_k, cache_mem_v, ffn1_norm, ffn1_w_gate, ffn1_w_up, ffn1_w_down,
           mix_norm, w_in, ml_b_i, ml_b_f, ml_out_norm, lru_conv_w, lru_conv_b, lru_w_a, lru_b_a,
           lru_w_i, lru_b_i, lru_lambda, w_out, xattn_norm, mem_norm, xattn_w_q, xattn_w_k, xattn_w_v,
           xattn_w_o, ffn2_norm, ffn2_w_gate, ffn2_w_up, ffn2_w_down, final_norm):
    bp, tp, _ = x_prompt.shape
    bs, ts, _ = x_sample.shape
    xp = x_prompt.reshape(bp * tp, D_MODEL)
    xs = x_sample.reshape(bs * ts, D_MODEL)

    bf = lambda w: w.astype(BF16)
    row = lambda v: v.astype(F32).reshape(DEPTH, 1, -1)
    gate_pad = ((0, 0), (0, 0), (0, LANES - 2 * ML_HEADS))
    w_in_b = bf(w_in)
    p = {
        "mix_norm": row(mix_norm),
        "w_in": w_in_b,
        "w_xy": w_in_b[:, :, W_X:],
        "gbias": jnp.pad(jnp.concatenate([ml_b_i, ml_b_f], axis=1).astype(F32)[:, None, :], gate_pad),
        "ml_out_norm": ml_out_norm.astype(F32),
        "lru_conv_w": lru_conv_w, "lru_conv_b": row(lru_conv_b),
        "w_ai": bf(jnp.concatenate([lru_w_a, lru_w_i], axis=-1)),
        "lru_b_a": row(lru_b_a), "lru_b_i": row(lru_b_i), "lru_lambda": row(lru_lambda),
        "w_out": bf(w_out),
    }
    ffn1 = (row(ffn1_norm), bf(ffn1_w_gate), bf(ffn1_w_up), bf(ffn1_w_down))
    ffn2 = (row(ffn2_norm), bf(ffn2_w_gate), bf(ffn2_w_up), bf(ffn2_w_down))
    xa_g, xa_q, xa_o = row(xattn_norm), bf(xattn_w_q), bf(xattn_w_o)
    fin = final_norm.astype(F32).reshape(1, D_MODEL)

    pk, pv = _memkv(mem_prompt, row(mem_norm), bf(xattn_w_k), bf(xattn_w_v))
    sk = cache_mem_k.reshape(DEPTH, bs, MEM_LEN, D_MODEL)
    sv = cache_mem_v.reshape(DEPTH, bs, MEM_LEN, D_MODEL)
    s_state = (
        state_mlstm_C.astype(F32), state_mlstm_n.astype(F32),
        jnp.pad(state_mlstm_m.astype(F32)[:, :, None, :], ((0, 0), (0, 0), (0, 0), (0, LANES - ML_HEADS))),
        state_lru_h.astype(F32)[:, :, None, :],
        jnp.pad(state_lru_conv.astype(F32), ((0, 0), (0, 0), (SUBLANES - (CONV_W - 1), 0), (0, 0))),
    )

    p_out = [[] for _ in range(5)]
    s_out = [[] for _ in range(5)]
    for l in range(DEPTH):
        last = l == DEPTH - 1
        xp, xs = _ffn(xp, xs, *ffn1, fin, l, tm=512, tf=256, final_norm=False)
        xp, *st = _pmixer(xp, p, l, nb=bp, t=tp, tt=512, cl=ML_BLOCK, zc=768 if l == 0 else 640)
        xp = _xattn(xp, xa_g, xa_q, pk, pv, xa_o, l, nb=bp, t=tp, tt=1024, split=4 if l == 0 else 2)
        for acc, v in zip(p_out, st):
            acc.append(v)
        znat, zxy = _zproj(xs, p, l)
        xs, *st = _smixer(znat, zxy, xs, s_state, p, l, nb=bs, t=ts, ns=4)
        xs = _xattn(xs, xa_g, xa_q, sk, sv, xa_o, l, nb=bs, t=ts, tt=ts, ns=4)
        for acc, v in zip(s_out, st):
            acc.append(v)
        xp, xs = _ffn(xp, xs, *ffn2, fin, l, tm=512, tf=512, final_norm=last)

    def states(acc):
        c, n, m, h, cv = (jnp.stack(a) for a in acc)
        return c, n, m[:, :, 0, :ML_HEADS], h[:, :, 0, :], cv[:, :, SUBLANES - (CONV_W - 1):, :]

    return (xp.reshape(bp, tp, D_MODEL), xs.reshape(bs, ts, D_MODEL),
            *states(p_out),
            pk.reshape(DEPTH, bp, MEM_LEN, XA_HEADS, XA_DH), pv.reshape(DEPTH, bp, MEM_LEN, XA_HEADS, XA_DH),
            *states(s_out))
```

```python
import functools

import jax
import jax.numpy as jnp
from jax import lax
from jax.experimental import pallas as pl
from jax.experimental.pallas import tpu as pltpu

F32 = jnp.float32
BF16 = jnp.bfloat16

D_MODEL = 1024
DEPTH = 2
ML_BLOCK = 128
ML_HEADS = 4
ML_DV = 256
ML_DK = 128
ML_WIDTH = ML_HEADS * ML_DV
LRU_WIDTH = 1024
LRU_BLOCKS = 8
LRU_BW = LRU_WIDTH // LRU_BLOCKS
CONV_W = 4
LRU_C = 8.0
MEM_LEN = 256
XA_HEADS = 4
XA_DH = D_MODEL // XA_HEADS
D_FF = 4 * D_MODEL
EPS = 1e-6

LANES = 128
SUBLANES = 8
VMEM_LIMIT = 56 * 1024 * 1024

Z_Q = 0
Z_K = Z_Q + ML_HEADS * ML_DK
Z_V = Z_K + ML_HEADS * ML_DK
Z_O = Z_V + ML_WIDTH
Z_G = Z_O + ML_WIDTH
Z_NAT = Z_G + LANES
Z_CHUNK = 768
Z_XY = 2 * LRU_WIDTH
W_X = Z_G + 2 * ML_HEADS


def _rms(x, g):
    return x * lax.rsqrt(jnp.mean(x * x, axis=-1, keepdims=True) + EPS) * g


def _nt_dot(a, b):
    return lax.dot_general(a, b, (((1,), (1,)), ((), ())), preferred_element_type=F32)


def _tn_dot(a, b):
    return lax.dot_general(a, b, (((0,), (0,)), ((), ())), preferred_element_type=F32)


def _dot(a, b):
    return jnp.dot(a, b, preferred_element_type=F32)


def _params(*sem):
    return pltpu.CompilerParams(dimension_semantics=sem, vmem_limit_bytes=VMEM_LIMIT)


def _resident(shape, index_map):
    return pl.BlockSpec(shape, index_map, pipeline_mode=pl.Buffered(1))


def _ffn_body(parts, g_ref, wg_ref, wu_ref, wd_ref, fg_ref, h_s, *, tf, final_norm):
    xs = [x_ref[...] for x_ref, _ in parts]
    rows = sum(x.shape[0] for x in xs)
    xn = jnp.concatenate([_rms(x, g_ref[...]).astype(BF16) for x in xs], axis=0)
    for j in range(D_FF // tf):
        cols = slice(j * tf, (j + 1) * tf)
        g = _dot(xn, wg_ref[:, cols])
        u = _dot(xn, wu_ref[:, cols])
        h_s[0:rows, cols] = (g * jax.nn.sigmoid(g) * u).astype(BF16)
    d = _dot(h_s[0:rows, :], wd_ref[...])
    lo = 0
    for x, (_, o_ref) in zip(xs, parts):
        y = x + 0.5 * d[lo:lo + x.shape[0], :]
        if final_norm:
            y = _rms(y, fg_ref[...])
        o_ref[...] = y
        lo += x.shape[0]


def _ffn_kernel(xp_ref, xs_ref, g_ref, wg_ref, wu_ref, wd_ref, fg_ref, op_ref, os_ref, h_s, **kw):
    last = pl.num_programs(0) - 1

    @pl.when(pl.program_id(0) < last)
    def _():
        _ffn_body([(xp_ref, op_ref)], g_ref, wg_ref, wu_ref, wd_ref, fg_ref, h_s, **kw)

    @pl.when(pl.program_id(0) == last)
    def _():
        _ffn_body([(xp_ref, op_ref), (xs_ref, os_ref)], g_ref, wg_ref, wu_ref, wd_ref, fg_ref, h_s, **kw)


def _ffn(xp, xs, g, wg, wu, wd, fg, l, *, tm, tf, final_norm):
    np_tiles = xp.shape[0] // tm
    ms = xs.shape[0]
    ptile = lambda i: (i, 0)
    whole = lambda i: (0, 0)
    lay = lambda i: (l, 0, 0)
    return pl.pallas_call(
        functools.partial(_ffn_kernel, tf=tf, final_norm=final_norm),
        out_shape=(jax.ShapeDtypeStruct(xp.shape, F32), jax.ShapeDtypeStruct(xs.shape, F32)),
        grid=(np_tiles,),
        in_specs=[
            pl.BlockSpec((tm, D_MODEL), ptile),
            _resident((ms, D_MODEL), whole),
            _resident((None, 1, D_MODEL), lay),
            _resident((None, D_MODEL, D_FF), lay),
            _resident((None, D_MODEL, D_FF), lay),
            _resident((None, D_FF, D_MODEL), lay),
            _resident((1, D_MODEL), whole),
        ],
        out_specs=(pl.BlockSpec((tm, D_MODEL), ptile), pl.BlockSpec((ms, D_MODEL), whole)),
        scratch_shapes=[pltpu.VMEM((tm + ms, D_FF), BF16)],
        compiler_params=_params("arbitrary"),
        name="ffn",
    )(xp, xs, g, wg, wu, wd, fg)


def _log_sigmoid(x):
    return jnp.minimum(x, 0.0) - jnp.log1p(jnp.exp(-jnp.abs(x)))


def _softplus(x):
    return jnp.maximum(x, 0.0) + jnp.log1p(jnp.exp(-jnp.abs(x)))


def _lane_pick(x, lane_ids, idx):
    return jnp.sum(jnp.where(lane_ids == idx, x, 0.0), axis=1, keepdims=True)


def _mlstm_blocks(z_ref, gbias_ref, onorm_ref, c_ref, n_ref, m_ref, hm_s, rows, cl, base=0):
    lane1 = lax.broadcasted_iota(jnp.int32, (1, LANES), 1)
    m_vec = m_ref[0]
    m_heads = [_lane_pick(m_vec, lane1, h) for h in range(ML_HEADS)]
    for c in range(rows // cl):
        m_heads = _mlstm_chunk(z_ref, gbias_ref, onorm_ref, c_ref, n_ref, m_heads, hm_s, base + c * cl, cl)
    for h in range(ML_HEADS):
        m_vec = jnp.where(lane1 == h, m_heads[h], m_vec)
    m_ref[0] = m_vec


def _mlstm_chunk(z_ref, gbias_ref, onorm_ref, c_ref, n_ref, m_heads, hm_s, r0, cl):
    lane = lax.broadcasted_iota(jnp.int32, (cl, LANES), 1)
    row = lax.broadcasted_iota(jnp.int32, (cl, LANES), 0)
    tril = (lax.broadcasted_iota(jnp.int32, (cl, cl), 1)
            <= lax.broadcasted_iota(jnp.int32, (cl, cl), 0))

    gates = z_ref[r0:r0 + cl, Z_G:Z_G + LANES] + gbias_ref[...]
    bsum = _log_sigmoid(gates)
    d = 1
    while d < cl:
        bsum = bsum + jnp.where(row >= d, pltpu.roll(bsum, d, axis=0), 0.0)
        d *= 2
    mixed = jnp.where(lane < ML_HEADS, gates, bsum)
    if cl < LANES:
        mixed = jnp.concatenate([mixed, jnp.zeros((LANES - cl, LANES), F32)], axis=0)
    mixed_t = mixed.T

    m_next = []
    for h in range(ML_HEADS):
        b_col = _lane_pick(bsum, lane, ML_HEADS + h)
        ig_col = _lane_pick(gates, lane, h)
        b_row = mixed_t[ML_HEADS + h:ML_HEADS + h + 1, 0:cl]
        ig_row = mixed_t[h:h + 1, 0:cl]
        m_prev = m_heads[h]

        dmat = jnp.where(tril, b_col - b_row + ig_row, -jnp.inf)
        inter = b_col + m_prev
        m_t = jnp.maximum(inter, jnp.max(dmat, axis=1, keepdims=True))

        qf = z_ref[r0:r0 + cl, Z_Q + h * ML_DK:Z_Q + (h + 1) * ML_DK] * (ML_DK ** -0.5)
        kf = z_ref[r0:r0 + cl, Z_K + h * ML_DK:Z_K + (h + 1) * ML_DK]
        vf = z_ref[r0:r0 + cl, Z_V + h * ML_DV:Z_V + (h + 1) * ML_DV]
        q = qf.astype(BF16)
        k = kf.astype(BF16)

        s = _nt_dot(q, k) * jnp.exp(dmat - m_t)
        w_inter = jnp.exp(inter - m_t)
        c_old = c_ref[0, h]
        n_old = n_ref[0, h:h + 1, :]
        num = w_inter * _nt_dot(q, c_old.astype(BF16)) + _dot(s.astype(BF16), vf.astype(BF16))
        den = (w_inter * jnp.sum(qf * n_old, axis=1, keepdims=True)
               + jnp.sum(s, axis=1, keepdims=True))
        hh = num / jnp.maximum(jnp.abs(den), jnp.exp(-m_t))

        hh = hh * lax.rsqrt(jnp.mean(hh * hh, axis=1, keepdims=True) + EPS) * onorm_ref[h:h + 1, :]
        o_gate = jax.nn.sigmoid(z_ref[r0:r0 + cl, Z_O + h * ML_DV:Z_O + (h + 1) * ML_DV])
        hm_s[r0:r0 + cl, h * ML_DV:(h + 1) * ML_DV] = (o_gate * hh).astype(BF16)

        m_new = m_t[cl - 1:cl, :]
        b_last = b_col[cl - 1:cl, :]
        wgt = jnp.exp(b_last - b_col + ig_col - m_new)
        decay = jnp.exp(b_last + m_prev - m_new)
        c_ref[0, h] = decay * c_old + _tn_dot((wgt * vf).astype(BF16), k)
        n_ref[0, h:h + 1, :] = decay * n_old + jnp.sum(wgt * kf, axis=0, keepdims=True)
        m_next.append(m_new)
    return m_next


def _lru_gates(conv, wai_ref, ba_ref, bi_ref, lam_ref):
    conv_b = conv.astype(BF16)
    pre = [_dot(conv_b[:, n * LRU_BW:(n + 1) * LRU_BW], wai_ref[n])
           for n in range(LRU_BLOCKS)]
    r_gate = jax.nn.sigmoid(jnp.concatenate([p[:, :LRU_BW] for p in pre], axis=1) + ba_ref[...])
    i_gate = jax.nn.sigmoid(jnp.concatenate([p[:, LRU_BW:] for p in pre], axis=1) + bi_ref[...])
    log_a = r_gate * (-LRU_C * _softplus(-lam_ref[...]))
    a = jnp.exp(log_a)
    w = -jnp.tanh(log_a) * (a * a + 1.0)
    xin = jnp.where(w > 0.0, w * lax.rsqrt(w), 0.0) * (i_gate * conv)
    return a, xin


def _pmixer_project(x_ref, gmix_ref, wxy_ref, zxy_dst, pbuf, xnb_s, *, tt):
    seg = tt // SUBLANES
    pitch = seg + SUBLANES
    xn = _rms(x_ref[...], gmix_ref[...])
    xnb_s[...] = xn.astype(BF16)
    for k in range(LRU_WIDTH // LANES):
        for s in range(SUBLANES):
            pbuf[k, pl.ds(s * pitch, seg), :] = xn[s * seg:(s + 1) * seg, k * LANES:(k + 1) * LANES]
    xnp = jnp.stack(
        [jnp.concatenate([pbuf[k, pl.ds(j, SUBLANES, stride=pitch), :]
                          for k in range(LRU_WIDTH // LANES)], axis=1) for j in range(seg)], axis=0)
    zxy_dst[...] = _dot(xnp.reshape(tt, D_MODEL).astype(BF16), wxy_ref[...])


def _pmixer_mix(x_ref, xnb_s, wnat_ref, z_s, zxy_s, gbias_ref, onorm_ref, convw_ref, convb_ref, wai_ref, ba_ref, bi_ref,
                lam_ref, wout_ref, y_ref, c_ref, n_ref, m_ref, h_ref, cv_ref, obuf, hm_s, *, cl, tt, zc):
    seg = tt // SUBLANES
    pitch = seg + SUBLANES

    u3 = zxy_s[:, 0:LRU_WIDTH].reshape(seg, SUBLANES, LRU_WIDTH)
    tail = cv_ref[0]
    sub = lax.broadcasted_iota(jnp.int32, (SUBLANES, LRU_WIDTH), 0)
    wrap = []
    for i in range(CONV_W - 1):
        prev = pltpu.roll(u3[seg - (CONV_W - 1) + i], 1, axis=0)
        fill = tail[SUBLANES - (CONV_W - 1) + i:SUBLANES - (CONV_W - 1) + i + 1, :]
        wrap.append(jnp.where(sub == 0, fill, prev))
        cv_ref[0, SUBLANES - (CONV_W - 1) + i:SUBLANES - (CONV_W - 1) + i + 1, :] = (
            u3[seg - (CONV_W - 1) + i][SUBLANES - 1:SUBLANES, :])
    ext = jnp.concatenate([jnp.stack(wrap, axis=0), u3], axis=0)
    conv3 = convb_ref[...] + ext[CONV_W - 1:] * convw_ref[CONV_W - 1:CONV_W, :]
    for j in range(1, CONV_W):
        conv3 = conv3 + ext[CONV_W - 1 - j:CONV_W - 1 - j + seg] * convw_ref[CONV_W - 1 - j:CONV_W - j, :]

    conv = conv3.reshape(tt, LRU_WIDTH)
    bounds = [min(k * zc, Z_NAT) for k in range(-(-Z_NAT // zc) + 1)]
    zcols = list(zip(bounds[:-1], bounds[1:]))

    def project(k):
        lo, hi = zcols[k]
        z_s[:, lo:hi] = _dot(xnb_s[...], wnat_ref[:, lo:hi])

    project(0)
    nrow = len(zcols) - 1
    step = tt // nrow
    a_parts, x_parts = [], []
    for c in range(nrow):
        r = slice(c * step, (c + 1) * step)
        a_c, x_c = _lru_gates(conv[r], wai_ref, ba_ref, bi_ref, lam_ref)
        a_parts.append(a_c)
        x_parts.append(x_c)
        project(c + 1)
    a = jnp.concatenate(a_parts, axis=0)
    xin = jnp.concatenate(x_parts, axis=0)

    a3 = a.reshape(seg, SUBLANES, LRU_WIDTH)
    x3 = xin.reshape(seg, SUBLANES, LRU_WIDTH)
    hs = [x3[0]]
    ps = [a3[0]]
    for j in range(1, seg):
        hs.append(a3[j] * hs[-1] + x3[j])
        ps.append(a3[j] * ps[-1])
    carry = h_ref[0]
    cin = []
    for s in range(SUBLANES):
        cin.append(carry)
        carry = hs[-1][s:s + 1, :] + ps[-1][s:s + 1, :] * carry
    h_ref[0] = carry
    cin = jnp.concatenate(cin, axis=0)
    h3 = jnp.stack([hs[j] + ps[j] * cin for j in range(seg)], axis=0)

    hl = (jax.nn.gelu(zxy_s[:, LRU_WIDTH:]) * h3.reshape(tt, LRU_WIDTH)).astype(BF16)
    ol3 = _dot(hl, wout_ref[ML_WIDTH:ML_WIDTH + LRU_WIDTH, :]).reshape(seg, SUBLANES, D_MODEL)
    for j in range(seg):
        for k in range(D_MODEL // LANES):
            obuf[k, pl.ds(j, SUBLANES, stride=pitch), :] = ol3[j][:, k * LANES:(k + 1) * LANES]
    out_lru = jnp.concatenate(
        [jnp.concatenate([obuf[k, pl.ds(s * pitch, seg), :] for s in range(SUBLANES)], axis=0)
         for k in range(D_MODEL // LANES)], axis=1)

    _mlstm_blocks(z_s, gbias_ref, onorm_ref, c_ref, n_ref, m_ref, hm_s, tt, cl)

    y_ref[...] = x_ref[...] + _dot(hm_s[...], wout_ref[0:ML_WIDTH, :]) + out_lru


def _pmixer_kernel(x_ref, gmix_ref, wnat_ref, wxy_ref, gbias_ref, onorm_ref, convw_ref,
                   convb_ref, wai_ref, ba_ref, bi_ref, lam_ref, wout_ref,
                   y_ref, c_ref, n_ref, m_ref, h_ref, cv_ref,
                   z_s, zxy_s, pbuf, obuf, hm_s, xnb_s, *, cl, tt, zc):
    @pl.when(pl.program_id(1) == 0)
    def _():
        c_ref[...] = jnp.zeros_like(c_ref)
        n_ref[...] = jnp.zeros_like(n_ref)
        m_ref[...] = jnp.zeros_like(m_ref)
        h_ref[...] = jnp.zeros_like(h_ref)
        cv_ref[...] = jnp.zeros_like(cv_ref)

    _pmixer_project(x_ref, gmix_ref, wxy_ref, zxy_s, pbuf, xnb_s, tt=tt)
    _pmixer_mix(x_ref, xnb_s, wnat_ref, z_s, zxy_s, gbias_ref, onorm_ref, convw_ref, convb_ref, wai_ref, ba_ref,
                bi_ref, lam_ref, wout_ref, y_ref, c_ref, n_ref, m_ref, h_ref, cv_ref, obuf, hm_s,
                cl=cl, tt=tt, zc=zc)


def _pmixer(x, p, l, *, nb, t, tt, cl, zc=Z_CHUNK):
    nt = t // tt
    pitch = tt // SUBLANES + SUBLANES
    tok = lambda b, i: (b * nt + i, 0)
    st4 = lambda b, i: (b, 0, 0, 0)
    st3 = lambda b, i: (b, 0, 0)
    lay3 = lambda b, i: (l, 0, 0)
    lay4 = lambda b, i: (l, 0, 0, 0)
    return pl.pallas_call(
        functools.partial(_pmixer_kernel, cl=cl, tt=tt, zc=zc),
        out_shape=(
            jax.ShapeDtypeStruct((nb * t, D_MODEL), F32),
            jax.ShapeDtypeStruct((nb, ML_HEADS, ML_DV, ML_DK), F32),
            jax.ShapeDtypeStruct((nb, ML_HEADS, ML_DK), F32),
            jax.ShapeDtypeStruct((nb, 1, LANES), F32),
            jax.ShapeDtypeStruct((nb, 1, LRU_WIDTH), F32),
            jax.ShapeDtypeStruct((nb, SUBLANES, LRU_WIDTH), F32),
        ),
        grid=(nb, nt),
        in_specs=[
            pl.BlockSpec((tt, D_MODEL), tok),
            _resident((None, 1, D_MODEL), lay3),
            _resident((None, D_MODEL, Z_NAT), lay3),
            _resident((None, D_MODEL, Z_XY), lay3),
            _resident((None, 1, LANES), lay3),
            _resident((None, ML_HEADS, ML_DV), lay3),
            _resident((None, CONV_W, LRU_WIDTH), lay3),
            _resident((None, 1, LRU_WIDTH), lay3),
            _resident((None, LRU_BLOCKS, LRU_BW, 2 * LRU_BW), lay4),
            _resident((None, 1, LRU_WIDTH), lay3),
            _resident((None, 1, LRU_WIDTH), lay3),
            _resident((None, 1, LRU_WIDTH), lay3),
            _resident((None, ML_WIDTH + LRU_WIDTH, D_MODEL), lay3),
        ],
        out_specs=(
            pl.BlockSpec((tt, D_MODEL), tok),
            pl.BlockSpec((1, ML_HEADS, ML_DV, ML_DK), st4),
            pl.BlockSpec((1, ML_HEADS, ML_DK), st3),
            pl.BlockSpec((1, 1, LANES), st3),
            pl.BlockSpec((1, 1, LRU_WIDTH), st3),
            pl.BlockSpec((1, SUBLANES, LRU_WIDTH), st3),
        ),
        scratch_shapes=[pltpu.VMEM((tt, Z_NAT), F32), pltpu.VMEM((tt, Z_XY), F32),
                        pltpu.VMEM((LRU_WIDTH // LANES, SUBLANES * pitch, LANES), F32),
                        pltpu.VMEM((D_MODEL // LANES, SUBLANES * pitch, LANES), F32),
                        pltpu.VMEM((tt, ML_WIDTH), BF16), pltpu.VMEM((tt, D_MODEL), BF16)],
        compiler_params=_params("parallel", "arbitrary"),
        name="pmixer",
    )(x, p["mix_norm"], p["w_in"], p["w_xy"], p["gbias"], p["ml_out_norm"], p["lru_conv_w"],
      p["lru_conv_b"], p["w_ai"], p["lru_b_a"], p["lru_b_i"], p["lru_lambda"], p["w_out"])


def _zproj_kernel(x_ref, g_ref, wnat_ref, wxy_ref, znat_ref, zxy_ref):
    xn = _rms(x_ref[...], g_ref[...]).astype(BF16)
    znat_ref[...] = _dot(xn, wnat_ref[...])
    zxy_ref[...] = _dot(xn, wxy_ref[...])


def _zproj(x, p, l):
    m = x.shape[0]
    lay3 = lambda i: (l, 0, 0)
    return pl.pallas_call(
        _zproj_kernel,
        out_shape=(jax.ShapeDtypeStruct((m, Z_NAT), F32), jax.ShapeDtypeStruct((m, Z_XY), F32)),
        grid=(1,),
        in_specs=[
            pl.BlockSpec((m, D_MODEL), lambda i: (0, 0)),
            pl.BlockSpec((None, 1, D_MODEL), lay3),
            pl.BlockSpec((None, D_MODEL, Z_NAT), lay3),
            pl.BlockSpec((None, D_MODEL, Z_XY), lay3),
        ],
        out_specs=(pl.BlockSpec((m, Z_NAT), lambda i: (0, 0)), pl.BlockSpec((m, Z_XY), lambda i: (0, 0))),
        compiler_params=_params("arbitrary"),
        name="zproj",
    )(x, p["mix_norm"], p["w_in"], p["w_xy"])


def _smixer_kernel(znat_ref, zxy_ref, x_ref, c0_ref, n0_ref, m0_ref, h0_ref, cv0_ref,
                   gbias_ref, onorm_ref, convw_ref, convb_ref, wai_ref, ba_ref, bi_ref, lam_ref,
                   wout_ref,
                   y_ref, c_ref, n_ref, m_ref, h_ref, cv_ref,
                   ubuf_s, hm_s, *, ns, t):
    c_ref[...] = c0_ref[...]
    n_ref[...] = n0_ref[...]
    m_ref[...] = m0_ref[...]
    pad = t + SUBLANES

    convs = []
    for s in range(ns):
        one = pl.ds(s, 1)
        _mlstm_blocks(znat_ref, gbias_ref, onorm_ref, c_ref.at[one], n_ref.at[one], m_ref.at[one], hm_s,
                      t, t, base=s * t)
        u = zxy_ref[s * t:(s + 1) * t, 0:LRU_WIDTH]
        ubuf_s[s * pad:s * pad + SUBLANES, :] = cv0_ref[s]
        ubuf_s[s * pad + SUBLANES:(s + 1) * pad, :] = u
        cv_ref[s] = ubuf_s[s * pad + t:(s + 1) * pad, :]
        conv = convb_ref[...] + u * convw_ref[CONV_W - 1:CONV_W, :]
        for j in range(1, CONV_W):
            conv = conv + (ubuf_s[pl.ds(s * pad + SUBLANES - j, t), :]
                           * convw_ref[CONV_W - 1 - j:CONV_W - j, :])
        convs.append(conv)
    a, xin = _lru_gates(jnp.concatenate(convs, axis=0), wai_ref, ba_ref, bi_ref, lam_ref)

    rmod = lax.broadcasted_iota(jnp.int32, (ns * t, LRU_WIDTH), 0) & (SUBLANES - 1)
    d = 1
    while d < SUBLANES:
        keep = rmod >= d
        xin = xin + a * jnp.where(keep, pltpu.roll(xin, d, axis=0), 0.0)
        a = a * jnp.where(keep, pltpu.roll(a, d, axis=0), 1.0)
        d *= 2
    groups = []
    for s in range(ns):
        carry = h0_ref[s]
        for gidx in range(t // SUBLANES):
            lo = s * t + gidx * SUBLANES
            hb = xin[lo:lo + SUBLANES, :] + a[lo:lo + SUBLANES, :] * carry
            carry = hb[SUBLANES - 1:SUBLANES, :]
            groups.append(hb)
        h_ref[s] = carry
    h_lru = jnp.concatenate(groups, axis=0)

    hl = (jax.nn.gelu(zxy_ref[:, LRU_WIDTH:]) * h_lru).astype(BF16)
    out = (_dot(hm_s[...], wout_ref[0:ML_WIDTH, :])
           + _dot(hl, wout_ref[ML_WIDTH:ML_WIDTH + LRU_WIDTH, :]))
    y_ref[...] = x_ref[...] + out


def _smixer(znat, zxy, x, st, p, l, *, nb, t, ns):
    tok = lambda b: (b, 0)
    lay3 = lambda b: (l, 0, 0)
    lay4 = lambda b: (l, 0, 0, 0)
    st3 = lambda b: (b, 0, 0)
    lst = lambda b: (l, b, 0, 0)
    c0, n0, m0, h0, cv0 = st
    return pl.pallas_call(
        functools.partial(_smixer_kernel, ns=ns, t=t),
        out_shape=(
            jax.ShapeDtypeStruct((nb * t, D_MODEL), F32),
            jax.ShapeDtypeStruct((nb, ML_HEADS, ML_DV, ML_DK), F32),
            jax.ShapeDtypeStruct((nb, ML_HEADS, ML_DK), F32),
            jax.ShapeDtypeStruct((nb, 1, LANES), F32),
            jax.ShapeDtypeStruct((nb, 1, LRU_WIDTH), F32),
            jax.ShapeDtypeStruct((nb, SUBLANES, LRU_WIDTH), F32),
        ),
        grid=(nb // ns,),
        in_specs=[
            pl.BlockSpec((ns * t, Z_NAT), tok),
            pl.BlockSpec((ns * t, Z_XY), tok),
            pl.BlockSpec((ns * t, D_MODEL), tok),
            pl.BlockSpec((None, ns, ML_HEADS, ML_DV, ML_DK), lambda b: (l, b, 0, 0, 0)),
            pl.BlockSpec((None, ns, ML_HEADS, ML_DK), lst),
            pl.BlockSpec((None, ns, 1, LANES), lst),
            pl.BlockSpec((None, ns, 1, LRU_WIDTH), lst),
            pl.BlockSpec((None, ns, SUBLANES, LRU_WIDTH), lst),
            pl.BlockSpec((None, 1, LANES), lay3),
            pl.BlockSpec((None, ML_HEADS, ML_DV), lay3),
            pl.BlockSpec((None, CONV_W, LRU_WIDTH), lay3),
            pl.BlockSpec((None, 1, LRU_WIDTH), lay3),
            pl.BlockSpec((None, LRU_BLOCKS, LRU_BW, 2 * LRU_BW), lay4),
            pl.BlockSpec((None, 1, LRU_WIDTH), lay3),
            pl.BlockSpec((None, 1, LRU_WIDTH), lay3),
            pl.BlockSpec((None, 1, LRU_WIDTH), lay3),
            pl.BlockSpec((None, ML_WIDTH + LRU_WIDTH, D_MODEL), lay3),
        ],
        out_specs=(
            pl.BlockSpec((ns * t, D_MODEL), tok),
            pl.BlockSpec((ns, ML_HEADS, ML_DV, ML_DK), lambda b: (b, 0, 0, 0)),
            pl.BlockSpec((ns, ML_HEADS, ML_DK), st3),
            pl.BlockSpec((ns, 1, LANES), st3),
            pl.BlockSpec((ns, 1, LRU_WIDTH), st3),
            pl.BlockSpec((ns, SUBLANES, LRU_WIDTH), st3),
        ),
        scratch_shapes=[pltpu.VMEM((ns * (t + SUBLANES), LRU_WIDTH), F32),
                        pltpu.VMEM((ns * t, ML_WIDTH), BF16)],
        compiler_params=_params("parallel"),
        name="smixer",
    )(znat, zxy, x, c0, n0, m0, h0, cv0, p["gbias"], p["ml_out_norm"], p["lru_conv_w"],
      p["lru_conv_b"], p["w_ai"], p["lru_b_a"], p["lru_b_i"], p["lru_lambda"], p["w_out"])


def _xattn_kernel(x_ref, g_ref, wq_ref, mk_ref, mv_ref, wo_ref, y_ref, o_s, *, ns, rows, split):
    step = rows // split
    chunks = [(s, slice(s * rows + c * step, s * rows + (c + 1) * step)) for s in range(ns) for c in range(split)]
    heads = [slice(h * XA_DH, (h + 1) * XA_DH) for h in range(XA_HEADS)]

    def scores(s, r):
        q = _dot(_rms(x_ref[r, :], g_ref[...]).astype(BF16), wq_ref[...])
        return [_nt_dot(q[:, c].astype(BF16), mk_ref[s, :, c].astype(BF16)) * (XA_DH ** -0.5) for c in heads]

    def attend(s, r, sc):
        for c, sch in zip(heads, sc):
            p = jnp.exp(sch - jnp.max(sch, axis=1, keepdims=True))
            p = p / jnp.sum(p, axis=1, keepdims=True)
            o_s[r, c] = _dot(p.astype(BF16), mv_ref[s, :, c].astype(BF16)).astype(BF16)

    def project(r):
        y_ref[r, :] = x_ref[r, :] + _dot(o_s[r, :], wo_ref[...])

    pending = None
    for s, r in chunks:
        sc = scores(s, r)
        if pending is not None:
            ps, pr, psc = pending
            attend(ps, pr, psc)
            project(pr)
        pending = (s, r, sc)
    ps, pr, psc = pending
    attend(ps, pr, psc)
    project(pr)


def _xattn(x, g, wq, mk, mv, wo, l, *, nb, t, tt, ns=1, split=1):
    nt = t // tt
    assert ns == 1 or nt == 1
    tok = lambda b, i: (b * nt + i, 0)
    lay3 = lambda b, i: (l, 0, 0)
    mem = lambda b, i: (l, b, 0, 0)
    return pl.pallas_call(
        functools.partial(_xattn_kernel, ns=ns, rows=tt, split=split),
        out_shape=jax.ShapeDtypeStruct((nb * t, D_MODEL), F32),
        grid=(nb // ns, nt),
        in_specs=[
            pl.BlockSpec((ns * tt, D_MODEL), tok),
            pl.BlockSpec((None, 1, D_MODEL), lay3),
            pl.BlockSpec((None, D_MODEL, D_MODEL), lay3),
            pl.BlockSpec((None, ns, MEM_LEN, D_MODEL), mem),
            pl.BlockSpec((None, ns, MEM_LEN, D_MODEL), mem),
            pl.BlockSpec((None, D_MODEL, D_MODEL), lay3),
        ],
        out_specs=pl.BlockSpec((ns * tt, D_MODEL), tok),
        scratch_shapes=[pltpu.VMEM((ns * tt, D_MODEL), BF16)],
        compiler_params=_params("parallel", "arbitrary"),
        name="xattn",
    )(x, g, wq, mk, mv, wo)


def _memkv_kernel(mem_ref, g_ref, wk_ref, wv_ref, k_ref, v_ref):
    mn = _rms(mem_ref[...], g_ref[...]).astype(BF16)
    k_ref[...] = _dot(mn, wk_ref[...])
    v_ref[...] = _dot(mn, wv_ref[...])


def _memkv(mem, g, wk, wv):
    nb = mem.shape[0]
    wspec = pl.BlockSpec((None, D_MODEL, D_MODEL), lambda l, b: (l, 0, 0))
    ospec = pl.BlockSpec((None, None, MEM_LEN, D_MODEL), lambda l, b: (l, b, 0, 0))
    out = jax.ShapeDtypeStruct((DEPTH, nb, MEM_LEN, D_MODEL), F32)
    return pl.pallas_call(
        _memkv_kernel,
        out_shape=(out, out),
        grid=(DEPTH, nb),
        in_specs=[pl.BlockSpec((None, MEM_LEN, D_MODEL), lambda l, b: (b, 0, 0)),
                  pl.BlockSpec((None, 1, D_MODEL), lambda l, b: (l, 0, 0)), wspec, wspec],
        out_specs=(ospec, ospec),
        compiler_params=_params("parallel", "parallel"),
        name="memkv",
    )(mem, g, wk, wv)


def kernel(x_prompt, x_sample, mem_prompt, state_mlstm_C, state_mlstm_n, state_mlstm_m, state_lru_h,
           state_lru_conv, cache_mem_k, cache_mem_v, ffn1_norm, ffn1_w_gate, ffn1_w_up, ffn1_w_down,
           mix_norm, w_in, ml_b_i, ml_b_f, ml_out_norm, lru_conv_w, lru_conv_b, lru_w_a, lru_b_a,
           lru_w_i, lru_b_i, lru_lambda, w_out, xattn_norm, mem_norm, xattn_w_q, xattn_w_k, xattn_w_v,
           xattn_w_o, ffn2_norm, ffn2_w_gate, ffn2_w_up, ffn2_w_down, final_norm):
    bp, tp, _ = x_prompt.shape
    bs, ts, _ = x_sample.shape
    xp = x_prompt.reshape(bp * tp, D_MODEL)
    xs = x_sample.reshape(bs * ts, D_MODEL)

    bf = lambda w: w.astype(BF16)
    row = lambda v: v.astype(F32).reshape(DEPTH, 1, -1)
    gate_pad = ((0, 0), (0, 0), (0, LANES - 2 * ML_HEADS))
    w_in_b = bf(w_in)
    p = {
        "mix_norm": row(mix_norm),
        "w_in": w_in_b,
        "w_xy": w_in_b[:, :, W_X:],
        "gbias": jnp.pad(jnp.concatenate([ml_b_i, ml_b_f], axis=1).astype(F32)[:, None, :], gate_pad),
        "ml_out_norm": ml_out_norm.astype(F32),
        "lru_conv_w": lru_conv_w, "lru_conv_b": row(lru_conv_b),
        "w_ai": bf(jnp.concatenate([lru_w_a, lru_w_i], axis=-1)),
        "lru_b_a": row(lru_b_a), "lru_b_i": row(lru_b_i), "lru_lambda": row(lru_lambda),
        "w_out": bf(w_out),
    }
    ffn1 = (row(ffn1_norm), bf(ffn1_w_gate), bf(ffn1_w_up), bf(ffn1_w_down))
    ffn2 = (row(ffn2_norm), bf(ffn2_w_gate), bf(ffn2_w_up), bf(ffn2_w_down))
    xa_g, xa_q, xa_o = row(xattn_norm), bf(xattn_w_q), bf(xattn_w_o)
    fin = final_norm.astype(F32).reshape(1, D_MODEL)

    pk, pv = _memkv(mem_prompt, row(mem_norm), bf(xattn_w_k), bf(xattn_w_v))
    sk = cache_mem_k.reshape(DEPTH, bs, MEM_LEN, D_MODEL)
    sv = cache_mem_v.reshape(DEPTH, bs, MEM_LEN, D_MODEL)
    s_state = (
        state_mlstm_C.astype(F32), state_mlstm_n.astype(F32),
        jnp.pad(state_mlstm_m.astype(F32)[:, :, None, :], ((0, 0), (0, 0), (0, 0), (0, LANES - ML_HEADS))),
        state_lru_h.astype(F32)[:, :, None, :],
        jnp.pad(state_lru_conv.astype(F32), ((0, 0), (0, 0), (SUBLANES - (CONV_W - 1), 0), (0, 0))),
    )

    p_out = [[] for _ in range(5)]
    s_out = [[] for _ in range(5)]
    for l in range(DEPTH):
        last = l == DEPTH - 1
        xp, xs = _ffn(xp, xs, *ffn1, fin, l, tm=512, tf=256, final_norm=False)
        xp, *st = _pmixer(xp, p, l, nb=bp, t=tp, tt=512, cl=ML_BLOCK, zc=768 if l == 0 else 384)
        xp = _xattn(xp, xa_g, xa_q, pk, pv, xa_o, l, nb=bp, t=tp, tt=1024, split=4)
        for acc, v in zip(p_out, st):
            acc.append(v)
        znat, zxy = _zproj(xs, p, l)
        xs, *st = _smixer(znat, zxy, xs, s_state, p, l, nb=bs, t=ts, ns=4)
        xs = _xattn(xs, xa_g, xa_q, sk, sv, xa_o, l, nb=bs, t=ts, tt=ts, ns=4)
        for acc, v in zip(s_out, st):
            acc.append(v)
        xp, xs = _ffn(xp, xs, *ffn2, fin, l, tm=512, tf=256 if l == 0 else 128, final_norm=last)

    def states(acc):
        c, n, m, h, cv = (jnp.stack(a) for a in acc)
        return c, n, m[:, :, 0, :ML_HEADS], h[:, :, 0, :], cv[:, :, SUBLANES - (CONV_W - 1):, :]

    return (xp.reshape(bp, tp, D_MODEL), xs.reshape(bs, ts, D_MODEL),
            *states(p_out),
            pk.reshape(DEPTH, bp, MEM_LEN, XA_HEADS, XA_DH), pv.reshape(DEPTH, bp, MEM_LEN, XA_HEADS, XA_DH),
            *states(s_out))
```

```python
import functools

import jax
import jax.numpy as jnp
from jax import lax
from jax.experimental import pallas as pl
from jax.experimental.pallas import tpu as pltpu

F32 = jnp.float32
BF16 = jnp.bfloat16

D_MODEL = 1024
DEPTH = 2
ML_BLOCK = 128
ML_HEADS = 4
ML_DV = 256
ML_DK = 128
ML_WIDTH = ML_HEADS * ML_DV
LRU_WIDTH = 1024
LRU_BLOCKS = 8
LRU_BW = LRU_WIDTH // LRU_BLOCKS
CONV_W = 4
LRU_C = 8.0
MEM_LEN = 256
XA_HEADS = 4
XA_DH = D_MODEL // XA_HEADS
D_FF = 4 * D_MODEL
EPS = 1e-6

LANES = 128
SUBLANES = 8
VMEM_LIMIT = 56 * 1024 * 1024

Z_Q = 0
Z_K = Z_Q + ML_HEADS * ML_DK
Z_V = Z_K + ML_HEADS * ML_DK
Z_O = Z_V + ML_WIDTH
Z_G = Z_O + ML_WIDTH
Z_NAT = Z_G + LANES
Z_CHUNK = 768
Z_XY = 2 * LRU_WIDTH
W_X = Z_G + 2 * ML_HEADS


def _rms(x, g):
    return x * lax.rsqrt(jnp.mean(x * x, axis=-1, keepdims=True) + EPS) * g


def _nt_dot(a, b):
    return lax.dot_general(a, b, (((1,), (1,)), ((), ())), preferred_element_type=F32)


def _tn_dot(a, b):
    return lax.dot_general(a, b, (((0,), (0,)), ((), ())), preferred_element_type=F32)


def _dot(a, b):
    return jnp.dot(a, b, preferred_element_type=F32)


def _params(*sem):
    return pltpu.CompilerParams(dimension_semantics=sem, vmem_limit_bytes=VMEM_LIMIT)


def _resident(shape, index_map):
    return pl.BlockSpec(shape, index_map, pipeline_mode=pl.Buffered(1))


def _ffn_body(parts, g_ref, wg_ref, wu_ref, wd_ref, fg_ref, h_s, *, tf, final_norm):
    xs = [x_ref[...] for x_ref, _ in parts]
    rows = sum(x.shape[0] for x in xs)
    xn = jnp.concatenate([_rms(x, g_ref[...]).astype(BF16) for x in xs], axis=0)
    for j in range(D_FF // tf):
        cols = slice(j * tf, (j + 1) * tf)
        g = _dot(xn, wg_ref[:, cols])
        u = _dot(xn, wu_ref[:, cols])
        h_s[0:rows, cols] = (g * jax.nn.sigmoid(g) * u).astype(BF16)
    d = _dot(h_s[0:rows, :], wd_ref[...])
    lo = 0
    for x, (_, o_ref) in zip(xs, parts):
        y = x + 0.5 * d[lo:lo + x.shape[0], :]
        if final_norm:
            y = _rms(y, fg_ref[...])
        o_ref[...] = y
        lo += x.shape[0]


def _ffn_kernel(xp_ref, xs_ref, g_ref, wg_ref, wu_ref, wd_ref, fg_ref, op_ref, os_ref, h_s, **kw):
    last = pl.num_programs(0) - 1

    @pl.when(pl.program_id(0) < last)
    def _():
        _ffn_body([(xp_ref, op_ref)], g_ref, wg_ref, wu_ref, wd_ref, fg_ref, h_s, **kw)

    @pl.when(pl.program_id(0) == last)
    def _():
        _ffn_body([(xp_ref, op_ref), (xs_ref, os_ref)], g_ref, wg_ref, wu_ref, wd_ref, fg_ref, h_s, **kw)


def _ffn(xp, xs, g, wg, wu, wd, fg, l, *, tm, tf, final_norm):
    np_tiles = xp.shape[0] // tm
    ms = xs.shape[0]
    ptile = lambda i: (i, 0)
    whole = lambda i: (0, 0)
    lay = lambda i: (l, 0, 0)
    return pl.pallas_call(
        functools.partial(_ffn_kernel, tf=tf, final_norm=final_norm),
        out_shape=(jax.ShapeDtypeStruct(xp.shape, F32), jax.ShapeDtypeStruct(xs.shape, F32)),
        grid=(np_tiles,),
        in_specs=[
            pl.BlockSpec((tm, D_MODEL), ptile),
            _resident((ms, D_MODEL), whole),
            _resident((None, 1, D_MODEL), lay),
            _resident((None, D_MODEL, D_FF), lay),
            _resident((None, D_MODEL, D_FF), lay),
            _resident((None, D_FF, D_MODEL), lay),
            _resident((1, D_MODEL), whole),
        ],
        out_specs=(pl.BlockSpec((tm, D_MODEL), ptile), pl.BlockSpec((ms, D_MODEL), whole)),
        scratch_shapes=[pltpu.VMEM((tm + ms, D_FF), BF16)],
        compiler_params=_params("arbitrary"),
        name="ffn",
    )(xp, xs, g, wg, wu, wd, fg)


def _log_sigmoid(x):
    return jnp.minimum(x, 0.0) - jnp.log1p(jnp.exp(-jnp.abs(x)))


def _softplus(x):
    return jnp.maximum(x, 0.0) + jnp.log1p(jnp.exp(-jnp.abs(x)))


def _lane_pick(x, lane_ids, idx):
    return jnp.sum(jnp.where(lane_ids == idx, x, 0.0), axis=1, keepdims=True)


def _mlstm_blocks(z_ref, gbias_ref, onorm_ref, c_ref, n_ref, m_ref, hm_s, rows, cl, base=0):
    lane1 = lax.broadcasted_iota(jnp.int32, (1, LANES), 1)
    m_vec = m_ref[0]
    m_heads = [_lane_pick(m_vec, lane1, h) for h in range(ML_HEADS)]
    for c in range(rows // cl):
        m_heads = _mlstm_chunk(z_ref, gbias_ref, onorm_ref, c_ref, n_ref, m_heads, hm_s, base + c * cl, cl)
    for h in range(ML_HEADS):
        m_vec = jnp.where(lane1 == h, m_heads[h], m_vec)
    m_ref[0] = m_vec


def _mlstm_chunk(z_ref, gbias_ref, onorm_ref, c_ref, n_ref, m_heads, hm_s, r0, cl):
    lane = lax.broadcasted_iota(jnp.int32, (cl, LANES), 1)
    row = lax.broadcasted_iota(jnp.int32, (cl, LANES), 0)
    tril = (lax.broadcasted_iota(jnp.int32, (cl, cl), 1)
            <= lax.broadcasted_iota(jnp.int32, (cl, cl), 0))

    gates = z_ref[r0:r0 + cl, Z_G:Z_G + LANES] + gbias_ref[...]
    bsum = _log_sigmoid(gates)
    d = 1
    while d < cl:
        bsum = bsum + jnp.where(row >= d, pltpu.roll(bsum, d, axis=0), 0.0)
        d *= 2
    mixed = jnp.where(lane < ML_HEADS, gates, bsum)
    if cl < LANES:
        mixed = jnp.concatenate([mixed, jnp.zeros((LANES - cl, LANES), F32)], axis=0)
    mixed_t = mixed.T

    m_next = []
    for h in range(ML_HEADS):
        b_col = _lane_pick(bsum, lane, ML_HEADS + h)
        ig_col = _lane_pick(gates, lane, h)
        b_row = mixed_t[ML_HEADS + h:ML_HEADS + h + 1, 0:cl]
        ig_row = mixed_t[h:h + 1, 0:cl]
        m_prev = m_heads[h]

        dmat = jnp.where(tril, b_col - b_row + ig_row, -jnp.inf)
        inter = b_col + m_prev
        m_t = jnp.maximum(inter, jnp.max(dmat, axis=1, keepdims=True))

        qf = z_ref[r0:r0 + cl, Z_Q + h * ML_DK:Z_Q + (h + 1) * ML_DK] * (ML_DK ** -0.5)
        kf = z_ref[r0:r0 + cl, Z_K + h * ML_DK:Z_K + (h + 1) * ML_DK]
        vf = z_ref[r0:r0 + cl, Z_V + h * ML_DV:Z_V + (h + 1) * ML_DV]
        q = qf.astype(BF16)
        k = kf.astype(BF16)

        s = _nt_dot(q, k) * jnp.exp(dmat - m_t)
        w_inter = jnp.exp(inter - m_t)
        c_old = c_ref[0, h]
        n_old = n_ref[0, h:h + 1, :]
        num = w_inter * _nt_dot(q, c_old.astype(BF16)) + _dot(s.astype(BF16), vf.astype(BF16))
        den = (w_inter * jnp.sum(qf * n_old, axis=1, keepdims=True)
               + jnp.sum(s, axis=1, keepdims=True))
        hh = num / jnp.maximum(jnp.abs(den), jnp.exp(-m_t))

        hh = hh * lax.rsqrt(jnp.mean(hh * hh, axis=1, keepdims=True) + EPS) * onorm_ref[h:h + 1, :]
        o_gate = jax.nn.sigmoid(z_ref[r0:r0 + cl, Z_O + h * ML_DV:Z_O + (h + 1) * ML_DV])
        hm_s[r0:r0 + cl, h * ML_DV:(h + 1) * ML_DV] = (o_gate * hh).astype(BF16)

        m_new = m_t[cl - 1:cl, :]
        b_last = b_col[cl - 1:cl, :]
        wgt = jnp.exp(b_last - b_col + ig_col - m_new)
        decay = jnp.exp(b_last + m_prev - m_new)
        c_ref[0, h] = decay * c_old + _tn_dot((wgt * vf).astype(BF16), k)
        n_ref[0, h:h + 1, :] = decay * n_old + jnp.sum(wgt * kf, axis=0, keepdims=True)
        m_next.append(m_new)
    return m_next


def _lru_gates(conv, wai_ref, ba_ref, bi_ref, lam_ref):
    conv_b = conv.astype(BF16)
    pre = [_dot(conv_b[:, n * LRU_BW:(n + 1) * LRU_BW], wai_ref[n])
           for n in range(LRU_BLOCKS)]
    r_gate = jax.nn.sigmoid(jnp.concatenate([p[:, :LRU_BW] for p in pre], axis=1) + ba_ref[...])
    i_gate = jax.nn.sigmoid(jnp.concatenate([p[:, LRU_BW:] for p in pre], axis=1) + bi_ref[...])
    log_a = r_gate * (-LRU_C * _softplus(-lam_ref[...]))
    a = jnp.exp(log_a)
    w = -jnp.tanh(log_a) * (a * a + 1.0)
    xin = jnp.where(w > 0.0, w * lax.rsqrt(w), 0.0) * (i_gate * conv)
    return a, xin


def _pmixer_project(x_ref, gmix_ref, wxy_ref, zxy_dst, pbuf, xnb_s, *, tt):
    seg = tt // SUBLANES
    pitch = seg + SUBLANES
    xn = _rms(x_ref[...], gmix_ref[...])
    xnb_s[...] = xn.astype(BF16)
    for k in range(LRU_WIDTH // LANES):
        for s in range(SUBLANES):
            pbuf[k, pl.ds(s * pitch, seg), :] = xn[s * seg:(s + 1) * seg, k * LANES:(k + 1) * LANES]
    xnp = jnp.stack(
        [jnp.concatenate([pbuf[k, pl.ds(j, SUBLANES, stride=pitch), :]
                          for k in range(LRU_WIDTH // LANES)], axis=1) for j in range(seg)], axis=0)
    zxy_dst[...] = _dot(xnp.reshape(tt, D_MODEL).astype(BF16), wxy_ref[...])


def _pmixer_mix(x_ref, xnb_s, wnat_ref, z_s, zxy_s, gbias_ref, onorm_ref, convw_ref, convb_ref, wai_ref, ba_ref, bi_ref,
                lam_ref, wout_ref, y_ref, c_ref, n_ref, m_ref, h_ref, cv_ref, obuf, hm_s, *, cl, tt, zc):
    seg = tt // SUBLANES
    pitch = seg + SUBLANES

    u3 = zxy_s[:, 0:LRU_WIDTH].reshape(seg, SUBLANES, LRU_WIDTH)
    tail = cv_ref[0]
    sub = lax.broadcasted_iota(jnp.int32, (SUBLANES, LRU_WIDTH), 0)
    wrap = []
    for i in range(CONV_W - 1):
        prev = pltpu.roll(u3[seg - (CONV_W - 1) + i], 1, axis=0)
        fill = tail[SUBLANES - (CONV_W - 1) + i:SUBLANES - (CONV_W - 1) + i + 1, :]
        wrap.append(jnp.where(sub == 0, fill, prev))
        cv_ref[0, SUBLANES - (CONV_W - 1) + i:SUBLANES - (CONV_W - 1) + i + 1, :] = (
            u3[seg - (CONV_W - 1) + i][SUBLANES - 1:SUBLANES, :])
    ext = jnp.concatenate([jnp.stack(wrap, axis=0), u3], axis=0)
    conv3 = convb_ref[...] + ext[CONV_W - 1:] * convw_ref[CONV_W - 1:CONV_W, :]
    for j in range(1, CONV_W):
        conv3 = conv3 + ext[CONV_W - 1 - j:CONV_W - 1 - j + seg] * convw_ref[CONV_W - 1 - j:CONV_W - j, :]

    conv = conv3.reshape(tt, LRU_WIDTH)
    bounds = [min(k * zc, Z_NAT) for k in range(-(-Z_NAT // zc) + 1)]
    zcols = list(zip(bounds[:-1], bounds[1:]))

    def project(k):
        lo, hi = zcols[k]
        z_s[:, lo:hi] = _dot(xnb_s[...], wnat_ref[:, lo:hi])

    project(0)
    nrow = len(zcols) - 1
    step = tt // nrow
    a_parts, x_parts = [], []
    for c in range(nrow):
        r = slice(c * step, (c + 1) * step)
        a_c, x_c = _lru_gates(conv[r], wai_ref, ba_ref, bi_ref, lam_ref)
        a_parts.append(a_c)
        x_parts.append(x_c)
        project(c + 1)
    a = jnp.concatenate(a_parts, axis=0)
    xin = jnp.concatenate(x_parts, axis=0)

    a3 = a.reshape(seg, SUBLANES, LRU_WIDTH)
    x3 = xin.reshape(seg, SUBLANES, LRU_WIDTH)
    hs = [x3[0]]
    ps = [a3[0]]
    for j in range(1, seg):
        hs.append(a3[j] * hs[-1] + x3[j])
        ps.append(a3[j] * ps[-1])
    carry = h_ref[0]
    cin = []
    for s in range(SUBLANES):
        cin.append(carry)
        carry = hs[-1][s:s + 1, :] + ps[-1][s:s + 1, :] * carry
    h_ref[0] = carry
    cin = jnp.concatenate(cin, axis=0)
    h3 = jnp.stack([hs[j] + ps[j] * cin for j in range(seg)], axis=0)

    hl = (jax.nn.gelu(zxy_s[:, LRU_WIDTH:]) * h3.reshape(tt, LRU_WIDTH)).astype(BF16)
    ol3 = _dot(hl, wout_ref[ML_WIDTH:ML_WIDTH + LRU_WIDTH, :]).reshape(seg, SUBLANES, D_MODEL)
    for j in range(seg):
        for k in range(D_MODEL // LANES):
            obuf[k, pl.ds(j, SUBLANES, stride=pitch), :] = ol3[j][:, k * LANES:(k + 1) * LANES]
    out_lru = jnp.concatenate(
        [jnp.concatenate([obuf[k, pl.ds(s * pitch, seg), :] for s in range(SUBLANES)], axis=0)
         for k in range(D_MODEL // LANES)], axis=1)

    _mlstm_blocks(z_s, gbias_ref, onorm_ref, c_ref, n_ref, m_ref, hm_s, tt, cl)

    y_ref[...] = x_ref[...] + _dot(hm_s[...], wout_ref[0:ML_WIDTH, :]) + out_lru


def _pmixer_kernel(x_ref, gmix_ref, wnat_ref, wxy_ref, gbias_ref, onorm_ref, convw_ref,
                   convb_ref, wai_ref, ba_ref, bi_ref, lam_ref, wout_ref,
                   y_ref, c_ref, n_ref, m_ref, h_ref, cv_ref,
                   z_s, zxy_s, pbuf, obuf, hm_s, xnb_s, *, cl, tt, zc):
    @pl.when(pl.program_id(1) == 0)
    def _():
        c_ref[...] = jnp.zeros_like(c_ref)
        n_ref[...] = jnp.zeros_like(n_ref)
        m_ref[...] = jnp.zeros_like(m_ref)
        h_ref[...] = jnp.zeros_like(h_ref)
        cv_ref[...] = jnp.zeros_like(cv_ref)

    _pmixer_project(x_ref, gmix_ref, wxy_ref, zxy_s, pbuf, xnb_s, tt=tt)
    _pmixer_mix(x_ref, xnb_s, wnat_ref, z_s, zxy_s, gbias_ref, onorm_ref, convw_ref, convb_ref, wai_ref, ba_ref,
                bi_ref, lam_ref, wout_ref, y_ref, c_ref, n_ref, m_ref, h_ref, cv_ref, obuf, hm_s,
                cl=cl, tt=tt, zc=zc)


def _pmixer(x, p, l, *, nb, t, tt, cl, zc=Z_CHUNK):
    nt = t // tt
    pitch = tt // SUBLANES + SUBLANES
    tok = lambda b, i: (b * nt + i, 0)
    st4 = lambda b, i: (b, 0, 0, 0)
    st3 = lambda b, i: (b, 0, 0)
    lay3 = lambda b, i: (l, 0, 0)
    lay4 = lambda b, i: (l, 0, 0, 0)
    return pl.pallas_call(
        functools.partial(_pmixer_kernel, cl=cl, tt=tt, zc=zc),
        out_shape=(
            jax.ShapeDtypeStruct((nb * t, D_MODEL), F32),
            jax.ShapeDtypeStruct((nb, ML_HEADS, ML_DV, ML_DK), F32),
            jax.ShapeDtypeStruct((nb, ML_HEADS, ML_DK), F32),
            jax.ShapeDtypeStruct((nb, 1, LANES), F32),
            jax.ShapeDtypeStruct((nb, 1, LRU_WIDTH), F32),
            jax.ShapeDtypeStruct((nb, SUBLANES, LRU_WIDTH), F32),
        ),
        grid=(nb, nt),
        in_specs=[
            pl.BlockSpec((tt, D_MODEL), tok),
            _resident((None, 1, D_MODEL), lay3),
            _resident((None, D_MODEL, Z_NAT), lay3),
            _resident((None, D_MODEL, Z_XY), lay3),
            _resident((None, 1, LANES), lay3),
            _resident((None, ML_HEADS, ML_DV), lay3),
            _resident((None, CONV_W, LRU_WIDTH), lay3),
            _resident((None, 1, LRU_WIDTH), lay3),
            _resident((None, LRU_BLOCKS, LRU_BW, 2 * LRU_BW), lay4),
            _resident((None, 1, LRU_WIDTH), lay3),
            _resident((None, 1, LRU_WIDTH), lay3),
            _resident((None, 1, LRU_WIDTH), lay3),
            _resident((None, ML_WIDTH + LRU_WIDTH, D_MODEL), lay3),
        ],
        out_specs=(
            pl.BlockSpec((tt, D_MODEL), tok),
            pl.BlockSpec((1, ML_HEADS, ML_DV, ML_DK), st4),
            pl.BlockSpec((1, ML_HEADS, ML_DK), st3),
            pl.BlockSpec((1, 1, LANES), st3),
            pl.BlockSpec((1, 1, LRU_WIDTH), st3),
            pl.BlockSpec((1, SUBLANES, LRU_WIDTH), st3),
        ),
        scratch_shapes=[pltpu.VMEM((tt, Z_NAT), F32), pltpu.VMEM((tt, Z_XY), F32),
                        pltpu.VMEM((LRU_WIDTH // LANES, SUBLANES * pitch, LANES), F32),
                        pltpu.VMEM((D_MODEL // LANES, SUBLANES * pitch, LANES), F32),
                        pltpu.VMEM((tt, ML_WIDTH), BF16), pltpu.VMEM((tt, D_MODEL), BF16)],
        compiler_params=_params("parallel", "arbitrary"),
        name="pmixer",
    )(x, p["mix_norm"], p["w_in"], p["w_xy"], p["gbias"], p["ml_out_norm"], p["lru_conv_w"],
      p["lru_conv_b"], p["w_ai"], p["lru_b_a"], p["lru_b_i"], p["lru_lambda"], p["w_out"])


def _zproj_kernel(x_ref, g_ref, wnat_ref, wxy_ref, znat_ref, zxy_ref):
    xn = _rms(x_ref[...], g_ref[...]).astype(BF16)
    znat_ref[...] = _dot(xn, wnat_ref[...])
    zxy_ref[...] = _dot(xn, wxy_ref[...])


def _zproj(x, p, l):
    m = x.shape[0]
    lay3 = lambda i: (l, 0, 0)
    return pl.pallas_call(
        _zproj_kernel,
        out_shape=(jax.ShapeDtypeStruct((m, Z_NAT), F32), jax.ShapeDtypeStruct((m, Z_XY), F32)),
        grid=(1,),
        in_specs=[
            pl.BlockSpec((m, D_MODEL), lambda i: (0, 0)),
            pl.BlockSpec((None, 1, D_MODEL), lay3),
            pl.BlockSpec((None, D_MODEL, Z_NAT), lay3),
            pl.BlockSpec((None, D_MODEL, Z_XY), lay3),
        ],
        out_specs=(pl.BlockSpec((m, Z_NAT), lambda i: (0, 0)), pl.BlockSpec((m, Z_XY), lambda i: (0, 0))),
        compiler_params=_params("arbitrary"),
        name="zproj",
    )(x, p["mix_norm"], p["w_in"], p["w_xy"])


def _smixer_kernel(znat_ref, zxy_ref, x_ref, c0_ref, n0_ref, m0_ref, h0_ref, cv0_ref,
                   gbias_ref, onorm_ref, convw_ref, convb_ref, wai_ref, ba_ref, bi_ref, lam_ref,
                   wout_ref,
                   y_ref, c_ref, n_ref, m_ref, h_ref, cv_ref,
                   ubuf_s, hm_s, *, ns, t):
    c_ref[...] = c0_ref[...]
    n_ref[...] = n0_ref[...]
    m_ref[...] = m0_ref[...]
    pad = t + SUBLANES

    convs = []
    for s in range(ns):
        one = pl.ds(s, 1)
        _mlstm_blocks(znat_ref, gbias_ref, onorm_ref, c_ref.at[one], n_ref.at[one], m_ref.at[one], hm_s,
                      t, t, base=s * t)
        u = zxy_ref[s * t:(s + 1) * t, 0:LRU_WIDTH]
        ubuf_s[s * pad:s * pad + SUBLANES, :] = cv0_ref[s]
        ubuf_s[s * pad + SUBLANES:(s + 1) * pad, :] = u
        cv_ref[s] = ubuf_s[s * pad + t:(s + 1) * pad, :]
        conv = convb_ref[...] + u * convw_ref[CONV_W - 1:CONV_W, :]
        for j in range(1, CONV_W):
            conv = conv + (ubuf_s[pl.ds(s * pad + SUBLANES - j, t), :]
                           * convw_ref[CONV_W - 1 - j:CONV_W - j, :])
        convs.append(conv)
    a, xin = _lru_gates(jnp.concatenate(convs, axis=0), wai_ref, ba_ref, bi_ref, lam_ref)

    rmod = lax.broadcasted_iota(jnp.int32, (ns * t, LRU_WIDTH), 0) & (SUBLANES - 1)
    d = 1
    while d < SUBLANES:
        keep = rmod >= d
        xin = xin + a * jnp.where(keep, pltpu.roll(xin, d, axis=0), 0.0)
        a = a * jnp.where(keep, pltpu.roll(a, d, axis=0), 1.0)
        d *= 2
    groups = []
    for s in range(ns):
        carry = h0_ref[s]
        for gidx in range(t // SUBLANES):
            lo = s * t + gidx * SUBLANES
            hb = xin[lo:lo + SUBLANES, :] + a[lo:lo + SUBLANES, :] * carry
            carry = hb[SUBLANES - 1:SUBLANES, :]
            groups.append(hb)
        h_ref[s] = carry
    h_lru = jnp.concatenate(groups, axis=0)

    hl = (jax.nn.gelu(zxy_ref[:, LRU_WIDTH:]) * h_lru).astype(BF16)
    out = (_dot(hm_s[...], wout_ref[0:ML_WIDTH, :])
           + _dot(hl, wout_ref[ML_WIDTH:ML_WIDTH + LRU_WIDTH, :]))
    y_ref[...] = x_ref[...] + out


def _smixer(znat, zxy, x, st, p, l, *, nb, t, ns):
    tok = lambda b: (b, 0)
    lay3 = lambda b: (l, 0, 0)
    lay4 = lambda b: (l, 0, 0, 0)
    st3 = lambda b: (b, 0, 0)
    lst = lambda b: (l, b, 0, 0)
    c0, n0, m0, h0, cv0 = st
    return pl.pallas_call(
        functools.partial(_smixer_kernel, ns=ns, t=t),
        out_shape=(
            jax.ShapeDtypeStruct((nb * t, D_MODEL), F32),
            jax.ShapeDtypeStruct((nb, ML_HEADS, ML_DV, ML_DK), F32),
            jax.ShapeDtypeStruct((nb, ML_HEADS, ML_DK), F32),
            jax.ShapeDtypeStruct((nb, 1, LANES), F32),
            jax.ShapeDtypeStruct((nb, 1, LRU_WIDTH), F32),
            jax.ShapeDtypeStruct((nb, SUBLANES, LRU_WIDTH), F32),
        ),
        grid=(nb // ns,),
        in_specs=[
            pl.BlockSpec((ns * t, Z_NAT), tok),
            pl.BlockSpec((ns * t, Z_XY), tok),
            pl.BlockSpec((ns * t, D_MODEL), tok),
            pl.BlockSpec((None, ns, ML_HEADS, ML_DV, ML_DK), lambda b: (l, b, 0, 0, 0)),
            pl.BlockSpec((None, ns, ML_HEADS, ML_DK), lst),
            pl.BlockSpec((None, ns, 1, LANES), lst),
            pl.BlockSpec((None, ns, 1, LRU_WIDTH), lst),
            pl.BlockSpec((None, ns, SUBLANES, LRU_WIDTH), lst),
            pl.BlockSpec((None, 1, LANES), lay3),
            pl.BlockSpec((None, ML_HEADS, ML_DV), lay3),
            pl.BlockSpec((None, CONV_W, LRU_WIDTH), lay3),
            pl.BlockSpec((None, 1, LRU_WIDTH), lay3),
            pl.BlockSpec((None, LRU_BLOCKS, LRU_BW, 2 * LRU_BW), lay4),
            pl.BlockSpec((None, 1, LRU_WIDTH), lay3),
            pl.BlockSpec((None, 1, LRU_WIDTH), lay3),
            pl.BlockSpec((None, 1, LRU_WIDTH), lay3),
            pl.BlockSpec((None, ML_WIDTH + LRU_WIDTH, D_MODEL), lay3),
        ],
        out_specs=(
            pl.BlockSpec((ns * t, D_MODEL), tok),
            pl.BlockSpec((ns, ML_HEADS, ML_DV, ML_DK), lambda b: (b, 0, 0, 0)),
            pl.BlockSpec((ns, ML_HEADS, ML_DK), st3),
            pl.BlockSpec((ns, 1, LANES), st3),
            pl.BlockSpec((ns, 1, LRU_WIDTH), st3),
            pl.BlockSpec((ns, SUBLANES, LRU_WIDTH), st3),
        ),
        scratch_shapes=[pltpu.VMEM((ns * (t + SUBLANES), LRU_WIDTH), F32),
                        pltpu.VMEM((ns * t, ML_WIDTH), BF16)],
        compiler_params=_params("parallel"),
        name="smixer",
    )(znat, zxy, x, c0, n0, m0, h0, cv0, p["gbias"], p["ml_out_norm"], p["lru_conv_w"],
      p["lru_conv_b"], p["w_ai"], p["lru_b_a"], p["lru_b_i"], p["lru_lambda"], p["w_out"])


def _xattn_kernel(x_ref, g_ref, wq_ref, mk_ref, mv_ref, wo_ref, y_ref, o_s, *, ns, rows, split):
    step = rows // split
    chunks = [(s, slice(s * rows + c * step, s * rows + (c + 1) * step)) for s in range(ns) for c in range(split)]
    heads = [slice(h * XA_DH, (h + 1) * XA_DH) for h in range(XA_HEADS)]

    def scores(s, r):
        q = _dot(_rms(x_ref[r, :], g_ref[...]).astype(BF16), wq_ref[...])
        return [_nt_dot(q[:, c].astype(BF16), mk_ref[s, :, c].astype(BF16)) * (XA_DH ** -0.5) for c in heads]

    def attend(s, r, sc):
        for c, sch in zip(heads, sc):
            p = jnp.exp(sch - jnp.max(sch, axis=1, keepdims=True))
            p = p / jnp.sum(p, axis=1, keepdims=True)
            o_s[r, c] = _dot(p.astype(BF16), mv_ref[s, :, c].astype(BF16)).astype(BF16)

    def project(r):
        y_ref[r, :] = x_ref[r, :] + _dot(o_s[r, :], wo_ref[...])

    pending = None
    for s, r in chunks:
        sc = scores(s, r)
        if pending is not None:
            ps, pr, psc = pending
            attend(ps, pr, psc)
            project(pr)
        pending = (s, r, sc)
    ps, pr, psc = pending
    attend(ps, pr, psc)
    project(pr)


def _xattn(x, g, wq, mk, mv, wo, l, *, nb, t, tt, ns=1, split=1):
    nt = t // tt
    assert ns == 1 or nt == 1
    tok = lambda b, i: (b * nt + i, 0)
    lay3 = lambda b, i: (l, 0, 0)
    mem = lambda b, i: (l, b, 0, 0)
    return pl.pallas_call(
        functools.partial(_xattn_kernel, ns=ns, rows=tt, split=split),
        out_shape=jax.ShapeDtypeStruct((nb * t, D_MODEL), F32),
        grid=(nb // ns, nt),
        in_specs=[
            pl.BlockSpec((ns * tt, D_MODEL), tok),
            pl.BlockSpec((None, 1, D_MODEL), lay3),
            pl.BlockSpec((None, D_MODEL, D_MODEL), lay3),
            pl.BlockSpec((None, ns, MEM_LEN, D_MODEL), mem),
            pl.BlockSpec((None, ns, MEM_LEN, D_MODEL), mem),
            pl.BlockSpec((None, D_MODEL, D_MODEL), lay3),
        ],
        out_specs=pl.BlockSpec((ns * tt, D_MODEL), tok),
        scratch_shapes=[pltpu.VMEM((ns * tt, D_MODEL), BF16)],
        compiler_params=_params("parallel", "arbitrary"),
        name="xattn",
    )(x, g, wq, mk, mv, wo)


def _memkv_kernel(mem_ref, g_ref, wk_ref, wv_ref, k_ref, v_ref):
    mn = _rms(mem_ref[...], g_ref[...]).astype(BF16)
    k_ref[...] = _dot(mn, wk_ref[...])
    v_ref[...] = _dot(mn, wv_ref[...])


def _memkv(mem, g, wk, wv):
    nb = mem.shape[0]
    wspec = pl.BlockSpec((None, D_MODEL, D_MODEL), lambda l, b: (l, 0, 0))
    ospec = pl.BlockSpec((None, None, MEM_LEN, D_MODEL), lambda l, b: (l, b, 0, 0))
    out = jax.ShapeDtypeStruct((DEPTH, nb, MEM_LEN, D_MODEL), F32)
    return pl.pallas_call(
        _memkv_kernel,
        out_shape=(out, out),
        grid=(DEPTH, nb),
        in_specs=[pl.BlockSpec((None, MEM_LEN, D_MODEL), lambda l, b: (b, 0, 0)),
                  pl.BlockSpec((None, 1, D_MODEL), lambda l, b: (l, 0, 0)), wspec, wspec],
        out_specs=(ospec, ospec),
        compiler_params=_params("parallel", "parallel"),
        name="memkv",
    )(mem, g, wk, wv)


def kernel(x_prompt, x_sample, mem_prompt, state_mlstm_C, state_mlstm_n, state_mlstm_m, state_lru_h,
           state_lru_conv, cache_mem_k, cache_mem_v, ffn1_norm, ffn1_w_gate, ffn1_w_up, ffn1_w_down,
           mix_norm, w_in, ml_b_i, ml_b_f, ml_out_norm, lru_conv_w, lru_conv_b, lru_w_a, lru_b_a,
           lru_w_i, lru_b_i, lru_lambda, w_out, xattn_norm, mem_norm, xattn_w_q, xattn_w_k, xattn_w_v,
           xattn_w_o, ffn2_norm, ffn2_w_gate, ffn2_w_up, ffn2_w_down, final_norm):
    bp, tp, _ = x_prompt.shape
    bs, ts, _ = x_sample.shape
    xp = x_prompt.reshape(bp * tp, D_MODEL)
    xs = x_sample.reshape(bs * ts, D_MODEL)

    bf = lambda w: w.astype(BF16)
    row = lambda v: v.astype(F32).reshape(DEPTH, 1, -1)
    gate_pad = ((0, 0), (0, 0), (0, LANES - 2 * ML_HEADS))
    w_in_b = bf(w_in)
    p = {
        "mix_norm": row(mix_norm),
        "w_in": w_in_b,
        "w_xy": w_in_b[:, :, W_X:],
        "gbias": jnp.pad(jnp.concatenate([ml_b_i, ml_b_f], axis=1).astype(F32)[:, None, :], gate_pad),
        "ml_out_norm": ml_out_norm.astype(F32),
        "lru_conv_w": lru_conv_w, "lru_conv_b": row(lru_conv_b),
        "w_ai": bf(jnp.concatenate([lru_w_a, lru_w_i], axis=-1)),
        "lru_b_a": row(lru_b_a), "lru_b_i": row(lru_b_i), "lru_lambda": row(lru_lambda),
        "w_out": bf(w_out),
    }
    ffn1 = (row(ffn1_norm), bf(ffn1_w_gate), bf(ffn1_w_up), bf(ffn1_w_down))
    ffn2 = (row(ffn2_norm), bf(ffn2_w_gate), bf(ffn2_w_up), bf(ffn2_w_down))
    xa_g, xa_q, xa_o = row(xattn_norm), bf(xattn_w_q), bf(xattn_w_o)
    fin = final_norm.astype(F32).reshape(1, D_MODEL)

    pk, pv = _memkv(mem_prompt, row(mem_norm), bf(xattn_w_k), bf(xattn_w_v))
    sk = cache_mem_k.reshape(DEPTH, bs, MEM_LEN, D_MODEL)
    sv = cache_mem_v.reshape(DEPTH, bs, MEM_LEN, D_MODEL)
    s_state = (
        state_mlstm_C.astype(F32), state_mlstm_n.astype(F32),
        jnp.pad(state_mlstm_m.astype(F32)[:, :, None, :], ((0, 0), (0, 0), (0, 0), (0, LANES - ML_HEADS))),
        state_lru_h.astype(F32)[:, :, None, :],
        jnp.pad(state_lru_conv.astype(F32), ((0, 0), (0, 0), (SUBLANES - (CONV_W - 1), 0), (0, 0))),
    )

    p_out = [[] for _ in range(5)]
    s_out = [[] for _ in range(5)]
    for l in range(DEPTH):
        last = l == DEPTH - 1
        xp, xs = _ffn(xp, xs, *ffn1, fin, l, tm=512, tf=256, final_norm=False)
        xp, *st = _pmixer(xp, p, l, nb=bp, t=tp, tt=512, cl=ML_BLOCK)
        xp = _xattn(xp, xa_g, xa_q, pk, pv, xa_o, l, nb=bp, t=tp, tt=1024, split=4)
        for acc, v in zip(p_out, st):
            acc.append(v)
        znat, zxy = _zproj(xs, p, l)
        xs, *st = _smixer(znat, zxy, xs, s_state, p, l, nb=bs, t=ts, ns=4)
        xs = _xattn(xs, xa_g, xa_q, sk, sv, xa_o, l, nb=bs, t=ts, tt=ts, ns=4)
        for acc, v in zip(s_out, st):
            acc.append(v)
        xp, xs = _ffn(xp, xs, *ffn2, fin, l, tm=512, tf=256, final_norm=last)

    def states(acc):
        c, n, m, h, cv = (jnp.stack(a) for a in acc)
        return c, n, m[:, :, 0, :ML_HEADS], h[:, :, 0, :], cv[:, :, SUBLANES - (CONV_W - 1):, :]

    return (xp.reshape(bp, tp, D_MODEL), xs.reshape(bs, ts, D_MODEL),
            *states(p_out),
            pk.reshape(DEPTH, bp, MEM_LEN, XA_HEADS, XA_DH), pv.reshape(DEPTH, bp, MEM_LEN, XA_HEADS, XA_DH),
            *states(s_out))
```

```python
import functools

import jax
import jax.numpy as jnp
from jax import lax
from jax.experimental import pallas as pl
from jax.experimental.pallas import tpu as pltpu

F32 = jnp.float32
BF16 = jnp.bfloat16

D_MODEL = 1024
DEPTH = 2
ML_BLOCK = 256
ML_HEADS = 4
ML_DV = 256
ML_DK = 128
ML_WIDTH = ML_HEADS * ML_DV
LRU_WIDTH = 1024
LRU_BLOCKS = 8
LRU_BW = LRU_WIDTH // LRU_BLOCKS
CONV_W = 4
LRU_C = 8.0
MEM_LEN = 256
XA_HEADS = 4
XA_DH = D_MODEL // XA_HEADS
D_FF = 4 * D_MODEL
EPS = 1e-6

LANES = 128
SUBLANES = 8
VMEM_LIMIT = 56 * 1024 * 1024

Z_Q = 0
Z_K = Z_Q + ML_HEADS * ML_DK
Z_V = Z_K + ML_HEADS * ML_DK
Z_O = Z_V + ML_WIDTH
Z_G = Z_O + ML_WIDTH
Z_NAT = Z_G + LANES
Z_CHUNK = 768
Z_XY = 2 * LRU_WIDTH
W_X = Z_G + 2 * ML_HEADS


def _rms(x, g):
    return x * lax.rsqrt(jnp.mean(x * x, axis=-1, keepdims=True) + EPS) * g


def _nt_dot(a, b):
    return lax.dot_general(a, b, (((1,), (1,)), ((), ())), preferred_element_type=F32)


def _tn_dot(a, b):
    return lax.dot_general(a, b, (((0,), (0,)), ((), ())), preferred_element_type=F32)


def _dot(a, b):
    return jnp.dot(a, b, preferred_element_type=F32)


def _params(*sem):
    return pltpu.CompilerParams(dimension_semantics=sem, vmem_limit_bytes=VMEM_LIMIT)


def _resident(shape, index_map):
    return pl.BlockSpec(shape, index_map, pipeline_mode=pl.Buffered(1))


def _ffn_body(parts, g_ref, wg_ref, wu_ref, wd_ref, fg_ref, h_s, *, tf, final_norm):
    xs = [x_ref[...] for x_ref, _ in parts]
    rows = sum(x.shape[0] for x in xs)
    xn = jnp.concatenate([_rms(x, g_ref[...]).astype(BF16) for x in xs], axis=0)
    for j in range(D_FF // tf):
        cols = slice(j * tf, (j + 1) * tf)
        g = _dot(xn, wg_ref[:, cols])
        u = _dot(xn, wu_ref[:, cols])
        h_s[0:rows, cols] = (g * jax.nn.sigmoid(g) * u).astype(BF16)
    d = _dot(h_s[0:rows, :], wd_ref[...])
    lo = 0
    for x, (_, o_ref) in zip(xs, parts):
        y = x + 0.5 * d[lo:lo + x.shape[0], :]
        if final_norm:
            y = _rms(y, fg_ref[...])
        o_ref[...] = y
        lo += x.shape[0]


def _ffn_kernel(xp_ref, xs_ref, g_ref, wg_ref, wu_ref, wd_ref, fg_ref, op_ref, os_ref, h_s, **kw):
    last = pl.num_programs(0) - 1

    @pl.when(pl.program_id(0) < last)
    def _():
        _ffn_body([(xp_ref, op_ref)], g_ref, wg_ref, wu_ref, wd_ref, fg_ref, h_s, **kw)

    @pl.when(pl.program_id(0) == last)
    def _():
        _ffn_body([(xp_ref, op_ref), (xs_ref, os_ref)], g_ref, wg_ref, wu_ref, wd_ref, fg_ref, h_s, **kw)


def _ffn(xp, xs, g, wg, wu, wd, fg, l, *, tm, tf, final_norm):
    np_tiles = xp.shape[0] // tm
    ms = xs.shape[0]
    ptile = lambda i: (i, 0)
    whole = lambda i: (0, 0)
    lay = lambda i: (l, 0, 0)
    return pl.pallas_call(
        functools.partial(_ffn_kernel, tf=tf, final_norm=final_norm),
        out_shape=(jax.ShapeDtypeStruct(xp.shape, F32), jax.ShapeDtypeStruct(xs.shape, F32)),
        grid=(np_tiles,),
        in_specs=[
            pl.BlockSpec((tm, D_MODEL), ptile),
            _resident((ms, D_MODEL), whole),
            _resident((None, 1, D_MODEL), lay),
            _resident((None, D_MODEL, D_FF), lay),
            _resident((None, D_MODEL, D_FF), lay),
            _resident((None, D_FF, D_MODEL), lay),
            _resident((1, D_MODEL), whole),
        ],
        out_specs=(pl.BlockSpec((tm, D_MODEL), ptile), pl.BlockSpec((ms, D_MODEL), whole)),
        scratch_shapes=[pltpu.VMEM((tm + ms, D_FF), BF16)],
        compiler_params=_params("arbitrary"),
        name="ffn",
    )(xp, xs, g, wg, wu, wd, fg)


def _log_sigmoid(x):
    return jnp.minimum(x, 0.0) - jnp.log1p(jnp.exp(-jnp.abs(x)))


def _softplus(x):
    return jnp.maximum(x, 0.0) + jnp.log1p(jnp.exp(-jnp.abs(x)))


def _lane_pick(x, lane_ids, idx):
    return jnp.sum(jnp.where(lane_ids == idx, x, 0.0), axis=1, keepdims=True)


def _mlstm_blocks(z_ref, gbias_ref, onorm_ref, c_ref, n_ref, m_ref, hm_s, rows, cl, base=0):
    lane1 = lax.broadcasted_iota(jnp.int32, (1, LANES), 1)
    m_vec = m_ref[0]
    m_heads = [_lane_pick(m_vec, lane1, h) for h in range(ML_HEADS)]
    for c in range(rows // cl):
        m_heads = _mlstm_chunk(z_ref, gbias_ref, onorm_ref, c_ref, n_ref, m_heads, hm_s, base + c * cl, cl)
    for h in range(ML_HEADS):
        m_vec = jnp.where(lane1 == h, m_heads[h], m_vec)
    m_ref[0] = m_vec


def _mlstm_chunk(z_ref, gbias_ref, onorm_ref, c_ref, n_ref, m_heads, hm_s, r0, cl):
    lane = lax.broadcasted_iota(jnp.int32, (cl, LANES), 1)
    row = lax.broadcasted_iota(jnp.int32, (cl, LANES), 0)
    tril = (lax.broadcasted_iota(jnp.int32, (cl, cl), 1)
            <= lax.broadcasted_iota(jnp.int32, (cl, cl), 0))

    gates = z_ref[r0:r0 + cl, Z_G:Z_G + LANES] + gbias_ref[...]
    bsum = _log_sigmoid(gates)
    d = 1
    while d < cl:
        bsum = bsum + jnp.where(row >= d, pltpu.roll(bsum, d, axis=0), 0.0)
        d *= 2
    mixed = jnp.where(lane < ML_HEADS, gates, bsum)
    if cl < LANES:
        mixed = jnp.concatenate([mixed, jnp.zeros((LANES - cl, LANES), F32)], axis=0)
    mixed_t = mixed.T

    m_next = []
    for h in range(ML_HEADS):
        b_col = _lane_pick(bsum, lane, ML_HEADS + h)
        ig_col = _lane_pick(gates, lane, h)
        b_row = mixed_t[ML_HEADS + h:ML_HEADS + h + 1, 0:cl]
        ig_row = mixed_t[h:h + 1, 0:cl]
        m_prev = m_heads[h]

        dmat = jnp.where(tril, b_col - b_row + ig_row, -jnp.inf)
        inter = b_col + m_prev
        m_t = jnp.maximum(inter, jnp.max(dmat, axis=1, keepdims=True))

        qf = z_ref[r0:r0 + cl, Z_Q + h * ML_DK:Z_Q + (h + 1) * ML_DK] * (ML_DK ** -0.5)
        kf = z_ref[r0:r0 + cl, Z_K + h * ML_DK:Z_K + (h + 1) * ML_DK]
        vf = z_ref[r0:r0 + cl, Z_V + h * ML_DV:Z_V + (h + 1) * ML_DV]
        q = qf.astype(BF16)
        k = kf.astype(BF16)

        s = _nt_dot(q, k) * jnp.exp(dmat - m_t)
        w_inter = jnp.exp(inter - m_t)
        c_old = c_ref[0, h]
        n_old = n_ref[0, h:h + 1, :]
        num = w_inter * _nt_dot(q, c_old.astype(BF16)) + _dot(s.astype(BF16), vf.astype(BF16))
        den = (w_inter * jnp.sum(qf * n_old, axis=1, keepdims=True)
               + jnp.sum(s, axis=1, keepdims=True))
        hh = num / jnp.maximum(jnp.abs(den), jnp.exp(-m_t))

        hh = hh * lax.rsqrt(jnp.mean(hh * hh, axis=1, keepdims=True) + EPS) * onorm_ref[h:h + 1, :]
        o_gate = jax.nn.sigmoid(z_ref[r0:r0 + cl, Z_O + h * ML_DV:Z_O + (h + 1) * ML_DV])
        hm_s[r0:r0 + cl, h * ML_DV:(h + 1) * ML_DV] = (o_gate * hh).astype(BF16)

        m_new = m_t[cl - 1:cl, :]
        b_last = b_col[cl - 1:cl, :]
        wgt = jnp.exp(b_last - b_col + ig_col - m_new)
        decay = jnp.exp(b_last + m_prev - m_new)
        c_ref[0, h] = decay * c_old + _tn_dot((wgt * vf).astype(BF16), k)
        n_ref[0, h:h + 1, :] = decay * n_old + jnp.sum(wgt * kf, axis=0, keepdims=True)
        m_next.append(m_new)
    return m_next


def _lru_gates(conv, wai_ref, ba_ref, bi_ref, lam_ref):
    conv_b = conv.astype(BF16)
    pre = [_dot(conv_b[:, n * LRU_BW:(n + 1) * LRU_BW], wai_ref[n])
           for n in range(LRU_BLOCKS)]
    r_gate = jax.nn.sigmoid(jnp.concatenate([p[:, :LRU_BW] for p in pre], axis=1) + ba_ref[...])
    i_gate = jax.nn.sigmoid(jnp.concatenate([p[:, LRU_BW:] for p in pre], axis=1) + bi_ref[...])
    log_a = r_gate * (-LRU_C * _softplus(-lam_ref[...]))
    a = jnp.exp(log_a)
    w = -jnp.tanh(log_a) * (a * a + 1.0)
    xin = jnp.where(w > 0.0, w * lax.rsqrt(w), 0.0) * (i_gate * conv)
    return a, xin


def _pmixer_project(x_ref, gmix_ref, wxy_ref, zxy_dst, pbuf, xnb_s, *, tt):
    seg = tt // SUBLANES
    pitch = seg + SUBLANES
    xn = _rms(x_ref[...], gmix_ref[...])
    xnb_s[...] = xn.astype(BF16)
    for k in range(LRU_WIDTH // LANES):
        for s in range(SUBLANES):
            pbuf[k, pl.ds(s * pitch, seg), :] = xn[s * seg:(s + 1) * seg, k * LANES:(k + 1) * LANES]
    xnp = jnp.stack(
        [jnp.concatenate([pbuf[k, pl.ds(j, SUBLANES, stride=pitch), :]
                          for k in range(LRU_WIDTH // LANES)], axis=1) for j in range(seg)], axis=0)
    zxy_dst[...] = _dot(xnp.reshape(tt, D_MODEL).astype(BF16), wxy_ref[...])


def _pmixer_mix(x_ref, xnb_s, wnat_ref, z_s, zxy_s, gbias_ref, onorm_ref, convw_ref, convb_ref, wai_ref, ba_ref, bi_ref,
                lam_ref, wout_ref, y_ref, c_ref, n_ref, m_ref, h_ref, cv_ref, obuf, hm_s, *, cl, tt, zc):
    seg = tt // SUBLANES
    pitch = seg + SUBLANES

    u3 = zxy_s[:, 0:LRU_WIDTH].reshape(seg, SUBLANES, LRU_WIDTH)
    tail = cv_ref[0]
    sub = lax.broadcasted_iota(jnp.int32, (SUBLANES, LRU_WIDTH), 0)
    wrap = []
    for i in range(CONV_W - 1):
        prev = pltpu.roll(u3[seg - (CONV_W - 1) + i], 1, axis=0)
        fill = tail[SUBLANES - (CONV_W - 1) + i:SUBLANES - (CONV_W - 1) + i + 1, :]
        wrap.append(jnp.where(sub == 0, fill, prev))
        cv_ref[0, SUBLANES - (CONV_W - 1) + i:SUBLANES - (CONV_W - 1) + i + 1, :] = (
            u3[seg - (CONV_W - 1) + i][SUBLANES - 1:SUBLANES, :])
    ext = jnp.concatenate([jnp.stack(wrap, axis=0), u3], axis=0)
    conv3 = convb_ref[...] + ext[CONV_W - 1:] * convw_ref[CONV_W - 1:CONV_W, :]
    for j in range(1, CONV_W):
        conv3 = conv3 + ext[CONV_W - 1 - j:CONV_W - 1 - j + seg] * convw_ref[CONV_W - 1 - j:CONV_W - j, :]

    conv = conv3.reshape(tt, LRU_WIDTH)
    bounds = [min(k * zc, Z_NAT) for k in range(-(-Z_NAT // zc) + 1)]
    zcols = list(zip(bounds[:-1], bounds[1:]))

    def project(k):
        lo, hi = zcols[k]
        z_s[:, lo:hi] = _dot(xnb_s[...], wnat_ref[:, lo:hi])

    project(0)
    nrow = len(zcols) - 1
    step = tt // nrow
    a_parts, x_parts = [], []
    for c in range(nrow):
        r = slice(c * step, (c + 1) * step)
        a_c, x_c = _lru_gates(conv[r], wai_ref, ba_ref, bi_ref, lam_ref)
        a_parts.append(a_c)
        x_parts.append(x_c)
        project(c + 1)
    a = jnp.concatenate(a_parts, axis=0)
    xin = jnp.concatenate(x_parts, axis=0)

    a3 = a.reshape(seg, SUBLANES, LRU_WIDTH)
    x3 = xin.reshape(seg, SUBLANES, LRU_WIDTH)
    hs = [x3[0]]
    ps = [a3[0]]
    for j in range(1, seg):
        hs.append(a3[j] * hs[-1] + x3[j])
        ps.append(a3[j] * ps[-1])
    carry = h_ref[0]
    cin = []
    for s in range(SUBLANES):
        cin.append(carry)
        carry = hs[-1][s:s + 1, :] + ps[-1][s:s + 1, :] * carry
    h_ref[0] = carry
    cin = jnp.concatenate(cin, axis=0)
    h3 = jnp.stack([hs[j] + ps[j] * cin for j in range(seg)], axis=0)

    hl = (jax.nn.gelu(zxy_s[:, LRU_WIDTH:]) * h3.reshape(tt, LRU_WIDTH)).astype(BF16)
    ol3 = _dot(hl, wout_ref[ML_WIDTH:ML_WIDTH + LRU_WIDTH, :]).reshape(seg, SUBLANES, D_MODEL)
    for j in range(seg):
        for k in range(D_MODEL // LANES):
            obuf[k, pl.ds(j, SUBLANES, stride=pitch), :] = ol3[j][:, k * LANES:(k + 1) * LANES]
    out_lru = jnp.concatenate(
        [jnp.concatenate([obuf[k, pl.ds(s * pitch, seg), :] for s in range(SUBLANES)], axis=0)
         for k in range(D_MODEL // LANES)], axis=1)

    _mlstm_blocks(z_s, gbias_ref, onorm_ref, c_ref, n_ref, m_ref, hm_s, tt, cl)

    y_ref[...] = x_ref[...] + _dot(hm_s[...], wout_ref[0:ML_WIDTH, :]) + out_lru


def _pmixer_kernel(x_ref, gmix_ref, wnat_ref, wxy_ref, gbias_ref, onorm_ref, convw_ref,
                   convb_ref, wai_ref, ba_ref, bi_ref, lam_ref, wout_ref,
                   y_ref, c_ref, n_ref, m_ref, h_ref, cv_ref,
                   z_s, zxy_s, pbuf, obuf, hm_s, xnb_s, *, cl, tt, zc):
    @pl.when(pl.program_id(1) == 0)
    def _():
        c_ref[...] = jnp.zeros_like(c_ref)
        n_ref[...] = jnp.zeros_like(n_ref)
        m_ref[...] = jnp.zeros_like(m_ref)
        h_ref[...] = jnp.zeros_like(h_ref)
        cv_ref[...] = jnp.zeros_like(cv_ref)

    _pmixer_project(x_ref, gmix_ref, wxy_ref, zxy_s, pbuf, xnb_s, tt=tt)
    _pmixer_mix(x_ref, xnb_s, wnat_ref, z_s, zxy_s, gbias_ref, onorm_ref, convw_ref, convb_ref, wai_ref, ba_ref,
                bi_ref, lam_ref, wout_ref, y_ref, c_ref, n_ref, m_ref, h_ref, cv_ref, obuf, hm_s,
                cl=cl, tt=tt, zc=zc)


def _pmixer(x, p, l, *, nb, t, tt, cl, zc=Z_CHUNK):
    nt = t // tt
    pitch = tt // SUBLANES + SUBLANES
    tok = lambda b, i: (b * nt + i, 0)
    st4 = lambda b, i: (b, 0, 0, 0)
    st3 = lambda b, i: (b, 0, 0)
    lay3 = lambda b, i: (l, 0, 0)
    lay4 = lambda b, i: (l, 0, 0, 0)
    return pl.pallas_call(
        functools.partial(_pmixer_kernel, cl=cl, tt=tt, zc=zc),
        out_shape=(
            jax.ShapeDtypeStruct((nb * t, D_MODEL), F32),
            jax.ShapeDtypeStruct((nb, ML_HEADS, ML_DV, ML_DK), F32),
            jax.ShapeDtypeStruct((nb, ML_HEADS, ML_DK), F32),
            jax.ShapeDtypeStruct((nb, 1, LANES), F32),
            jax.ShapeDtypeStruct((nb, 1, LRU_WIDTH), F32),
            jax.ShapeDtypeStruct((nb, SUBLANES, LRU_WIDTH), F32),
        ),
        grid=(nb, nt),
        in_specs=[
            pl.BlockSpec((tt, D_MODEL), tok),
            _resident((None, 1, D_MODEL), lay3),
            _resident((None, D_MODEL, Z_NAT), lay3),
            _resident((None, D_MODEL, Z_XY), lay3),
            _resident((None, 1, LANES), lay3),
            _resident((None, ML_HEADS, ML_DV), lay3),
            _resident((None, CONV_W, LRU_WIDTH), lay3),
            _resident((None, 1, LRU_WIDTH), lay3),
            _resident((None, LRU_BLOCKS, LRU_BW, 2 * LRU_BW), lay4),
            _resident((None, 1, LRU_WIDTH), lay3),
            _resident((None, 1, LRU_WIDTH), lay3),
            _resident((None, 1, LRU_WIDTH), lay3),
            _resident((None, ML_WIDTH + LRU_WIDTH, D_MODEL), lay3),
        ],
        out_specs=(
            pl.BlockSpec((tt, D_MODEL), tok),
            pl.BlockSpec((1, ML_HEADS, ML_DV, ML_DK), st4),
            pl.BlockSpec((1, ML_HEADS, ML_DK), st3),
            pl.BlockSpec((1, 1, LANES), st3),
            pl.BlockSpec((1, 1, LRU_WIDTH), st3),
            pl.BlockSpec((1, SUBLANES, LRU_WIDTH), st3),
        ),
        scratch_shapes=[pltpu.VMEM((tt, Z_NAT), F32), pltpu.VMEM((tt, Z_XY), F32),
                        pltpu.VMEM((LRU_WIDTH // LANES, SUBLANES * pitch, LANES), F32),
                        pltpu.VMEM((D_MODEL // LANES, SUBLANES * pitch, LANES), F32),
                        pltpu.VMEM((tt, ML_WIDTH), BF16), pltpu.VMEM((tt, D_MODEL), BF16)],
        compiler_params=_params("parallel", "arbitrary"),
        name="pmixer",
    )(x, p["mix_norm"], p["w_in"], p["w_xy"], p["gbias"], p["ml_out_norm"], p["lru_conv_w"],
      p["lru_conv_b"], p["w_ai"], p["lru_b_a"], p["lru_b_i"], p["lru_lambda"], p["w_out"])


def _zproj_kernel(x_ref, g_ref, wnat_ref, wxy_ref, znat_ref, zxy_ref):
    xn = _rms(x_ref[...], g_ref[...]).astype(BF16)
    znat_ref[...] = _dot(xn, wnat_ref[...])
    zxy_ref[...] = _dot(xn, wxy_ref[...])


def _zproj(x, p, l):
    m = x.shape[0]
    lay3 = lambda i: (l, 0, 0)
    return pl.pallas_call(
        _zproj_kernel,
        out_shape=(jax.ShapeDtypeStruct((m, Z_NAT), F32), jax.ShapeDtypeStruct((m, Z_XY), F32)),
        grid=(1,),
        in_specs=[
            pl.BlockSpec((m, D_MODEL), lambda i: (0, 0)),
            pl.BlockSpec((None, 1, D_MODEL), lay3),
            pl.BlockSpec((None, D_MODEL, Z_NAT), lay3),
            pl.BlockSpec((None, D_MODEL, Z_XY), lay3),
        ],
        out_specs=(pl.BlockSpec((m, Z_NAT), lambda i: (0, 0)), pl.BlockSpec((m, Z_XY), lambda i: (0, 0))),
        compiler_params=_params("arbitrary"),
        name="zproj",
    )(x, p["mix_norm"], p["w_in"], p["w_xy"])


def _smixer_kernel(znat_ref, zxy_ref, x_ref, c0_ref, n0_ref, m0_ref, h0_ref, cv0_ref,
                   gbias_ref, onorm_ref, convw_ref, convb_ref, wai_ref, ba_ref, bi_ref, lam_ref,
                   wout_ref,
                   y_ref, c_ref, n_ref, m_ref, h_ref, cv_ref,
                   ubuf_s, hm_s, *, ns, t):
    c_ref[...] = c0_ref[...]
    n_ref[...] = n0_ref[...]
    m_ref[...] = m0_ref[...]
    pad = t + SUBLANES

    convs = []
    for s in range(ns):
        one = pl.ds(s, 1)
        _mlstm_blocks(znat_ref, gbias_ref, onorm_ref, c_ref.at[one], n_ref.at[one], m_ref.at[one], hm_s,
                      t, t, base=s * t)
        u = zxy_ref[s * t:(s + 1) * t, 0:LRU_WIDTH]
        ubuf_s[s * pad:s * pad + SUBLANES, :] = cv0_ref[s]
        ubuf_s[s * pad + SUBLANES:(s + 1) * pad, :] = u
        cv_ref[s] = ubuf_s[s * pad + t:(s + 1) * pad, :]
        conv = convb_ref[...] + u * convw_ref[CONV_W - 1:CONV_W, :]
        for j in range(1, CONV_W):
            conv = conv + (ubuf_s[pl.ds(s * pad + SUBLANES - j, t), :]
                           * convw_ref[CONV_W - 1 - j:CONV_W - j, :])
        convs.append(conv)
    a, xin = _lru_gates(jnp.concatenate(convs, axis=0), wai_ref, ba_ref, bi_ref, lam_ref)

    rmod = lax.broadcasted_iota(jnp.int32, (ns * t, LRU_WIDTH), 0) & (SUBLANES - 1)
    d = 1
    while d < SUBLANES:
        keep = rmod >= d
        xin = xin + a * jnp.where(keep, pltpu.roll(xin, d, axis=0), 0.0)
        a = a * jnp.where(keep, pltpu.roll(a, d, axis=0), 1.0)
        d *= 2
    groups = []
    for s in range(ns):
        carry = h0_ref[s]
        for gidx in range(t // SUBLANES):
            lo = s * t + gidx * SUBLANES
            hb = xin[lo:lo + SUBLANES, :] + a[lo:lo + SUBLANES, :] * carry
            carry = hb[SUBLANES - 1:SUBLANES, :]
            groups.append(hb)
        h_ref[s] = carry
    h_lru = jnp.concatenate(groups, axis=0)

    hl = (jax.nn.gelu(zxy_ref[:, LRU_WIDTH:]) * h_lru).astype(BF16)
    out = (_dot(hm_s[...], wout_ref[0:ML_WIDTH, :])
           + _dot(hl, wout_ref[ML_WIDTH:ML_WIDTH + LRU_WIDTH, :]))
    y_ref[...] = x_ref[...] + out


def _smixer(znat, zxy, x, st, p, l, *, nb, t, ns):
    tok = lambda b: (b, 0)
    lay3 = lambda b: (l, 0, 0)
    lay4 = lambda b: (l, 0, 0, 0)
    st3 = lambda b: (b, 0, 0)
    lst = lambda b: (l, b, 0, 0)
    c0, n0, m0, h0, cv0 = st
    return pl.pallas_call(
        functools.partial(_smixer_kernel, ns=ns, t=t),
        out_shape=(
            jax.ShapeDtypeStruct((nb * t, D_MODEL), F32),
            jax.ShapeDtypeStruct((nb, ML_HEADS, ML_DV, ML_DK), F32),
            jax.ShapeDtypeStruct((nb, ML_HEADS, ML_DK), F32),
            jax.ShapeDtypeStruct((nb, 1, LANES), F32),
            jax.ShapeDtypeStruct((nb, 1, LRU_WIDTH), F32),
            jax.ShapeDtypeStruct((nb, SUBLANES, LRU_WIDTH), F32),
        ),
        grid=(nb // ns,),
        in_specs=[
            pl.BlockSpec((ns * t, Z_NAT), tok),
            pl.BlockSpec((ns * t, Z_XY), tok),
            pl.BlockSpec((ns * t, D_MODEL), tok),
            pl.BlockSpec((None, ns, ML_HEADS, ML_DV, ML_DK), lambda b: (l, b, 0, 0, 0)),
            pl.BlockSpec((None, ns, ML_HEADS, ML_DK), lst),
            pl.BlockSpec((None, ns, 1, LANES), lst),
            pl.BlockSpec((None, ns, 1, LRU_WIDTH), lst),
            pl.BlockSpec((None, ns, SUBLANES, LRU_WIDTH), lst),
            pl.BlockSpec((None, 1, LANES), lay3),
            pl.BlockSpec((None, ML_HEADS, ML_DV), lay3),
            pl.BlockSpec((None, CONV_W, LRU_WIDTH), lay3),
            pl.BlockSpec((None, 1, LRU_WIDTH), lay3),
            pl.BlockSpec((None, LRU_BLOCKS, LRU_BW, 2 * LRU_BW), lay4),
            pl.BlockSpec((None, 1, LRU_WIDTH), lay3),
            pl.BlockSpec((None, 1, LRU_WIDTH), lay3),
            pl.BlockSpec((None, 1, LRU_WIDTH), lay3),
            pl.BlockSpec((None, ML_WIDTH + LRU_WIDTH, D_MODEL), lay3),
        ],
        out_specs=(
            pl.BlockSpec((ns * t, D_MODEL), tok),
            pl.BlockSpec((ns, ML_HEADS, ML_DV, ML_DK), lambda b: (b, 0, 0, 0)),
            pl.BlockSpec((ns, ML_HEADS, ML_DK), st3),
            pl.BlockSpec((ns, 1, LANES), st3),
            pl.BlockSpec((ns, 1, LRU_WIDTH), st3),
            pl.BlockSpec((ns, SUBLANES, LRU_WIDTH), st3),
        ),
        scratch_shapes=[pltpu.VMEM((ns * (t + SUBLANES), LRU_WIDTH), F32),
                        pltpu.VMEM((ns * t, ML_WIDTH), BF16)],
        compiler_params=_params("parallel"),
        name="smixer",
    )(znat, zxy, x, c0, n0, m0, h0, cv0, p["gbias"], p["ml_out_norm"], p["lru_conv_w"],
      p["lru_conv_b"], p["w_ai"], p["lru_b_a"], p["lru_b_i"], p["lru_lambda"], p["w_out"])


def _xattn_kernel(x_ref, g_ref, wq_ref, mk_ref, mv_ref, wo_ref, y_ref, o_s, *, ns, rows, split):
    step = rows // split
    chunks = [(s, slice(s * rows + c * step, s * rows + (c + 1) * step)) for s in range(ns) for c in range(split)]
    heads = [slice(h * XA_DH, (h + 1) * XA_DH) for h in range(XA_HEADS)]

    def scores(s, r):
        q = _dot(_rms(x_ref[r, :], g_ref[...]).astype(BF16), wq_ref[...])
        return [_nt_dot(q[:, c].astype(BF16), mk_ref[s, :, c].astype(BF16)) * (XA_DH ** -0.5) for c in heads]

    def attend(s, r, sc):
        for c, sch in zip(heads, sc):
            p = jnp.exp(sch - jnp.max(sch, axis=1, keepdims=True))
            p = p / jnp.sum(p, axis=1, keepdims=True)
            o_s[r, c] = _dot(p.astype(BF16), mv_ref[s, :, c].astype(BF16)).astype(BF16)

    def project(r):
        y_ref[r, :] = x_ref[r, :] + _dot(o_s[r, :], wo_ref[...])

    pending = None
    for s, r in chunks:
        sc = scores(s, r)
        if pending is not None:
            ps, pr, psc = pending
            attend(ps, pr, psc)
            project(pr)
        pending = (s, r, sc)
    ps, pr, psc = pending
    attend(ps, pr, psc)
    project(pr)


def _xattn(x, g, wq, mk, mv, wo, l, *, nb, t, tt, ns=1, split=1):
    nt = t // tt
    assert ns == 1 or nt == 1
    tok = lambda b, i: (b * nt + i, 0)
    lay3 = lambda b, i: (l, 0, 0)
    mem = lambda b, i: (l, b, 0, 0)
    return pl.pallas_call(
        functools.partial(_xattn_kernel, ns=ns, rows=tt, split=split),
        out_shape=jax.ShapeDtypeStruct((nb * t, D_MODEL), F32),
        grid=(nb // ns, nt),
        in_specs=[
            pl.BlockSpec((ns * tt, D_MODEL), tok),
            pl.BlockSpec((None, 1, D_MODEL), lay3),
            pl.BlockSpec((None, D_MODEL, D_MODEL), lay3),
            pl.BlockSpec((None, ns, MEM_LEN, D_MODEL), mem),
            pl.BlockSpec((None, ns, MEM_LEN, D_MODEL), mem),
            pl.BlockSpec((None, D_MODEL, D_MODEL), lay3),
        ],
        out_specs=pl.BlockSpec((ns * tt, D_MODEL), tok),
        scratch_shapes=[pltpu.VMEM((ns * tt, D_MODEL), BF16)],
        compiler_params=_params("parallel", "arbitrary"),
        name="xattn",
    )(x, g, wq, mk, mv, wo)


def _memkv_kernel(mem_ref, g_ref, wk_ref, wv_ref, k_ref, v_ref):
    mn = _rms(mem_ref[...], g_ref[...]).astype(BF16)
    k_ref[...] = _dot(mn, wk_ref[...])
    v_ref[...] = _dot(mn, wv_ref[...])


def _memkv(mem, g, wk, wv):
    nb = mem.shape[0]
    wspec = pl.BlockSpec((None, D_MODEL, D_MODEL), lambda l, b: (l, 0, 0))
    ospec = pl.BlockSpec((None, None, MEM_LEN, D_MODEL), lambda l, b: (l, b, 0, 0))
    out = jax.ShapeDtypeStruct((DEPTH, nb, MEM_LEN, D_MODEL), F32)
    return pl.pallas_call(
        _memkv_kernel,
        out_shape=(out, out),
        grid=(DEPTH, nb),
        in_specs=[pl.BlockSpec((None, MEM_LEN, D_MODEL), lambda l, b: (b, 0, 0)),
                  pl.BlockSpec((None, 1, D_MODEL), lambda l, b: (l, 0, 0)), wspec, wspec],
        out_specs=(ospec, ospec),
        compiler_params=_params("parallel", "parallel"),
        name="memkv",
    )(mem, g, wk, wv)


def _wxy_kernel(a_ref, b_ref, c_ref, o_ref):
    sh = W_X - Z_G
    a, b, c = a_ref[...], b_ref[...], c_ref[...]
    o_ref[:, 0:LRU_WIDTH] = jnp.concatenate([a[:, sh:], b[:, 0:sh]], axis=1).astype(BF16)
    o_ref[:, LRU_WIDTH:Z_XY] = jnp.concatenate([b[:, sh:], c[:, 0:sh]], axis=1).astype(BF16)


def _wxy(w_in):
    blk = lambda k: pl.BlockSpec((None, D_MODEL, LRU_WIDTH), lambda l, k=k: (l, 0, Z_G // LRU_WIDTH + k))
    return pl.pallas_call(
        _wxy_kernel,
        out_shape=jax.ShapeDtypeStruct((DEPTH, D_MODEL, Z_XY), BF16),
        grid=(DEPTH,),
        in_specs=[blk(0), blk(1), blk(2)],
        out_specs=pl.BlockSpec((None, D_MODEL, Z_XY), lambda l: (l, 0, 0)),
        compiler_params=_params("parallel"),
        name="wxy",
    )(w_in, w_in, w_in)


def kernel(x_prompt, x_sample, mem_prompt, state_mlstm_C, state_mlstm_n, state_mlstm_m, state_lru_h,
           state_lru_conv, cache_mem_k, cache_mem_v, ffn1_norm, ffn1_w_gate, ffn1_w_up, ffn1_w_down,
           mix_norm, w_in, ml_b_i, ml_b_f, ml_out_norm, lru_conv_w, lru_conv_b, lru_w_a, lru_b_a,
           lru_w_i, lru_b_i, lru_lambda, w_out, xattn_norm, mem_norm, xattn_w_q, xattn_w_k, xattn_w_v,
           xattn_w_o, ffn2_norm, ffn2_w_gate, ffn2_w_up, ffn2_w_down, final_norm):
    bp, tp, _ = x_prompt.shape
    bs, ts, _ = x_sample.shape
    xp = x_prompt.reshape(bp * tp, D_MODEL)
    xs = x_sample.reshape(bs * ts, D_MODEL)

    bf = lambda w: w.astype(BF16)
    row = lambda v: v.astype(F32).reshape(DEPTH, 1, -1)
    gate_pad = ((0, 0), (0, 0), (0, LANES - 2 * ML_HEADS))
    p = {
        "mix_norm": row(mix_norm),
        "w_in": bf(w_in[:, :, 0:Z_NAT]),
        "w_xy": _wxy(w_in),
        "gbias": jnp.pad(jnp.concatenate([ml_b_i, ml_b_f], axis=1).astype(F32)[:, None, :], gate_pad),
        "ml_out_norm": ml_out_norm.astype(F32),
        "lru_conv_w": lru_conv_w, "lru_conv_b": row(lru_conv_b),
        "w_ai": bf(jnp.concatenate([lru_w_a, lru_w_i], axis=-1)),
        "lru_b_a": row(lru_b_a), "lru_b_i": row(lru_b_i), "lru_lambda": row(lru_lambda),
        "w_out": bf(w_out),
    }
    ffn1 = (row(ffn1_norm), bf(ffn1_w_gate), bf(ffn1_w_up), bf(ffn1_w_down))
    ffn2 = (row(ffn2_norm), bf(ffn2_w_gate), bf(ffn2_w_up), bf(ffn2_w_down))
    xa_g, xa_q, xa_o = row(xattn_norm), bf(xattn_w_q), bf(xattn_w_o)
    fin = final_norm.astype(F32).reshape(1, D_MODEL)

    pk, pv = _memkv(mem_prompt, row(mem_norm), bf(xattn_w_k), bf(xattn_w_v))
    sk = cache_mem_k.reshape(DEPTH, bs, MEM_LEN, D_MODEL)
    sv = cache_mem_v.reshape(DEPTH, bs, MEM_LEN, D_MODEL)
    s_state = (
        state_mlstm_C.astype(F32), state_mlstm_n.astype(F32),
        jnp.pad(state_mlstm_m.astype(F32)[:, :, None, :], ((0, 0), (0, 0), (0, 0), (0, LANES - ML_HEADS))),
        state_lru_h.astype(F32)[:, :, None, :],
        jnp.pad(state_lru_conv.astype(F32), ((0, 0), (0, 0), (SUBLANES - (CONV_W - 1), 0), (0, 0))),
    )

    p_out = [[] for _ in range(5)]
    s_out = [[] for _ in range(5)]
    for l in range(DEPTH):
        last = l == DEPTH - 1
        xp, xs = _ffn(xp, xs, *ffn1, fin, l, tm=512, tf=256, final_norm=False)
        xp, *st = _pmixer(xp, p, l, nb=bp, t=tp, tt=512, cl=ML_BLOCK)
        xp = _xattn(xp, xa_g, xa_q, pk, pv, xa_o, l, nb=bp, t=tp, tt=1024, split=4)
        for acc, v in zip(p_out, st):
            acc.append(v)
        znat, zxy = _zproj(xs, p, l)
        xs, *st = _smixer(znat, zxy, xs, s_state, p, l, nb=bs, t=ts, ns=4)
        xs = _xattn(xs, xa_g, xa_q, sk, sv, xa_o, l, nb=bs, t=ts, tt=ts, ns=4)
        for acc, v in zip(s_out, st):
            acc.append(v)
        xp, xs = _ffn(xp, xs, *ffn2, fin, l, tm=512, tf=256, final_norm=last)

    def states(acc):
        c, n, m, h, cv = (jnp.stack(a) for a in acc)
        return c, n, m[:, :, 0, :ML_HEADS], h[:, :, 0, :], cv[:, :, SUBLANES - (CONV_W - 1):, :]

    return (xp.reshape(bp, tp, D_MODEL), xs.reshape(bs, ts, D_MODEL),
            *states(p_out),
            pk.reshape(DEPTH, bp, MEM_LEN, XA_HEADS, XA_DH), pv.reshape(DEPTH, bp, MEM_LEN, XA_HEADS, XA_DH),
            *states(s_out))
```

```python
import functools

import jax
import jax.numpy as jnp
from jax import lax
from jax.experimental import pallas as pl
from jax.experimental.pallas import tpu as pltpu

F32 = jnp.float32
BF16 = jnp.bfloat16

D_MODEL = 1024
DEPTH = 2
ML_BLOCK = 256
ML_HEADS = 4
ML_DV = 256
ML_DK = 128
ML_WIDTH = ML_HEADS * ML_DV
LRU_WIDTH = 1024
LRU_BLOCKS = 8
LRU_BW = LRU_WIDTH // LRU_BLOCKS
CONV_W = 4
LRU_C = 8.0
MEM_LEN = 256
XA_HEADS = 4
XA_DH = D_MODEL // XA_HEADS
D_FF = 4 * D_MODEL
EPS = 1e-6

LANES = 128
SUBLANES = 8
VMEM_LIMIT = 56 * 1024 * 1024

Z_Q = 0
Z_K = Z_Q + ML_HEADS * ML_DK
Z_V = Z_K + ML_HEADS * ML_DK
Z_O = Z_V + ML_WIDTH
Z_G = Z_O + ML_WIDTH
Z_NAT = Z_G + LANES
Z_CHUNK = 768
Z_XY = 2 * LRU_WIDTH
W_X = Z_G + 2 * ML_HEADS


def _rms(x, g):
    return x * lax.rsqrt(jnp.mean(x * x, axis=-1, keepdims=True) + EPS) * g


def _nt_dot(a, b):
    return lax.dot_general(a, b, (((1,), (1,)), ((), ())), preferred_element_type=F32)


def _tn_dot(a, b):
    return lax.dot_general(a, b, (((0,), (0,)), ((), ())), preferred_element_type=F32)


def _dot(a, b):
    return jnp.dot(a, b, preferred_element_type=F32)


def _params(*sem):
    return pltpu.CompilerParams(dimension_semantics=sem, vmem_limit_bytes=VMEM_LIMIT)


def _resident(shape, index_map):
    return pl.BlockSpec(shape, index_map, pipeline_mode=pl.Buffered(1))


def _ffn_body(parts, g_ref, wg_ref, wu_ref, wd_ref, fg_ref, h_s, *, tf, final_norm):
    xs = [x_ref[...] for x_ref, _ in parts]
    rows = sum(x.shape[0] for x in xs)
    xn = jnp.concatenate([_rms(x, g_ref[...]).astype(BF16) for x in xs], axis=0)
    for j in range(D_FF // tf):
        cols = slice(j * tf, (j + 1) * tf)
        g = _dot(xn, wg_ref[:, cols])
        u = _dot(xn, wu_ref[:, cols])
        h_s[0:rows, cols] = (g * jax.nn.sigmoid(g) * u).astype(BF16)
    d = _dot(h_s[0:rows, :], wd_ref[...])
    lo = 0
    for x, (_, o_ref) in zip(xs, parts):
        y = x + 0.5 * d[lo:lo + x.shape[0], :]
        if final_norm:
            y = _rms(y, fg_ref[...])
        o_ref[...] = y
        lo += x.shape[0]


def _ffn_kernel(xp_ref, xs_ref, g_ref, wg_ref, wu_ref, wd_ref, fg_ref, op_ref, os_ref, h_s, **kw):
    last = pl.num_programs(0) - 1

    @pl.when(pl.program_id(0) < last)
    def _():
        _ffn_body([(xp_ref, op_ref)], g_ref, wg_ref, wu_ref, wd_ref, fg_ref, h_s, **kw)

    @pl.when(pl.program_id(0) == last)
    def _():
        _ffn_body([(xp_ref, op_ref), (xs_ref, os_ref)], g_ref, wg_ref, wu_ref, wd_ref, fg_ref, h_s, **kw)


def _ffn(xp, xs, g, wg, wu, wd, fg, l, *, tm, tf, final_norm):
    np_tiles = xp.shape[0] // tm
    ms = xs.shape[0]
    ptile = lambda i: (i, 0)
    whole = lambda i: (0, 0)
    lay = lambda i: (l, 0, 0)
    return pl.pallas_call(
        functools.partial(_ffn_kernel, tf=tf, final_norm=final_norm),
        out_shape=(jax.ShapeDtypeStruct(xp.shape, F32), jax.ShapeDtypeStruct(xs.shape, F32)),
        grid=(np_tiles,),
        in_specs=[
            pl.BlockSpec((tm, D_MODEL), ptile),
            _resident((ms, D_MODEL), whole),
            _resident((None, 1, D_MODEL), lay),
            _resident((None, D_MODEL, D_FF), lay),
            _resident((None, D_MODEL, D_FF), lay),
            _resident((None, D_FF, D_MODEL), lay),
            _resident((1, D_MODEL), whole),
        ],
        out_specs=(pl.BlockSpec((tm, D_MODEL), ptile), pl.BlockSpec((ms, D_MODEL), whole)),
        scratch_shapes=[pltpu.VMEM((tm + ms, D_FF), BF16)],
        compiler_params=_params("arbitrary"),
        name="ffn",
    )(xp, xs, g, wg, wu, wd, fg)


def _log_sigmoid(x):
    return jnp.minimum(x, 0.0) - jnp.log1p(jnp.exp(-jnp.abs(x)))


def _softplus(x):
    return jnp.maximum(x, 0.0) + jnp.log1p(jnp.exp(-jnp.abs(x)))


def _lane_pick(x, lane_ids, idx):
    return jnp.sum(jnp.where(lane_ids == idx, x, 0.0), axis=1, keepdims=True)


def _mlstm_blocks(z_ref, gbias_ref, onorm_ref, c_ref, n_ref, m_ref, hm_s, rows, cl, base=0):
    lane1 = lax.broadcasted_iota(jnp.int32, (1, LANES), 1)
    m_vec = m_ref[0]
    m_heads = [_lane_pick(m_vec, lane1, h) for h in range(ML_HEADS)]
    for c in range(rows // cl):
        m_heads = _mlstm_chunk(z_ref, gbias_ref, onorm_ref, c_ref, n_ref, m_heads, hm_s, base + c * cl, cl)
    for h in range(ML_HEADS):
        m_vec = jnp.where(lane1 == h, m_heads[h], m_vec)
    m_ref[0] = m_vec


def _mlstm_chunk(z_ref, gbias_ref, onorm_ref, c_ref, n_ref, m_heads, hm_s, r0, cl):
    lane = lax.broadcasted_iota(jnp.int32, (cl, LANES), 1)
    row = lax.broadcasted_iota(jnp.int32, (cl, LANES), 0)
    tril = (lax.broadcasted_iota(jnp.int32, (cl, cl), 1)
            <= lax.broadcasted_iota(jnp.int32, (cl, cl), 0))

    gates = z_ref[r0:r0 + cl, Z_G:Z_G + LANES] + gbias_ref[...]
    bsum = _log_sigmoid(gates)
    d = 1
    while d < cl:
        bsum = bsum + jnp.where(row >= d, pltpu.roll(bsum, d, axis=0), 0.0)
        d *= 2
    mixed = jnp.where(lane < ML_HEADS, gates, bsum)
    if cl < LANES:
        mixed = jnp.concatenate([mixed, jnp.zeros((LANES - cl, LANES), F32)], axis=0)
    mixed_t = mixed.T

    m_next = []
    for h in range(ML_HEADS):
        b_col = _lane_pick(bsum, lane, ML_HEADS + h)
        ig_col = _lane_pick(gates, lane, h)
        b_row = mixed_t[ML_HEADS + h:ML_HEADS + h + 1, 0:cl]
        ig_row = mixed_t[h:h + 1, 0:cl]
        m_prev = m_heads[h]

        dmat = jnp.where(tril, b_col - b_row + ig_row, -jnp.inf)
        inter = b_col + m_prev
        m_t = jnp.maximum(inter, jnp.max(dmat, axis=1, keepdims=True))

        qf = z_ref[r0:r0 + cl, Z_Q + h * ML_DK:Z_Q + (h + 1) * ML_DK] * (ML_DK ** -0.5)
        kf = z_ref[r0:r0 + cl, Z_K + h * ML_DK:Z_K + (h + 1) * ML_DK]
        vf = z_ref[r0:r0 + cl, Z_V + h * ML_DV:Z_V + (h + 1) * ML_DV]
        q = qf.astype(BF16)
        k = kf.astype(BF16)

        s = _nt_dot(q, k) * jnp.exp(dmat - m_t)
        w_inter = jnp.exp(inter - m_t)
        c_old = c_ref[0, h]
        n_old = n_ref[0, h:h + 1, :]
        num = w_inter * _nt_dot(q, c_old.astype(BF16)) + _dot(s.astype(BF16), vf.astype(BF16))
        den = (w_inter * jnp.sum(qf * n_old, axis=1, keepdims=True)
               + jnp.sum(s, axis=1, keepdims=True))
        hh = num / jnp.maximum(jnp.abs(den), jnp.exp(-m_t))

        hh = hh * lax.rsqrt(jnp.mean(hh * hh, axis=1, keepdims=True) + EPS) * onorm_ref[h:h + 1, :]
        o_gate = jax.nn.sigmoid(z_ref[r0:r0 + cl, Z_O + h * ML_DV:Z_O + (h + 1) * ML_DV])
        hm_s[r0:r0 + cl, h * ML_DV:(h + 1) * ML_DV] = (o_gate * hh).astype(BF16)

        m_new = m_t[cl - 1:cl, :]
        b_last = b_col[cl - 1:cl, :]
        wgt = jnp.exp(b_last - b_col + ig_col - m_new)
        decay = jnp.exp(b_last + m_prev - m_new)
        c_ref[0, h] = decay * c_old + _tn_dot((wgt * vf).astype(BF16), k)
        n_ref[0, h:h + 1, :] = decay * n_old + jnp.sum(wgt * kf, axis=0, keepdims=True)
        m_next.append(m_new)
    return m_next


def _lru_gates(conv, wai_ref, ba_ref, bi_ref, lam_ref):
    conv_b = conv.astype(BF16)
    pre = [_dot(conv_b[:, n * LRU_BW:(n + 1) * LRU_BW], wai_ref[n])
           for n in range(LRU_BLOCKS)]
    r_gate = jax.nn.sigmoid(jnp.concatenate([p[:, :LRU_BW] for p in pre], axis=1) + ba_ref[...])
    i_gate = jax.nn.sigmoid(jnp.concatenate([p[:, LRU_BW:] for p in pre], axis=1) + bi_ref[...])
    log_a = r_gate * (-LRU_C * _softplus(-lam_ref[...]))
    a = jnp.exp(log_a)
    w = -jnp.tanh(log_a) * (a * a + 1.0)
    xin = jnp.where(w > 0.0, w * lax.rsqrt(w), 0.0) * (i_gate * conv)
    return a, xin


def _pmixer_project(x_ref, gmix_ref, wxy_ref, zxy_dst, pbuf, xnb_s, *, tt):
    seg = tt // SUBLANES
    pitch = seg + SUBLANES
    xn = _rms(x_ref[...], gmix_ref[...])
    xnb_s[...] = xn.astype(BF16)
    for k in range(LRU_WIDTH // LANES):
        for s in range(SUBLANES):
            pbuf[k, pl.ds(s * pitch, seg), :] = xn[s * seg:(s + 1) * seg, k * LANES:(k + 1) * LANES]
    xnp = jnp.stack(
        [jnp.concatenate([pbuf[k, pl.ds(j, SUBLANES, stride=pitch), :]
                          for k in range(LRU_WIDTH // LANES)], axis=1) for j in range(seg)], axis=0)
    zxy_dst[...] = _dot(xnp.reshape(tt, D_MODEL).astype(BF16), wxy_ref[...])


def _pmixer_mix(x_ref, xnb_s, wnat_ref, z_s, zxy_s, gbias_ref, onorm_ref, convw_ref, convb_ref, wai_ref, ba_ref, bi_ref,
                lam_ref, wout_ref, y_ref, c_ref, n_ref, m_ref, h_ref, cv_ref, obuf, hm_s, *, cl, tt, zc):
    seg = tt // SUBLANES
    pitch = seg + SUBLANES

    u3 = zxy_s[:, 0:LRU_WIDTH].reshape(seg, SUBLANES, LRU_WIDTH)
    tail = cv_ref[0]
    sub = lax.broadcasted_iota(jnp.int32, (SUBLANES, LRU_WIDTH), 0)
    wrap = []
    for i in range(CONV_W - 1):
        prev = pltpu.roll(u3[seg - (CONV_W - 1) + i], 1, axis=0)
        fill = tail[SUBLANES - (CONV_W - 1) + i:SUBLANES - (CONV_W - 1) + i + 1, :]
        wrap.append(jnp.where(sub == 0, fill, prev))
        cv_ref[0, SUBLANES - (CONV_W - 1) + i:SUBLANES - (CONV_W - 1) + i + 1, :] = (
            u3[seg - (CONV_W - 1) + i][SUBLANES - 1:SUBLANES, :])
    ext = jnp.concatenate([jnp.stack(wrap, axis=0), u3], axis=0)
    conv3 = convb_ref[...] + ext[CONV_W - 1:] * convw_ref[CONV_W - 1:CONV_W, :]
    for j in range(1, CONV_W):
        conv3 = conv3 + ext[CONV_W - 1 - j:CONV_W - 1 - j + seg] * convw_ref[CONV_W - 1 - j:CONV_W - j, :]

    conv = conv3.reshape(tt, LRU_WIDTH)
    bounds = [min(k * zc, Z_NAT) for k in range(-(-Z_NAT // zc) + 1)]
    zcols = list(zip(bounds[:-1], bounds[1:]))

    def project(k):
        lo, hi = zcols[k]
        z_s[:, lo:hi] = _dot(xnb_s[...], wnat_ref[:, lo:hi])

    project(0)
    nrow = len(zcols) - 1
    step = tt // nrow
    a_parts, x_parts = [], []
    for c in range(nrow):
        r = slice(c * step, (c + 1) * step)
        a_c, x_c = _lru_gates(conv[r], wai_ref, ba_ref, bi_ref, lam_ref)
        a_parts.append(a_c)
        x_parts.append(x_c)
        project(c + 1)
    a = jnp.concatenate(a_parts, axis=0)
    xin = jnp.concatenate(x_parts, axis=0)

    a3 = a.reshape(seg, SUBLANES, LRU_WIDTH)
    x3 = xin.reshape(seg, SUBLANES, LRU_WIDTH)
    hs = [x3[0]]
    ps = [a3[0]]
    for j in range(1, seg):
        hs.append(a3[j] * hs[-1] + x3[j])
        ps.append(a3[j] * ps[-1])
    carry = h_ref[0]
    cin = []
    for s in range(SUBLANES):
        cin.append(carry)
        carry = hs[-1][s:s + 1, :] + ps[-1][s:s + 1, :] * carry
    h_ref[0] = carry
    cin = jnp.concatenate(cin, axis=0)
    h3 = jnp.stack([hs[j] + ps[j] * cin for j in range(seg)], axis=0)

    hl = (jax.nn.gelu(zxy_s[:, LRU_WIDTH:]) * h3.reshape(tt, LRU_WIDTH)).astype(BF16)
    ol3 = _dot(hl, wout_ref[ML_WIDTH:ML_WIDTH + LRU_WIDTH, :]).reshape(seg, SUBLANES, D_MODEL)
    for j in range(seg):
        for k in range(D_MODEL // LANES):
            obuf[k, pl.ds(j, SUBLANES, stride=pitch), :] = ol3[j][:, k * LANES:(k + 1) * LANES]
    out_lru = jnp.concatenate(
        [jnp.concatenate([obuf[k, pl.ds(s * pitch, seg), :] for s in range(SUBLANES)], axis=0)
         for k in range(D_MODEL // LANES)], axis=1)

    _mlstm_blocks(z_s, gbias_ref, onorm_ref, c_ref, n_ref, m_ref, hm_s, tt, cl)

    y_ref[...] = x_ref[...] + _dot(hm_s[...], wout_ref[0:ML_WIDTH, :]) + out_lru


def _pmixer_kernel(x_ref, gmix_ref, wnat_ref, wxy_ref, gbias_ref, onorm_ref, convw_ref,
                   convb_ref, wai_ref, ba_ref, bi_ref, lam_ref, wout_ref,
                   y_ref, c_ref, n_ref, m_ref, h_ref, cv_ref,
                   z_s, zxy_s, pbuf, obuf, hm_s, xnb_s, *, cl, tt, zc):
    @pl.when(pl.program_id(1) == 0)
    def _():
        c_ref[...] = jnp.zeros_like(c_ref)
        n_ref[...] = jnp.zeros_like(n_ref)
        m_ref[...] = jnp.zeros_like(m_ref)
        h_ref[...] = jnp.zeros_like(h_ref)
        cv_ref[...] = jnp.zeros_like(cv_ref)

    _pmixer_project(x_ref, gmix_ref, wxy_ref, zxy_s, pbuf, xnb_s, tt=tt)
    _pmixer_mix(x_ref, xnb_s, wnat_ref, z_s, zxy_s, gbias_ref, onorm_ref, convw_ref, convb_ref, wai_ref, ba_ref,
                bi_ref, lam_ref, wout_ref, y_ref, c_ref, n_ref, m_ref, h_ref, cv_ref, obuf, hm_s,
                cl=cl, tt=tt, zc=zc)


def _pmixer(x, p, l, *, nb, t, tt, cl, zc=Z_CHUNK):
    nt = t // tt
    pitch = tt // SUBLANES + SUBLANES
    tok = lambda b, i: (b * nt + i, 0)
    st4 = lambda b, i: (b, 0, 0, 0)
    st3 = lambda b, i: (b, 0, 0)
    lay3 = lambda b, i: (l, 0, 0)
    lay4 = lambda b, i: (l, 0, 0, 0)
    return pl.pallas_call(
        functools.partial(_pmixer_kernel, cl=cl, tt=tt, zc=zc),
        out_shape=(
            jax.ShapeDtypeStruct((nb * t, D_MODEL), F32),
            jax.ShapeDtypeStruct((nb, ML_HEADS, ML_DV, ML_DK), F32),
            jax.ShapeDtypeStruct((nb, ML_HEADS, ML_DK), F32),
            jax.ShapeDtypeStruct((nb, 1, LANES), F32),
            jax.ShapeDtypeStruct((nb, 1, LRU_WIDTH), F32),
            jax.ShapeDtypeStruct((nb, SUBLANES, LRU_WIDTH), F32),
        ),
        grid=(nb, nt),
        in_specs=[
            pl.BlockSpec((tt, D_MODEL), tok),
            _resident((None, 1, D_MODEL), lay3),
            _resident((None, D_MODEL, Z_NAT), lay3),
            _resident((None, D_MODEL, Z_XY), lay3),
            _resident((None, 1, LANES), lay3),
            _resident((None, ML_HEADS, ML_DV), lay3),
            _resident((None, CONV_W, LRU_WIDTH), lay3),
            _resident((None, 1, LRU_WIDTH), lay3),
            _resident((None, LRU_BLOCKS, LRU_BW, 2 * LRU_BW), lay4),
            _resident((None, 1, LRU_WIDTH), lay3),
            _resident((None, 1, LRU_WIDTH), lay3),
            _resident((None, 1, LRU_WIDTH), lay3),
            _resident((None, ML_WIDTH + LRU_WIDTH, D_MODEL), lay3),
        ],
        out_specs=(
            pl.BlockSpec((tt, D_MODEL), tok),
            pl.BlockSpec((1, ML_HEADS, ML_DV, ML_DK), st4),
            pl.BlockSpec((1, ML_HEADS, ML_DK), st3),
            pl.BlockSpec((1, 1, LANES), st3),
            pl.BlockSpec((1, 1, LRU_WIDTH), st3),
            pl.BlockSpec((1, SUBLANES, LRU_WIDTH), st3),
        ),
        scratch_shapes=[pltpu.VMEM((tt, Z_NAT), F32), pltpu.VMEM((tt, Z_XY), F32),
                        pltpu.VMEM((LRU_WIDTH // LANES, SUBLANES * pitch, LANES), F32),
                        pltpu.VMEM((D_MODEL // LANES, SUBLANES * pitch, LANES), F32),
                        pltpu.VMEM((tt, ML_WIDTH), BF16), pltpu.VMEM((tt, D_MODEL), BF16)],
        compiler_params=_params("parallel", "arbitrary"),
        name="pmixer",
    )(x, p["mix_norm"], p["w_in"], p["w_xy"], p["gbias"], p["ml_out_norm"], p["lru_conv_w"],
      p["lru_conv_b"], p["w_ai"], p["lru_b_a"], p["lru_b_i"], p["lru_lambda"], p["w_out"])


def _zproj_kernel(x_ref, g_ref, wnat_ref, wxy_ref, znat_ref, zxy_ref):
    xn = _rms(x_ref[...], g_ref[...]).astype(BF16)
    znat_ref[...] = _dot(xn, wnat_ref[...])
    zxy_ref[...] = _dot(xn, wxy_ref[...])


def _zproj(x, p, l):
    m = x.shape[0]
    lay3 = lambda i: (l, 0, 0)
    return pl.pallas_call(
        _zproj_kernel,
        out_shape=(jax.ShapeDtypeStruct((m, Z_NAT), F32), jax.ShapeDtypeStruct((m, Z_XY), F32)),
        grid=(1,),
        in_specs=[
            pl.BlockSpec((m, D_MODEL), lambda i: (0, 0)),
            pl.BlockSpec((None, 1, D_MODEL), lay3),
            pl.BlockSpec((None, D_MODEL, Z_NAT), lay3),
            pl.BlockSpec((None, D_MODEL, Z_XY), lay3),
        ],
        out_specs=(pl.BlockSpec((m, Z_NAT), lambda i: (0, 0)), pl.BlockSpec((m, Z_XY), lambda i: (0, 0))),
        compiler_params=_params("arbitrary"),
        name="zproj",
    )(x, p["mix_norm"], p["w_in"], p["w_xy"])


def _smixer_kernel(znat_ref, zxy_ref, x_ref, c0_ref, n0_ref, m0_ref, h0_ref, cv0_ref,
                   gbias_ref, onorm_ref, convw_ref, convb_ref, wai_ref, ba_ref, bi_ref, lam_ref,
                   wout_ref,
                   y_ref, c_ref, n_ref, m_ref, h_ref, cv_ref,
                   ubuf_s, hm_s, *, ns, t):
    c_ref[...] = c0_ref[...]
    n_ref[...] = n0_ref[...]
    m_ref[...] = m0_ref[...]
    pad = t + SUBLANES

    convs = []
    for s in range(ns):
        one = pl.ds(s, 1)
        _mlstm_blocks(znat_ref, gbias_ref, onorm_ref, c_ref.at[one], n_ref.at[one], m_ref.at[one], hm_s,
                      t, t, base=s * t)
        u = zxy_ref[s * t:(s + 1) * t, 0:LRU_WIDTH]
        ubuf_s[s * pad:s * pad + SUBLANES, :] = cv0_ref[s]
        ubuf_s[s * pad + SUBLANES:(s + 1) * pad, :] = u
        cv_ref[s] = ubuf_s[s * pad + t:(s + 1) * pad, :]
        conv = convb_ref[...] + u * convw_ref[CONV_W - 1:CONV_W, :]
        for j in range(1, CONV_W):
            conv = conv + (ubuf_s[pl.ds(s * pad + SUBLANES - j, t), :]
                           * convw_ref[CONV_W - 1 - j:CONV_W - j, :])
        convs.append(conv)
    a, xin = _lru_gates(jnp.concatenate(convs, axis=0), wai_ref, ba_ref, bi_ref, lam_ref)

    rmod = lax.broadcasted_iota(jnp.int32, (ns * t, LRU_WIDTH), 0) & (SUBLANES - 1)
    d = 1
    while d < SUBLANES:
        keep = rmod >= d
        xin = xin + a * jnp.where(keep, pltpu.roll(xin, d, axis=0), 0.0)
        a = a * jnp.where(keep, pltpu.roll(a, d, axis=0), 1.0)
        d *= 2
    groups = []
    for s in range(ns):
        carry = h0_ref[s]
        for gidx in range(t // SUBLANES):
            lo = s * t + gidx * SUBLANES
            hb = xin[lo:lo + SUBLANES, :] + a[lo:lo + SUBLANES, :] * carry
            carry = hb[SUBLANES - 1:SUBLANES, :]
            groups.append(hb)
        h_ref[s] = carry
    h_lru = jnp.concatenate(groups, axis=0)

    hl = (jax.nn.gelu(zxy_ref[:, LRU_WIDTH:]) * h_lru).astype(BF16)
    out = (_dot(hm_s[...], wout_ref[0:ML_WIDTH, :])
           + _dot(hl, wout_ref[ML_WIDTH:ML_WIDTH + LRU_WIDTH, :]))
    y_ref[...] = x_ref[...] + out


def _smixer(znat, zxy, x, st, p, l, *, nb, t, ns):
    tok = lambda b: (b, 0)
    lay3 = lambda b: (l, 0, 0)
    lay4 = lambda b: (l, 0, 0, 0)
    st3 = lambda b: (b, 0, 0)
    lst = lambda b: (l, b, 0, 0)
    c0, n0, m0, h0, cv0 = st
    return pl.pallas_call(
        functools.partial(_smixer_kernel, ns=ns, t=t),
        out_shape=(
            jax.ShapeDtypeStruct((nb * t, D_MODEL), F32),
            jax.ShapeDtypeStruct((nb, ML_HEADS, ML_DV, ML_DK), F32),
            jax.ShapeDtypeStruct((nb, ML_HEADS, ML_DK), F32),
            jax.ShapeDtypeStruct((nb, 1, LANES), F32),
            jax.ShapeDtypeStruct((nb, 1, LRU_WIDTH), F32),
            jax.ShapeDtypeStruct((nb, SUBLANES, LRU_WIDTH), F32),
        ),
        grid=(nb // ns,),
        in_specs=[
            pl.BlockSpec((ns * t, Z_NAT), tok),
            pl.BlockSpec((ns * t, Z_XY), tok),
            pl.BlockSpec((ns * t, D_MODEL), tok),
            pl.BlockSpec((None, ns, ML_HEADS, ML_DV, ML_DK), lambda b: (l, b, 0, 0, 0)),
            pl.BlockSpec((None, ns, ML_HEADS, ML_DK), lst),
            pl.BlockSpec((None, ns, 1, LANES), lst),
            pl.BlockSpec((None, ns, 1, LRU_WIDTH), lst),
            pl.BlockSpec((None, ns, SUBLANES, LRU_WIDTH), lst),
            pl.BlockSpec((None, 1, LANES), lay3),
            pl.BlockSpec((None, ML_HEADS, ML_DV), lay3),
            pl.BlockSpec((None, CONV_W, LRU_WIDTH), lay3),
            pl.BlockSpec((None, 1, LRU_WIDTH), lay3),
            pl.BlockSpec((None, LRU_BLOCKS, LRU_BW, 2 * LRU_BW), lay4),
            pl.BlockSpec((None, 1, LRU_WIDTH), lay3),
            pl.BlockSpec((None, 1, LRU_WIDTH), lay3),
            pl.BlockSpec((None, 1, LRU_WIDTH), lay3),
            pl.BlockSpec((None, ML_WIDTH + LRU_WIDTH, D_MODEL), lay3),
        ],
        out_specs=(
            pl.BlockSpec((ns * t, D_MODEL), tok),
            pl.BlockSpec((ns, ML_HEADS, ML_DV, ML_DK), lambda b: (b, 0, 0, 0)),
            pl.BlockSpec((ns, ML_HEADS, ML_DK), st3),
            pl.BlockSpec((ns, 1, LANES), st3),
            pl.BlockSpec((ns, 1, LRU_WIDTH), st3),
            pl.BlockSpec((ns, SUBLANES, LRU_WIDTH), st3),
        ),
        scratch_shapes=[pltpu.VMEM((ns * (t + SUBLANES), LRU_WIDTH), F32),
                        pltpu.VMEM((ns * t, ML_WIDTH), BF16)],
        compiler_params=_params("parallel"),
        name="smixer",
    )(znat, zxy, x, c0, n0, m0, h0, cv0, p["gbias"], p["ml_out_norm"], p["lru_conv_w"],
      p["lru_conv_b"], p["w_ai"], p["lru_b_a"], p["lru_b_i"], p["lru_lambda"], p["w_out"])


def _xattn_kernel(x_ref, g_ref, wq_ref, mk_ref, mv_ref, wo_ref, y_ref, o_s, *, ns, rows, split):
    step = rows // split
    chunks = [(s, slice(s * rows + c * step, s * rows + (c + 1) * step)) for s in range(ns) for c in range(split)]
    heads = [slice(h * XA_DH, (h + 1) * XA_DH) for h in range(XA_HEADS)]

    def scores(s, r):
        q = _dot(_rms(x_ref[r, :], g_ref[...]).astype(BF16), wq_ref[...])
        return [_nt_dot(q[:, c].astype(BF16), mk_ref[s, :, c].astype(BF16)) * (XA_DH ** -0.5) for c in heads]

    def attend(s, r, sc):
        for c, sch in zip(heads, sc):
            p = jnp.exp(sch - jnp.max(sch, axis=1, keepdims=True))
            p = p / jnp.sum(p, axis=1, keepdims=True)
            o_s[r, c] = _dot(p.astype(BF16), mv_ref[s, :, c].astype(BF16)).astype(BF16)

    def project(r):
        y_ref[r, :] = x_ref[r, :] + _dot(o_s[r, :], wo_ref[...])

    pending = None
    for s, r in chunks:
        sc = scores(s, r)
        if pending is not None:
            ps, pr, psc = pending
            attend(ps, pr, psc)
            project(pr)
        pending = (s, r, sc)
    ps, pr, psc = pending
    attend(ps, pr, psc)
    project(pr)


def _xattn(x, g, wq, mk, mv, wo, l, *, nb, t, tt, ns=1, split=1):
    nt = t // tt
    assert ns == 1 or nt == 1
    tok = lambda b, i: (b * nt + i, 0)
    lay3 = lambda b, i: (l, 0, 0)
    mem = lambda b, i: (l, b, 0, 0)
    return pl.pallas_call(
        functools.partial(_xattn_kernel, ns=ns, rows=tt, split=split),
        out_shape=jax.ShapeDtypeStruct((nb * t, D_MODEL), F32),
        grid=(nb // ns, nt),
        in_specs=[
            pl.BlockSpec((ns * tt, D_MODEL), tok),
            pl.BlockSpec((None, 1, D_MODEL), lay3),
            pl.BlockSpec((None, D_MODEL, D_MODEL), lay3),
            pl.BlockSpec((None, ns, MEM_LEN, D_MODEL), mem),
            pl.BlockSpec((None, ns, MEM_LEN, D_MODEL), mem),
            pl.BlockSpec((None, D_MODEL, D_MODEL), lay3),
        ],
        out_specs=pl.BlockSpec((ns * tt, D_MODEL), tok),
        scratch_shapes=[pltpu.VMEM((ns * tt, D_MODEL), BF16)],
        compiler_params=_params("parallel", "arbitrary"),
        name="xattn",
    )(x, g, wq, mk, mv, wo)


def _memkv_kernel(mem_ref, g_ref, wk_ref, wv_ref, k_ref, v_ref):
    mn = _rms(mem_ref[...], g_ref[...]).astype(BF16)
    k_ref[...] = _dot(mn, wk_ref[...])
    v_ref[...] = _dot(mn, wv_ref[...])


def _memkv(mem, g, wk, wv):
    nb = mem.shape[0]
    wspec = pl.BlockSpec((None, D_MODEL, D_MODEL), lambda l, b: (l, 0, 0))
    ospec = pl.BlockSpec((None, None, MEM_LEN, D_MODEL), lambda l, b: (l, b, 0, 0))
    out = jax.ShapeDtypeStruct((DEPTH, nb, MEM_LEN, D_MODEL), F32)
    return pl.pallas_call(
        _memkv_kernel,
        out_shape=(out, out),
        grid=(DEPTH, nb),
        in_specs=[pl.BlockSpec((None, MEM_LEN, D_MODEL), lambda l, b: (b, 0, 0)),
                  pl.BlockSpec((None, 1, D_MODEL), lambda l, b: (l, 0, 0)), wspec, wspec],
        out_specs=(ospec, ospec),
        compiler_params=_params("parallel", "parallel"),
        name="memkv",
    )(mem, g, wk, wv)


def kernel(x_prompt, x_sample, mem_prompt, state_mlstm_C, state_mlstm_n, state_mlstm_m, state_lru_h,
           state_lru_conv, cache_mem_k, cache_mem_v, ffn1_norm, ffn1_w_gate, ffn1_w_up, ffn1_w_down,
           mix_norm, w_in, ml_b_i, ml_b_f, ml_out_norm, lru_conv_w, lru_conv_b, lru_w_a, lru_b_a,
           lru_w_i, lru_b_i, lru_lambda, w_out, xattn_norm, mem_norm, xattn_w_q, xattn_w_k, xattn_w_v,
           xattn_w_o, ffn2_norm, ffn2_w_gate, ffn2_w_up, ffn2_w_down, final_norm):
    bp, tp, _ = x_prompt.shape
    bs, ts, _ = x_sample.shape
    xp = x_prompt.reshape(bp * tp, D_MODEL)
    xs = x_sample.reshape(bs * ts, D_MODEL)

    bf = lambda w: w.astype(BF16)
    row = lambda v: v.astype(F32).reshape(DEPTH, 1, -1)
    gate_pad = ((0, 0), (0, 0), (0, LANES - 2 * ML_HEADS))
    w_in_b = bf(w_in)
    p = {
        "mix_norm": row(mix_norm),
        "w_in": w_in_b,
        "w_xy": w_in_b[:, :, W_X:],
        "gbias": jnp.pad(jnp.concatenate([ml_b_i, ml_b_f], axis=1).astype(F32)[:, None, :], gate_pad),
        "ml_out_norm": ml_out_norm.astype(F32),
        "lru_conv_w": lru_conv_w, "lru_conv_b": row(lru_conv_b),
        "w_ai": bf(jnp.concatenate([lru_w_a, lru_w_i], axis=-1)),
        "lru_b_a": row(lru_b_a), "lru_b_i": row(lru_b_i), "lru_lambda": row(lru_lambda),
        "w_out": bf(w_out),
    }
    ffn1 = (row(ffn1_norm), bf(ffn1_w_gate), bf(ffn1_w_up), bf(ffn1_w_down))
    ffn2 = (row(ffn2_norm), bf(ffn2_w_gate), bf(ffn2_w_up), bf(ffn2_w_down))
    xa_g, xa_q, xa_o = row(xattn_norm), bf(xattn_w_q), bf(xattn_w_o)
    fin = final_norm.astype(F32).reshape(1, D_MODEL)

    pk, pv = _memkv(mem_prompt, row(mem_norm), bf(xattn_w_k), bf(xattn_w_v))
    sk = cache_mem_k.reshape(DEPTH, bs, MEM_LEN, D_MODEL)
    sv = cache_mem_v.reshape(DEPTH, bs, MEM_LEN, D_MODEL)
    s_state = (
        state_mlstm_C.astype(F32), state_mlstm_n.astype(F32),
        jnp.pad(state_mlstm_m.astype(F32)[:, :, None, :], ((0, 0), (0, 0), (0, 0), (0, LANES - ML_HEADS))),
        state_lru_h.astype(F32)[:, :, None, :],
        jnp.pad(state_lru_conv.astype(F32), ((0, 0), (0, 0), (SUBLANES - (CONV_W - 1), 0), (0, 0))),
    )

    p_out = [[] for _ in range(5)]
    s_out = [[] for _ in range(5)]
    for l in range(DEPTH):
        last = l == DEPTH - 1
        xp, xs = _ffn(xp, xs, *ffn1, fin, l, tm=512, tf=256, final_norm=False)
        xp, *st = _pmixer(xp, p, l, nb=bp, t=tp, tt=512, cl=ML_BLOCK)
        xp = _xattn(xp, xa_g, xa_q, pk, pv, xa_o, l, nb=bp, t=tp, tt=1024, split=4)
        for acc, v in zip(p_out, st):
            acc.append(v)
        znat, zxy = _zproj(xs, p, l)
        xs, *st = _smixer(znat, zxy, xs, s_state, p, l, nb=bs, t=ts, ns=4)
        xs = _xattn(xs, xa_g, xa_q, sk, sv, xa_o, l, nb=bs, t=ts, tt=ts, ns=4)
        for acc, v in zip(s_out, st):
            acc.append(v)
        xp, xs = _ffn(xp, xs, *ffn2, fin, l, tm=512, tf=256, final_norm=last)

    def states(acc):
        c, n, m, h, cv = (jnp.stack(a) for a in acc)
        return c, n, m[:, :, 0, :ML_HEADS], h[:, :, 0, :], cv[:, :, SUBLANES - (CONV_W - 1):, :]

    return (xp.reshape(bp, tp, D_MODEL), xs.reshape(bs, ts, D_MODEL),
            *states(p_out),
            pk.reshape(DEPTH, bp, MEM_LEN, XA_HEADS, XA_DH), pv.reshape(DEPTH, bp, MEM_LEN, XA_HEADS, XA_DH),
            *states(s_out))
```

```python
import functools

import jax
import jax.numpy as jnp
from jax import lax
from jax.experimental import pallas as pl
from jax.experimental.pallas import tpu as pltpu

F32 = jnp.float32
BF16 = jnp.bfloat16

D_MODEL = 1024
DEPTH = 2
ML_BLOCK = 256
ML_HEADS = 4
ML_DV = 256
ML_DK = 128
ML_WIDTH = ML_HEADS * ML_DV
LRU_WIDTH = 1024
LRU_BLOCKS = 8
LRU_BW = LRU_WIDTH // LRU_BLOCKS
CONV_W = 4
LRU_C = 8.0
MEM_LEN = 256
XA_HEADS = 4
XA_DH = D_MODEL // XA_HEADS
D_FF = 4 * D_MODEL
EPS = 1e-6

LANES = 128
SUBLANES = 8
VMEM_LIMIT = 56 * 1024 * 1024

Z_Q = 0
Z_K = Z_Q + ML_HEADS * ML_DK
Z_V = Z_K + ML_HEADS * ML_DK
Z_O = Z_V + ML_WIDTH
Z_G = Z_O + ML_WIDTH
Z_NAT = Z_G + LANES
Z_CHUNK = 768
Z_XY = 2 * LRU_WIDTH
W_X = Z_G + 2 * ML_HEADS


def _rms(x, g):
    return x * lax.rsqrt(jnp.mean(x * x, axis=-1, keepdims=True) + EPS) * g


def _nt_dot(a, b):
    return lax.dot_general(a, b, (((1,), (1,)), ((), ())), preferred_element_type=F32)


def _tn_dot(a, b):
    return lax.dot_general(a, b, (((0,), (0,)), ((), ())), preferred_element_type=F32)


def _dot(a, b):
    return jnp.dot(a, b, preferred_element_type=F32)


def _params(*sem):
    return pltpu.CompilerParams(dimension_semantics=sem, vmem_limit_bytes=VMEM_LIMIT)


def _resident(shape, index_map):
    return pl.BlockSpec(shape, index_map, pipeline_mode=pl.Buffered(1))


def _ffn_body(parts, g_ref, wg_ref, wu_ref, wd_ref, fg_ref, h_s, *, tf, final_norm):
    xs = [x_ref[...] for x_ref, _ in parts]
    rows = sum(x.shape[0] for x in xs)
    xn = jnp.concatenate([_rms(x, g_ref[...]).astype(BF16) for x in xs], axis=0)
    for j in range(D_FF // tf):
        cols = slice(j * tf, (j + 1) * tf)
        g = _dot(xn, wg_ref[:, cols])
        u = _dot(xn, wu_ref[:, cols])
        h_s[0:rows, cols] = (g * jax.nn.sigmoid(g) * u).astype(BF16)
    d = _dot(h_s[0:rows, :], wd_ref[...])
    lo = 0
    for x, (_, o_ref) in zip(xs, parts):
        y = x + 0.5 * d[lo:lo + x.shape[0], :]
        if final_norm:
            y = _rms(y, fg_ref[...])
        o_ref[...] = y
        lo += x.shape[0]


def _ffn_kernel(xp_ref, xs_ref, g_ref, wg_ref, wu_ref, wd_ref, fg_ref, op_ref, os_ref, h_s, **kw):
    last = pl.num_programs(0) - 1

    @pl.when(pl.program_id(0) < last)
    def _():
        _ffn_body([(xp_ref, op_ref)], g_ref, wg_ref, wu_ref, wd_ref, fg_ref, h_s, **kw)

    @pl.when(pl.program_id(0) == last)
    def _():
        _ffn_body([(xp_ref, op_ref), (xs_ref, os_ref)], g_ref, wg_ref, wu_ref, wd_ref, fg_ref, h_s, **kw)


def _ffn(xp, xs, g, wg, wu, wd, fg, l, *, tm, tf, final_norm):
    np_tiles = xp.shape[0] // tm
    ms = xs.shape[0]
    ptile = lambda i: (i, 0)
    whole = lambda i: (0, 0)
    lay = lambda i: (l, 0, 0)
    return pl.pallas_call(
        functools.partial(_ffn_kernel, tf=tf, final_norm=final_norm),
        out_shape=(jax.ShapeDtypeStruct(xp.shape, F32), jax.ShapeDtypeStruct(xs.shape, F32)),
        grid=(np_tiles,),
        in_specs=[
            pl.BlockSpec((tm, D_MODEL), ptile),
            _resident((ms, D_MODEL), whole),
            _resident((None, 1, D_MODEL), lay),
            _resident((None, D_MODEL, D_FF), lay),
            _resident((None, D_MODEL, D_FF), lay),
            _resident((None, D_FF, D_MODEL), lay),
            _resident((1, D_MODEL), whole),
        ],
        out_specs=(pl.BlockSpec((tm, D_MODEL), ptile), pl.BlockSpec((ms, D_MODEL), whole)),
        scratch_shapes=[pltpu.VMEM((tm + ms, D_FF), BF16)],
        compiler_params=_params("arbitrary"),
        name="ffn",
    )(xp, xs, g, wg, wu, wd, fg)


def _log_sigmoid(x):
    return jnp.minimum(x, 0.0) - jnp.log1p(jnp.exp(-jnp.abs(x)))


def _softplus(x):
    return jnp.maximum(x, 0.0) + jnp.log1p(jnp.exp(-jnp.abs(x)))


def _lane_pick(x, lane_ids, idx):
    return jnp.sum(jnp.where(lane_ids == idx, x, 0.0), axis=1, keepdims=True)


def _mlstm_blocks(z_ref, gbias_ref, onorm_ref, c_ref, n_ref, m_ref, hm_s, rows, cl, base=0):
    lane1 = lax.broadcasted_iota(jnp.int32, (1, LANES), 1)
    m_vec = m_ref[0]
    m_heads = [_lane_pick(m_vec, lane1, h) for h in range(ML_HEADS)]
    for c in range(rows // cl):
        m_heads = _mlstm_chunk(z_ref, gbias_ref, onorm_ref, c_ref, n_ref, m_heads, hm_s, base + c * cl, cl)
    for h in range(ML_HEADS):
        m_vec = jnp.where(lane1 == h, m_heads[h], m_vec)
    m_ref[0] = m_vec


def _mlstm_chunk(z_ref, gbias_ref, onorm_ref, c_ref, n_ref, m_heads, hm_s, r0, cl):
    lane = lax.broadcasted_iota(jnp.int32, (cl, LANES), 1)
    row = lax.broadcasted_iota(jnp.int32, (cl, LANES), 0)
    tril = (lax.broadcasted_iota(jnp.int32, (cl, cl), 1)
            <= lax.broadcasted_iota(jnp.int32, (cl, cl), 0))

    gates = z_ref[r0:r0 + cl, Z_G:Z_G + LANES] + gbias_ref[...]
    bsum = _log_sigmoid(gates)
    d = 1
    while d < cl:
        bsum = bsum + jnp.where(row >= d, pltpu.roll(bsum, d, axis=0), 0.0)
        d *= 2
    mixed = jnp.where(lane < ML_HEADS, gates, bsum)
    if cl < LANES:
        mixed = jnp.concatenate([mixed, jnp.zeros((LANES - cl, LANES), F32)], axis=0)
    mixed_t = mixed.T

    m_next = []
    for h in range(ML_HEADS):
        b_col = _lane_pick(bsum, lane, ML_HEADS + h)
        ig_col = _lane_pick(gates, lane, h)
        b_row = mixed_t[ML_HEADS + h:ML_HEADS + h + 1, 0:cl]
        ig_row = mixed_t[h:h + 1, 0:cl]
        m_prev = m_heads[h]

        dmat = jnp.where(tril, b_col - b_row + ig_row, -jnp.inf)
        inter = b_col + m_prev
        m_t = jnp.maximum(inter, jnp.max(dmat, axis=1, keepdims=True))

        qf = z_ref[r0:r0 + cl, Z_Q + h * ML_DK:Z_Q + (h + 1) * ML_DK] * (ML_DK ** -0.5)
        kf = z_ref[r0:r0 + cl, Z_K + h * ML_DK:Z_K + (h + 1) * ML_DK]
        vf = z_ref[r0:r0 + cl, Z_V + h * ML_DV:Z_V + (h + 1) * ML_DV]
        q = qf.astype(BF16)
        k = kf.astype(BF16)

        s = _nt_dot(q, k) * jnp.exp(dmat - m_t)
        w_inter = jnp.exp(inter - m_t)
        c_old = c_ref[0, h]
        n_old = n_ref[0, h:h + 1, :]
        num = w_inter * _nt_dot(q, c_old.astype(BF16)) + _dot(s.astype(BF16), vf.astype(BF16))
        den = (w_inter * jnp.sum(qf * n_old, axis=1, keepdims=True)
               + jnp.sum(s, axis=1, keepdims=True))
        hh = num / jnp.maximum(jnp.abs(den), jnp.exp(-m_t))

        hh = hh * lax.rsqrt(jnp.mean(hh * hh, axis=1, keepdims=True) + EPS) * onorm_ref[h:h + 1, :]
        o_gate = jax.nn.sigmoid(z_ref[r0:r0 + cl, Z_O + h * ML_DV:Z_O + (h + 1) * ML_DV])
        hm_s[r0:r0 + cl, h * ML_DV:(h + 1) * ML_DV] = (o_gate * hh).astype(BF16)

        m_new = m_t[cl - 1:cl, :]
        b_last = b_col[cl - 1:cl, :]
        wgt = jnp.exp(b_last - b_col + ig_col - m_new)
        decay = jnp.exp(b_last + m_prev - m_new)
        c_ref[0, h] = decay * c_old + _tn_dot((wgt * vf).astype(BF16), k)
        n_ref[0, h:h + 1, :] = decay * n_old + jnp.sum(wgt * kf, axis=0, keepdims=True)
        m_next.append(m_new)
    return m_next


def _lru_gates(conv, wai_ref, ba_ref, bi_ref, lam_ref):
    conv_b = conv.astype(BF16)
    pre = [_dot(conv_b[:, n * LRU_BW:(n + 1) * LRU_BW], wai_ref[n])
           for n in range(LRU_BLOCKS)]
    r_gate = jax.nn.sigmoid(jnp.concatenate([p[:, :LRU_BW] for p in pre], axis=1) + ba_ref[...])
    i_gate = jax.nn.sigmoid(jnp.concatenate([p[:, LRU_BW:] for p in pre], axis=1) + bi_ref[...])
    log_a = r_gate * (-LRU_C * _softplus(-lam_ref[...]))
    a = jnp.exp(log_a)
    w = -jnp.tanh(log_a) * (a * a + 1.0)
    xin = jnp.where(w > 0.0, w * lax.rsqrt(w), 0.0) * (i_gate * conv)
    return a, xin


def _pmixer_project(x_ref, gmix_ref, wxy_ref, zxy_dst, pbuf, xnb_s, *, tt):
    seg = tt // SUBLANES
    pitch = seg + SUBLANES
    xn = _rms(x_ref[...], gmix_ref[...])
    xnb_s[...] = xn.astype(BF16)
    for k in range(LRU_WIDTH // LANES):
        for s in range(SUBLANES):
            pbuf[k, pl.ds(s * pitch, seg), :] = xn[s * seg:(s + 1) * seg, k * LANES:(k + 1) * LANES]
    xnp = jnp.stack(
        [jnp.concatenate([pbuf[k, pl.ds(j, SUBLANES, stride=pitch), :]
                          for k in range(LRU_WIDTH // LANES)], axis=1) for j in range(seg)], axis=0)
    zxy_dst[...] = _dot(xnp.reshape(tt, D_MODEL).astype(BF16), wxy_ref[...])


def _pmixer_mix(x_ref, xnb_s, wnat_ref, z_s, zxy_s, gbias_ref, onorm_ref, convw_ref, convb_ref, wai_ref, ba_ref, bi_ref,
                lam_ref, wout_ref, y_ref, c_ref, n_ref, m_ref, h_ref, cv_ref, obuf, hm_s, *, cl, tt, zc):
    seg = tt // SUBLANES
    pitch = seg + SUBLANES

    u3 = zxy_s[:, 0:LRU_WIDTH].reshape(seg, SUBLANES, LRU_WIDTH)
    tail = cv_ref[0]
    sub = lax.broadcasted_iota(jnp.int32, (SUBLANES, LRU_WIDTH), 0)
    wrap = []
    for i in range(CONV_W - 1):
        prev = pltpu.roll(u3[seg - (CONV_W - 1) + i], 1, axis=0)
        fill = tail[SUBLANES - (CONV_W - 1) + i:SUBLANES - (CONV_W - 1) + i + 1, :]
        wrap.append(jnp.where(sub == 0, fill, prev))
        cv_ref[0, SUBLANES - (CONV_W - 1) + i:SUBLANES - (CONV_W - 1) + i + 1, :] = (
            u3[seg - (CONV_W - 1) + i][SUBLANES - 1:SUBLANES, :])
    ext = jnp.concatenate([jnp.stack(wrap, axis=0), u3], axis=0)
    conv3 = convb_ref[...] + ext[CONV_W - 1:] * convw_ref[CONV_W - 1:CONV_W, :]
    for j in range(1, CONV_W):
        conv3 = conv3 + ext[CONV_W - 1 - j:CONV_W - 1 - j + seg] * convw_ref[CONV_W - 1 - j:CONV_W - j, :]

    conv = conv3.reshape(tt, LRU_WIDTH)
    bounds = [min(k * zc, Z_NAT) for k in range(-(-Z_NAT // zc) + 1)]
    zcols = list(zip(bounds[:-1], bounds[1:]))

    def project(k):
        lo, hi = zcols[k]
        z_s[:, lo:hi] = _dot(xnb_s[...], wnat_ref[:, lo:hi])

    project(0)
    nrow = len(zcols) - 1
    step = tt // nrow
    a_parts, x_parts = [], []
    for c in range(nrow):
        r = slice(c * step, (c + 1) * step)
        a_c, x_c = _lru_gates(conv[r], wai_ref, ba_ref, bi_ref, lam_ref)
        a_parts.append(a_c)
        x_parts.append(x_c)
        project(c + 1)
    a = jnp.concatenate(a_parts, axis=0)
    xin = jnp.concatenate(x_parts, axis=0)

    a3 = a.reshape(seg, SUBLANES, LRU_WIDTH)
    x3 = xin.reshape(seg, SUBLANES, LRU_WIDTH)
    hs = [x3[0]]
    ps = [a3[0]]
    for j in range(1, seg):
        hs.append(a3[j] * hs[-1] + x3[j])
        ps.append(a3[j] * ps[-1])
    carry = h_ref[0]
    cin = []
    for s in range(SUBLANES):
        cin.append(carry)
        carry = hs[-1][s:s + 1, :] + ps[-1][s:s + 1, :] * carry
    h_ref[0] = carry
    cin = jnp.concatenate(cin, axis=0)
    h3 = jnp.stack([hs[j] + ps[j] * cin for j in range(seg)], axis=0)

    hl = (jax.nn.gelu(zxy_s[:, LRU_WIDTH:]) * h3.reshape(tt, LRU_WIDTH)).astype(BF16)
    ol3 = _dot(hl, wout_ref[ML_WIDTH:ML_WIDTH + LRU_WIDTH, :]).reshape(seg, SUBLANES, D_MODEL)
    for j in range(seg):
        for k in range(D_MODEL // LANES):
            obuf[k, pl.ds(j, SUBLANES, stride=pitch), :] = ol3[j][:, k * LANES:(k + 1) * LANES]
    out_lru = jnp.concatenate(
        [jnp.concatenate([obuf[k, pl.ds(s * pitch, seg), :] for s in range(SUBLANES)], axis=0)
         for k in range(D_MODEL // LANES)], axis=1)

    _mlstm_blocks(z_s, gbias_ref, onorm_ref, c_ref, n_ref, m_ref, hm_s, tt, cl)

    y_ref[...] = x_ref[...] + _dot(hm_s[...], wout_ref[0:ML_WIDTH, :]) + out_lru


def _pmixer_kernel(x_ref, gmix_ref, wnat_ref, wxy_ref, gbias_ref, onorm_ref, convw_ref,
                   convb_ref, wai_ref, ba_ref, bi_ref, lam_ref, wout_ref,
                   y_ref, c_ref, n_ref, m_ref, h_ref, cv_ref,
                   z_s, zxy_s, pbuf, obuf, hm_s, xnb_s, *, cl, tt, zc):
    @pl.when(pl.program_id(1) == 0)
    def _():
        c_ref[...] = jnp.zeros_like(c_ref)
        n_ref[...] = jnp.zeros_like(n_ref)
        m_ref[...] = jnp.zeros_like(m_ref)
        h_ref[...] = jnp.zeros_like(h_ref)
        cv_ref[...] = jnp.zeros_like(cv_ref)

    _pmixer_project(x_ref, gmix_ref, wxy_ref, zxy_s, pbuf, xnb_s, tt=tt)
    _pmixer_mix(x_ref, xnb_s, wnat_ref, z_s, zxy_s, gbias_ref, onorm_ref, convw_ref, convb_ref, wai_ref, ba_ref,
                bi_ref, lam_ref, wout_ref, y_ref, c_ref, n_ref, m_ref, h_ref, cv_ref, obuf, hm_s,
                cl=cl, tt=tt, zc=zc)


def _pmixer(x, p, l, *, nb, t, tt, cl, zc=Z_CHUNK):
    nt = t // tt
    pitch = tt // SUBLANES + SUBLANES
    tok = lambda b, i: (b * nt + i, 0)
    st4 = lambda b, i: (b, 0, 0, 0)
    st3 = lambda b, i: (b, 0, 0)
    lay3 = lambda b, i: (l, 0, 0)
    lay4 = lambda b, i: (l, 0, 0, 0)
    return pl.pallas_call(
        functools.partial(_pmixer_kernel, cl=cl, tt=tt, zc=zc),
        out_shape=(
            jax.ShapeDtypeStruct((nb * t, D_MODEL), F32),
            jax.ShapeDtypeStruct((nb, ML_HEADS, ML_DV, ML_DK), F32),
            jax.ShapeDtypeStruct((nb, ML_HEADS, ML_DK), F32),
            jax.ShapeDtypeStruct((nb, 1, LANES), F32),
            jax.ShapeDtypeStruct((nb, 1, LRU_WIDTH), F32),
            jax.ShapeDtypeStruct((nb, SUBLANES, LRU_WIDTH), F32),
        ),
        grid=(nb, nt),
        in_specs=[
            pl.BlockSpec((tt, D_MODEL), tok),
            _resident((None, 1, D_MODEL), lay3),
            _resident((None, D_MODEL, Z_NAT), lay3),
            _resident((None, D_MODEL, Z_XY), lay3),
            _resident((None, 1, LANES), lay3),
            _resident((None, ML_HEADS, ML_DV), lay3),
            _resident((None, CONV_W, LRU_WIDTH), lay3),
            _resident((None, 1, LRU_WIDTH), lay3),
            _resident((None, LRU_BLOCKS, LRU_BW, 2 * LRU_BW), lay4),
            _resident((None, 1, LRU_WIDTH), lay3),
            _resident((None, 1, LRU_WIDTH), lay3),
            _resident((None, 1, LRU_WIDTH), lay3),
            _resident((None, ML_WIDTH + LRU_WIDTH, D_MODEL), lay3),
        ],
        out_specs=(
            pl.BlockSpec((tt, D_MODEL), tok),
            pl.BlockSpec((1, ML_HEADS, ML_DV, ML_DK), st4),
            pl.BlockSpec((1, ML_HEADS, ML_DK), st3),
            pl.BlockSpec((1, 1, LANES), st3),
            pl.BlockSpec((1, 1, LRU_WIDTH), st3),
            pl.BlockSpec((1, SUBLANES, LRU_WIDTH), st3),
        ),
        scratch_shapes=[pltpu.VMEM((tt, Z_NAT), F32), pltpu.VMEM((tt, Z_XY), F32),
                        pltpu.VMEM((LRU_WIDTH // LANES, SUBLANES * pitch, LANES), F32),
                        pltpu.VMEM((D_MODEL // LANES, SUBLANES * pitch, LANES), F32),
                        pltpu.VMEM((tt, ML_WIDTH), BF16), pltpu.VMEM((tt, D_MODEL), BF16)],
        compiler_params=_params("parallel", "arbitrary"),
        name="pmixer",
    )(x, p["mix_norm"], p["w_in"], p["w_xy"], p["gbias"], p["ml_out_norm"], p["lru_conv_w"],
      p["lru_conv_b"], p["w_ai"], p["lru_b_a"], p["lru_b_i"], p["lru_lambda"], p["w_out"])


def _zproj_kernel(x_ref, g_ref, wnat_ref, wxy_ref, znat_ref, zxy_ref):
    xn = _rms(x_ref[...], g_ref[...]).astype(BF16)
    znat_ref[...] = _dot(xn, wnat_ref[...])
    zxy_ref[...] = _dot(xn, wxy_ref[...])


def _zproj(x, p, l):
    m = x.shape[0]
    lay3 = lambda i: (l, 0, 0)
    return pl.pallas_call(
        _zproj_kernel,
        out_shape=(jax.ShapeDtypeStruct((m, Z_NAT), F32), jax.ShapeDtypeStruct((m, Z_XY), F32)),
        grid=(1,),
        in_specs=[
            pl.BlockSpec((m, D_MODEL), lambda i: (0, 0)),
            pl.BlockSpec((None, 1, D_MODEL), lay3),
            pl.BlockSpec((None, D_MODEL, Z_NAT), lay3),
            pl.BlockSpec((None, D_MODEL, Z_XY), lay3),
        ],
        out_specs=(pl.BlockSpec((m, Z_NAT), lambda i: (0, 0)), pl.BlockSpec((m, Z_XY), lambda i: (0, 0))),
        compiler_params=_params("arbitrary"),
        name="zproj",
    )(x, p["mix_norm"], p["w_in"], p["w_xy"])


def _smixer_kernel(znat_ref, zxy_ref, x_ref, c0_ref, n0_ref, m0_ref, h0_ref, cv0_ref,
                   gbias_ref, onorm_ref, convw_ref, convb_ref, wai_ref, ba_ref, bi_ref, lam_ref,
                   wout_ref,
                   y_ref, c_ref, n_ref, m_ref, h_ref, cv_ref,
                   ubuf_s, hm_s, *, ns, t):
    c_ref[...] = c0_ref[...]
    n_ref[...] = n0_ref[...]
    m_ref[...] = m0_ref[...]
    pad = t + SUBLANES

    convs = []
    for s in range(ns):
        one = pl.ds(s, 1)
        _mlstm_blocks(znat_ref, gbias_ref, onorm_ref, c_ref.at[one], n_ref.at[one], m_ref.at[one], hm_s,
                      t, t, base=s * t)
        u = zxy_ref[s * t:(s + 1) * t, 0:LRU_WIDTH]
        ubuf_s[s * pad:s * pad + SUBLANES, :] = cv0_ref[s]
        ubuf_s[s * pad + SUBLANES:(s + 1) * pad, :] = u
        cv_ref[s] = ubuf_s[s * pad + t:(s + 1) * pad, :]
        conv = convb_ref[...] + u * convw_ref[CONV_W - 1:CONV_W, :]
        for j in range(1, CONV_W):
            conv = conv + (ubuf_s[pl.ds(s * pad + SUBLANES - j, t), :]
                           * convw_ref[CONV_W - 1 - j:CONV_W - j, :])
        convs.append(conv)
    a, xin = _lru_gates(jnp.concatenate(convs, axis=0), wai_ref, ba_ref, bi_ref, lam_ref)

    rmod = lax.broadcasted_iota(jnp.int32, (ns * t, LRU_WIDTH), 0) & (SUBLANES - 1)
    d = 1
    while d < SUBLANES:
        keep = rmod >= d
        xin = xin + a * jnp.where(keep, pltpu.roll(xin, d, axis=0), 0.0)
        a = a * jnp.where(keep, pltpu.roll(a, d, axis=0), 1.0)
        d *= 2
    groups = []
    for s in range(ns):
        carry = h0_ref[s]
        for gidx in range(t // SUBLANES):
            lo = s * t + gidx * SUBLANES
            hb = xin[lo:lo + SUBLANES, :] + a[lo:lo + SUBLANES, :] * carry
            carry = hb[SUBLANES - 1:SUBLANES, :]
            groups.append(hb)
        h_ref[s] = carry
    h_lru = jnp.concatenate(groups, axis=0)

    hl = (jax.nn.gelu(zxy_ref[:, LRU_WIDTH:]) * h_lru).astype(BF16)
    out = (_dot(hm_s[...], wout_ref[0:ML_WIDTH, :])
           + _dot(hl, wout_ref[ML_WIDTH:ML_WIDTH + LRU_WIDTH, :]))
    y_ref[...] = x_ref[...] + out


def _smixer(znat, zxy, x, st, p, l, *, nb, t, ns):
    tok = lambda b: (b, 0)
    lay3 = lambda b: (l, 0, 0)
    lay4 = lambda b: (l, 0, 0, 0)
    st3 = lambda b: (b, 0, 0)
    lst = lambda b: (l, b, 0, 0)
    c0, n0, m0, h0, cv0 = st
    return pl.pallas_call(
        functools.partial(_smixer_kernel, ns=ns, t=t),
        out_shape=(
            jax.ShapeDtypeStruct((nb * t, D_MODEL), F32),
            jax.ShapeDtypeStruct((nb, ML_HEADS, ML_DV, ML_DK), F32),
            jax.ShapeDtypeStruct((nb, ML_HEADS, ML_DK), F32),
            jax.ShapeDtypeStruct((nb, 1, LANES), F32),
            jax.ShapeDtypeStruct((nb, 1, LRU_WIDTH), F32),
            jax.ShapeDtypeStruct((nb, SUBLANES, LRU_WIDTH), F32),
        ),
        grid=(nb // ns,),
        in_specs=[
            pl.BlockSpec((ns * t, Z_NAT), tok),
            pl.BlockSpec((ns * t, Z_XY), tok),
            pl.BlockSpec((ns * t, D_MODEL), tok),
            pl.BlockSpec((None, ns, ML_HEADS, ML_DV, ML_DK), lambda b: (l, b, 0, 0, 0)),
            pl.BlockSpec((None, ns, ML_HEADS, ML_DK), lst),
            pl.BlockSpec((None, ns, 1, LANES), lst),
            pl.BlockSpec((None, ns, 1, LRU_WIDTH), lst),
            pl.BlockSpec((None, ns, SUBLANES, LRU_WIDTH), lst),
            pl.BlockSpec((None, 1, LANES), lay3),
            pl.BlockSpec((None, ML_HEADS, ML_DV), lay3),
            pl.BlockSpec((None, CONV_W, LRU_WIDTH), lay3),
            pl.BlockSpec((None, 1, LRU_WIDTH), lay3),
            pl.BlockSpec((None, LRU_BLOCKS, LRU_BW, 2 * LRU_BW), lay4),
            pl.BlockSpec((None, 1, LRU_WIDTH), lay3),
            pl.BlockSpec((None, 1, LRU_WIDTH), lay3),
            pl.BlockSpec((None, 1, LRU_WIDTH), lay3),
            pl.BlockSpec((None, ML_WIDTH + LRU_WIDTH, D_MODEL), lay3),
        ],
        out_specs=(
            pl.BlockSpec((ns * t, D_MODEL), tok),
            pl.BlockSpec((ns, ML_HEADS, ML_DV, ML_DK), lambda b: (b, 0, 0, 0)),
            pl.BlockSpec((ns, ML_HEADS, ML_DK), st3),
            pl.BlockSpec((ns, 1, LANES), st3),
            pl.BlockSpec((ns, 1, LRU_WIDTH), st3),
            pl.BlockSpec((ns, SUBLANES, LRU_WIDTH), st3),
        ),
        scratch_shapes=[pltpu.VMEM((ns * (t + SUBLANES), LRU_WIDTH), F32),
                        pltpu.VMEM((ns * t, ML_WIDTH), BF16)],
        compiler_params=_params("parallel"),
        name="smixer",
    )(znat, zxy, x, c0, n0, m0, h0, cv0, p["gbias"], p["ml_out_norm"], p["lru_conv_w"],
      p["lru_conv_b"], p["w_ai"], p["lru_b_a"], p["lru_b_i"], p["lru_lambda"], p["w_out"])


def _xattn_kernel(x_ref, g_ref, wq_ref, mk_ref, mv_ref, wo_ref, y_ref, o_s, *, ns, rows, split):
    step = rows // split
    chunks = [(s, slice(s * rows + c * step, s * rows + (c + 1) * step)) for s in range(ns) for c in range(split)]
    heads = [slice(h * XA_DH, (h + 1) * XA_DH) for h in range(XA_HEADS)]

    def scores(s, r):
        q = _dot(_rms(x_ref[r, :], g_ref[...]).astype(BF16), wq_ref[...])
        return [_nt_dot(q[:, c].astype(BF16), mk_ref[s, :, c].astype(BF16)) * (XA_DH ** -0.5) for c in heads]

    def attend(s, r, sc):
        for c, sch in zip(heads, sc):
            p = jnp.exp(sch - jnp.max(sch, axis=1, keepdims=True))
            p = p / jnp.sum(p, axis=1, keepdims=True)
            o_s[r, c] = _dot(p.astype(BF16), mv_ref[s, :, c].astype(BF16)).astype(BF16)

    def project(r):
        y_ref[r, :] = x_ref[r, :] + _dot(o_s[r, :], wo_ref[...])

    pending = None
    for s, r in chunks:
        sc = scores(s, r)
        if pending is not None:
            ps, pr, psc = pending
            attend(ps, pr, psc)
            project(pr)
        pending = (s, r, sc)
    ps, pr, psc = pending
    attend(ps, pr, psc)
    project(pr)


def _xattn(x, g, wq, mk, mv, wo, l, *, nb, t, tt, ns=1, split=1):
    nt = t // tt
    assert ns == 1 or nt == 1
    tok = lambda b, i: (b * nt + i, 0)
    lay3 = lambda b, i: (l, 0, 0)
    mem = lambda b, i: (l, b, 0, 0)
    return pl.pallas_call(
        functools.partial(_xattn_kernel, ns=ns, rows=tt, split=split),
        out_shape=jax.ShapeDtypeStruct((nb * t, D_MODEL), F32),
        grid=(nb // ns, nt),
        in_specs=[
            pl.BlockSpec((ns * tt, D_MODEL), tok),
            pl.BlockSpec((None, 1, D_MODEL), lay3),
            pl.BlockSpec((None, D_MODEL, D_MODEL), lay3),
            pl.BlockSpec((None, ns, MEM_LEN, D_MODEL), mem),
            pl.BlockSpec((None, ns, MEM_LEN, D_MODEL), mem),
            pl.BlockSpec((None, D_MODEL, D_MODEL), lay3),
        ],
        out_specs=pl.BlockSpec((ns * tt, D_MODEL), tok),
        scratch_shapes=[pltpu.VMEM((ns * tt, D_MODEL), BF16)],
        compiler_params=_params("parallel", "arbitrary"),
        name="xattn",
    )(x, g, wq, mk, mv, wo)


def _memkv_kernel(mem_ref, g_ref, wk_ref, wv_ref, k_ref, v_ref, kb_ref, vb_ref):
    mn = _rms(mem_ref[...], g_ref[...]).astype(BF16)
    k = _dot(mn, wk_ref[...])
    v = _dot(mn, wv_ref[...])
    k_ref[...] = k
    v_ref[...] = v
    kb_ref[...] = k.astype(BF16)
    vb_ref[...] = v.astype(BF16)


def _memkv(mem, g, wk, wv):
    nb = mem.shape[0]
    wspec = pl.BlockSpec((None, D_MODEL, D_MODEL), lambda l, b: (l, 0, 0))
    ospec = pl.BlockSpec((None, None, MEM_LEN, D_MODEL), lambda l, b: (l, b, 0, 0))
    out = jax.ShapeDtypeStruct((DEPTH, nb, MEM_LEN, D_MODEL), F32)
    outb = jax.ShapeDtypeStruct((DEPTH, nb, MEM_LEN, D_MODEL), BF16)
    return pl.pallas_call(
        _memkv_kernel,
        out_shape=(out, out, outb, outb),
        grid=(DEPTH, nb),
        in_specs=[pl.BlockSpec((None, MEM_LEN, D_MODEL), lambda l, b: (b, 0, 0)),
                  pl.BlockSpec((None, 1, D_MODEL), lambda l, b: (l, 0, 0)), wspec, wspec],
        out_specs=(ospec, ospec, ospec, ospec),
        compiler_params=_params("parallel", "parallel"),
        name="memkv",
    )(mem, g, wk, wv)


def kernel(x_prompt, x_sample, mem_prompt, state_mlstm_C, state_mlstm_n, state_mlstm_m, state_lru_h,
           state_lru_conv, cache_mem_k, cache_mem_v, ffn1_norm, ffn1_w_gate, ffn1_w_up, ffn1_w_down,
           mix_norm, w_in, ml_b_i, ml_b_f, ml_out_norm, lru_conv_w, lru_conv_b, lru_w_a, lru_b_a,
           lru_w_i, lru_b_i, lru_lambda, w_out, xattn_norm, mem_norm, xattn_w_q, xattn_w_k, xattn_w_v,
           xattn_w_o, ffn2_norm, ffn2_w_gate, ffn2_w_up, ffn2_w_down, final_norm):
    bp, tp, _ = x_prompt.shape
    bs, ts, _ = x_sample.shape
    xp = x_prompt.reshape(bp * tp, D_MODEL)
    xs = x_sample.reshape(bs * ts, D_MODEL)

    bf = lambda w: w.astype(BF16)
    row = lambda v: v.astype(F32).reshape(DEPTH, 1, -1)
    gate_pad = ((0, 0), (0, 0), (0, LANES - 2 * ML_HEADS))
    w_in_b = bf(w_in)
    p = {
        "mix_norm": row(mix_norm),
        "w_in": w_in_b,
        "w_xy": w_in_b[:, :, W_X:],
        "gbias": jnp.pad(jnp.concatenate([ml_b_i, ml_b_f], axis=1).astype(F32)[:, None, :], gate_pad),
        "ml_out_norm": ml_out_norm.astype(F32),
        "lru_conv_w": lru_conv_w, "lru_conv_b": row(lru_conv_b),
        "w_ai": bf(jnp.concatenate([lru_w_a, lru_w_i], axis=-1)),
        "lru_b_a": row(lru_b_a), "lru_b_i": row(lru_b_i), "lru_lambda": row(lru_lambda),
        "w_out": bf(w_out),
    }
    ffn1 = (row(ffn1_norm), bf(ffn1_w_gate), bf(ffn1_w_up), bf(ffn1_w_down))
    ffn2 = (row(ffn2_norm), bf(ffn2_w_gate), bf(ffn2_w_up), bf(ffn2_w_down))
    xa_g, xa_q, xa_o = row(xattn_norm), bf(xattn_w_q), bf(xattn_w_o)
    fin = final_norm.astype(F32).reshape(1, D_MODEL)

    pk, pv, pkb, pvb = _memkv(mem_prompt, row(mem_norm), bf(xattn_w_k), bf(xattn_w_v))
    sk = bf(cache_mem_k.reshape(DEPTH, bs, MEM_LEN, D_MODEL))
    sv = bf(cache_mem_v.reshape(DEPTH, bs, MEM_LEN, D_MODEL))
    s_state = (
        state_mlstm_C.astype(F32), state_mlstm_n.astype(F32),
        jnp.pad(state_mlstm_m.astype(F32)[:, :, None, :], ((0, 0), (0, 0), (0, 0), (0, LANES - ML_HEADS))),
        state_lru_h.astype(F32)[:, :, None, :],
        jnp.pad(state_lru_conv.astype(F32), ((0, 0), (0, 0), (SUBLANES - (CONV_W - 1), 0), (0, 0))),
    )

    p_out = [[] for _ in range(5)]
    s_out = [[] for _ in range(5)]
    for l in range(DEPTH):
        last = l == DEPTH - 1
        xp, xs = _ffn(xp, xs, *ffn1, fin, l, tm=512, tf=256, final_norm=False)
        xp, *st = _pmixer(xp, p, l, nb=bp, t=tp, tt=512, cl=ML_BLOCK)
        xp = _xattn(xp, xa_g, xa_q, pkb, pvb, xa_o, l, nb=bp, t=tp, tt=1024, split=4)
        for acc, v in zip(p_out, st):
            acc.append(v)
        znat, zxy = _zproj(xs, p, l)
        xs, *st = _smixer(znat, zxy, xs, s_state, p, l, nb=bs, t=ts, ns=4)
        xs = _xattn(xs, xa_g, xa_q, sk, sv, xa_o, l, nb=bs, t=ts, tt=ts, ns=4)
        for acc, v in zip(s_out, st):
            acc.append(v)
        xp, xs = _ffn(xp, xs, *ffn2, fin, l, tm=512, tf=256, final_norm=last)

    def states(acc):
        c, n, m, h, cv = (jnp.stack(a) for a in acc)
        return c, n, m[:, :, 0, :ML_HEADS], h[:, :, 0, :], cv[:, :, SUBLANES - (CONV_W - 1):, :]

    return (xp.reshape(bp, tp, D_MODEL), xs.reshape(bs, ts, D_MODEL),
            *states(p_out),
            pk.reshape(DEPTH, bp, MEM_LEN, XA_HEADS, XA_DH), pv.reshape(DEPTH, bp, MEM_LEN, XA_HEADS, XA_DH),
            *states(s_out))
```
